```python
import math
import jax, jax.numpy as jnp
from jax import lax
import numpy as np

D_MODEL = 1024
BATCH = 4
SEQ = 4096
DEPTH = 1

HEAD_DIM = 64
N_Q_HEADS = 8
N_KV_HEADS = 2
GROUP = N_Q_HEADS // N_KV_HEADS
ATT_WIDTH = N_Q_HEADS * HEAD_DIM
KV_WIDTH = N_KV_HEADS * HEAD_DIM
WINDOW = 128
BLOCK = 128
CONV_WIDTH = D_MODEL // 2
CONV_K = 3
D_FF = 2816
N_BUCKETS = 32
MAX_DISTANCE = 128
LN_EPS = 1e-5
DEEPNORM_ALPHA = (2 * DEPTH) ** 0.25
DEEPNORM_BETA = (8 * DEPTH) ** -0.25
IN_PROJ_WIDTH = ATT_WIDTH + 2 * KV_WIDTH + 3 * CONV_WIDTH + 2 * D_MODEL
MASK_VALUE = -1e30

kernel_name = "hybrid_swa_shortconv_convffn_deepnorm"


def layer_norm(x, g, b):
    xf = x.astype(jnp.float32)
    mu = jnp.mean(xf, axis=-1, keepdims=True)
    var = jnp.mean(jnp.square(xf - mu), axis=-1, keepdims=True)
    return ((xf - mu) * lax.rsqrt(var + LN_EPS)).astype(x.dtype) * g + b


def dwconv3(x, w, b):
    xp = jnp.pad(x, ((0, 0), (1, 1), (0, 0)))
    return xp[:, :-2] * w[0] + xp[:, 1:-1] * w[1] + xp[:, 2:] * w[2] + b


def relative_bucket(rel):
    half = N_BUCKETS // 2
    max_exact = half // 2
    offset = jnp.where(rel > 0, half, 0)
    n = jnp.abs(rel)
    nf = jnp.maximum(n, 1).astype(jnp.float32)
    large = max_exact + (jnp.log(nf / max_exact) / math.log(MAX_DISTANCE / max_exact)
                         * (half - max_exact)).astype(jnp.int32)
    large = jnp.minimum(large, half - 1)
    return offset + jnp.where(n < max_exact, n, large)


def windowed_gqa(q, k, v, sink, rel_bias):
    bsz, seq = q.shape[0], q.shape[1]
    nb = seq // BLOCK
    qb = q.reshape(bsz, nb, BLOCK, N_KV_HEADS, GROUP, HEAD_DIM)

    def band(t):
        t = t.reshape(bsz, seq, N_KV_HEADS, HEAD_DIM)
        tp = jnp.pad(t, ((0, 0), (BLOCK, BLOCK), (0, 0), (0, 0)))
        tb = tp.reshape(bsz, nb + 2, BLOCK, N_KV_HEADS, HEAD_DIM)
        return jnp.concatenate([tb[:, :-2], tb[:, 1:-1], tb[:, 2:]], axis=2)

    kb, vb = band(k), band(v)
    scale = HEAD_DIM ** -0.5
    scores = jnp.einsum('bnqkgd,bnckd->bnkgqc', qb, kb).astype(jnp.float32) * scale

    qi = jnp.arange(BLOCK)[:, None]
    kc = jnp.arange(3 * BLOCK)[None, :]
    rel = kc - BLOCK - qi
    bias = rel_bias.astype(jnp.float32)[relative_bucket(rel)]
    bias = jnp.transpose(bias, (2, 0, 1)).reshape(N_KV_HEADS, GROUP, BLOCK, 3 * BLOCK)
    kpos = jnp.arange(nb)[:, None] * BLOCK - BLOCK + jnp.arange(3 * BLOCK)[None, :]
    valid = ((jnp.abs(rel) <= WINDOW)[None]
             & ((kpos >= 0) & (kpos < seq))[:, None, :])
    logits = jnp.where(valid[None, :, None, None], scores + bias, MASK_VALUE)

    sink_l = jnp.broadcast_to(sink.astype(jnp.float32).reshape(N_KV_HEADS, GROUP, 1, 1),
                              logits.shape[:-1] + (1,))
    probs = jax.nn.softmax(jnp.concatenate([logits, sink_l], axis=-1), axis=-1)[..., :-1]
    out = jnp.einsum('bnkgqc,bnckd->bnqkgd', probs.astype(vb.dtype), vb)
    return out.reshape(bsz, seq, ATT_WIDTH)


def hybrid_layer(h, w_in, b_gates, attn_sink, rel_bias, conv_w, conv_b,
                 w_att_branch, w_conv_branch, w_o, ln_mix_g, ln_mix_b,
                 w_ffn_up, ffn_conv_w, ffn_conv_b, w_ffn_down, ln_ffn_g, ln_ffn_b):
    proj = h @ w_in
    q, k, v, cb, cc, cx, gates = jnp.split(
        proj,
        np.cumsum([ATT_WIDTH, KV_WIDTH, KV_WIDTH, CONV_WIDTH, CONV_WIDTH, CONV_WIDTH]).tolist(),
        axis=-1)
    att = windowed_gqa(q, k, v, attn_sink, rel_bias)
    conv = cb * dwconv3(cc * cx, conv_w, conv_b)
    g = jax.nn.sigmoid(gates + b_gates)
    g_att, g_conv = g[..., :D_MODEL], g[..., D_MODEL:]
    merged = g_att * (att @ w_att_branch) + g_conv * (conv @ w_conv_branch)
    h = layer_norm(DEEPNORM_ALPHA * h + merged @ w_o, ln_mix_g, ln_mix_b)
    up = dwconv3(h @ w_ffn_up, ffn_conv_w, ffn_conv_b)
    a, u = up[..., :D_FF], up[..., D_FF:]
    ffn = (jax.nn.silu(a) * u) @ w_ffn_down
    return layer_norm(DEEPNORM_ALPHA * h + ffn, ln_ffn_g, ln_ffn_b)


def setup_inputs(seed: int = 0) -> dict:
    key = jax.random.key(seed)
    ks = jax.random.split(key, 24)
    f32 = jnp.float32
    L = DEPTH
    beta = DEEPNORM_BETA

    def nrm(k, shape, scale):
        return jax.random.normal(k, shape, f32) * scale

    col_scale = jnp.concatenate([
        jnp.ones((ATT_WIDTH + KV_WIDTH,), f32), jnp.full((KV_WIDTH,), beta, f32),
        jnp.ones((2 * CONV_WIDTH,), f32), jnp.full((CONV_WIDTH,), beta, f32),
        jnp.ones((2 * D_MODEL,), f32)])
    return {
        "x": nrm(ks[0], (BATCH, SEQ, D_MODEL), 1.0),
        "ln_in_g": 1.0 + nrm(ks[1], (D_MODEL,), 0.02),
        "ln_in_b": nrm(ks[2], (D_MODEL,), 0.02),
        "w_in": nrm(ks[3], (L, D_MODEL, IN_PROJ_WIDTH), D_MODEL ** -0.5) * col_scale,
        "b_gates": nrm(ks[4], (L, 2 * D_MODEL), 0.1),
        "attn_sink": nrm(ks[5], (L, N_Q_HEADS), 0.5),
        "rel_bias": nrm(ks[6], (N_BUCKETS, N_Q_HEADS), 0.2),
        "conv_w": nrm(ks[7], (L, CONV_K, CONV_WIDTH), CONV_K ** -0.5),
        "conv_b": nrm(ks[8], (L, CONV_WIDTH), 0.02),
        "w_att_branch": nrm(ks[9], (L, ATT_WIDTH, D_MODEL), beta * ATT_WIDTH ** -0.5),
        "w_conv_branch": nrm(ks[10], (L, CONV_WIDTH, D_MODEL), beta * CONV_WIDTH ** -0.5),
        "w_o": nrm(ks[11], (L, D_MODEL, D_MODEL), beta * D_MODEL ** -0.5),
        "ln_mix_g": 1.0 + nrm(ks[12], (L, D_MODEL), 0.02),
        "ln_mix_b": nrm(ks[13], (L, D_MODEL), 0.02),
        "w_ffn_up": nrm(ks[14], (L, D_MODEL, 2 * D_FF), beta * D_MODEL ** -0.5),
        "ffn_conv_w": nrm(ks[15], (L, CONV_K, 2 * D_FF), CONV_K ** -0.5),
        "ffn_conv_b": nrm(ks[16], (L, 2 * D_FF), 0.02),
        "w_ffn_down": nrm(ks[17], (L, D_FF, D_MODEL), beta * D_FF ** -0.5),
        "ln_ffn_g": 1.0 + nrm(ks[18], (L, D_MODEL), 0.02),
        "ln_ffn_b": nrm(ks[19], (L, D_MODEL), 0.02),
    }


def reference(x, ln_in_g, ln_in_b, w_in, b_gates, attn_sink, rel_bias, conv_w, conv_b,
              w_att_branch, w_conv_branch, w_o, ln_mix_g, ln_mix_b,
              w_ffn_up, ffn_conv_w, ffn_conv_b, w_ffn_down, ln_ffn_g, ln_ffn_b):
    h = layer_norm(x, ln_in_g, ln_in_b)
    for l in range(DEPTH):
        h = hybrid_layer(h, w_in[l], b_gates[l], attn_sink[l], rel_bias, conv_w[l], conv_b[l],
                         w_att_branch[l], w_conv_branch[l], w_o[l], ln_mix_g[l], ln_mix_b[l],
                         w_ffn_up[l], ffn_conv_w[l], ffn_conv_b[l], w_ffn_down[l],
                         ln_ffn_g[l], ln_ffn_b[l])
    return h
```

```python
import functools
import math

import jax
import jax.numpy as jnp
import numpy as np
from jax import lax
from jax.experimental import pallas as pl
from jax.experimental.pallas import tpu as pltpu

D_MODEL = 1024
HEAD_DIM = 64
N_Q_HEADS = 8
N_KV_HEADS = 2
GROUP = N_Q_HEADS // N_KV_HEADS
ATT_WIDTH = N_Q_HEADS * HEAD_DIM
KV_WIDTH = N_KV_HEADS * HEAD_DIM
WINDOW = 128
BLOCK = 128
CONV_WIDTH = D_MODEL // 2
D_FF = 2816
N_BUCKETS = 32
MAX_DISTANCE = 128
LN_EPS = 1e-5
DEPTH = 1
DEEPNORM_ALPHA = (2 * DEPTH) ** 0.25
MASK_VALUE = -1e30

Q_OFF = 0
K_OFF = ATT_WIDTH
V_OFF = K_OFF + KV_WIDTH
CB_OFF = V_OFF + KV_WIDTH
CC_OFF = CB_OFF + CONV_WIDTH
CX_OFF = CC_OFF + CONV_WIDTH
GATE_OFF = CX_OFF + CONV_WIDTH
IN_PROJ_WIDTH = GATE_OFF + 2 * D_MODEL

TILE = 512
QB_PER_TILE = TILE // BLOCK
CONV_HALO = 16
FF_CHUNK = 256
N_FF_CHUNKS = D_FF // FF_CHUNK
VMEM_LIMIT_BYTES = 60 * 1024 * 1024

F32 = jnp.float32
BF16 = jnp.bfloat16


def _layer_norm(x, g, b):
    mu = jnp.mean(x, axis=-1, keepdims=True)
    xc = x - mu
    var = jnp.mean(xc * xc, axis=-1, keepdims=True)
    return xc * lax.rsqrt(var + LN_EPS) * g + b


def _dot(a, b):
    return jnp.dot(a, b, preferred_element_type=F32)


def _dot_nt(a, b):
    return lax.dot_general(a, b, (((1,), (1,)), ((), ())), preferred_element_type=F32)


def _bucket_ids():
    qi = jnp.arange(BLOCK)[:, None]
    kc = jnp.arange(3 * BLOCK)[None, :]
    rel = kc - BLOCK - qi
    half = N_BUCKETS // 2
    max_exact = half // 2
    offset = jnp.where(rel > 0, half, 0)
    n = jnp.abs(rel)
    nf = jnp.maximum(n, 1).astype(jnp.float32)
    large = max_exact + (jnp.log(nf / max_exact) / math.log(MAX_DISTANCE / max_exact)
                         * (half - max_exact)).astype(jnp.int32)
    large = jnp.minimum(large, half - 1)
    bucket = offset + jnp.where(n < max_exact, n, large)
    return jnp.where(n <= WINDOW, bucket, -1).astype(jnp.int32)


def _bias_table_kernel(ids_ref, rb_ref, out_ref):
    ids = ids_ref[...]
    col = lax.broadcasted_iota(jnp.int32, ids.shape, 1)
    for h in range(N_Q_HEADS):
        t = jnp.full(ids.shape, MASK_VALUE, F32)
        for bkt in range(N_BUCKETS):
            t = jnp.where(ids == bkt, rb_ref[bkt, h], t)
        out_ref[0, h] = t
        out_ref[1, h] = jnp.where(col < BLOCK, MASK_VALUE, t)
        out_ref[2, h] = jnp.where(col >= 2 * BLOCK, MASK_VALUE, t)


def _bias_table(rel_bias):
    ids = _bucket_ids()
    return pl.pallas_call(
        _bias_table_kernel,
        out_shape=jax.ShapeDtypeStruct((3, N_Q_HEADS, BLOCK, 3 * BLOCK), F32),
        in_specs=[pl.BlockSpec(memory_space=pltpu.VMEM), pl.BlockSpec(memory_space=pltpu.SMEM)],
        out_specs=pl.BlockSpec(memory_space=pltpu.VMEM),
        name="bias_table",
    )(ids, rel_bias)


def _mixer_kernel(xp_ref, xc_ref, xn_ref, lng_ref, lnb_ref, win_ref, bg_ref, sink_ref, bias_ref,
                  cw_ref, cbias_ref, wa_ref, wc_ref, wo_ref, g2_ref, b2_ref, out_ref,
                  hext, hres, q_scr, kvar, vvar, att_scr, u_scr):
    i = pl.program_id(1)
    last_i = pl.num_programs(1) - 1
    lng = lng_ref[...]
    lnb = lnb_ref[...]

    hc = _layer_norm(xc_ref[0], lng, lnb)
    hres[...] = hc
    hext[0:BLOCK] = _layer_norm(xp_ref[0], lng, lnb).astype(BF16)
    hext[BLOCK:BLOCK + TILE] = hc.astype(BF16)
    hext[BLOCK + TILE:] = _layer_norm(xn_ref[0], lng, lnb).astype(BF16)

    kv = _dot(hext[...], win_ref[:, K_OFF:K_OFF + 2 * KV_WIDTH])
    low = lax.broadcasted_iota(jnp.int32, (TILE + 2 * BLOCK, KV_WIDTH), 1) < HEAD_DIM
    for src, dst in ((kv[:, :KV_WIDTH], kvar), (kv[:, KV_WIDTH:], vvar)):
        rolled = pltpu.roll(src, HEAD_DIM, axis=1)
        dst[0] = jnp.where(low, src, 0.0).astype(BF16)
        dst[1] = jnp.where(low, 0.0, rolled).astype(BF16)
        dst[2] = jnp.where(low, rolled, 0.0).astype(BF16)
        dst[3] = jnp.where(low, 0.0, src).astype(BF16)

    hcb = hext[BLOCK:BLOCK + TILE]
    q_scr[...] = (_dot(hcb, win_ref[:, Q_OFF:Q_OFF + ATT_WIDTH]) * (HEAD_DIM ** -0.5)).astype(BF16)

    lane_low = lax.broadcasted_iota(jnp.int32, (BLOCK, 2 * HEAD_DIM), 1) < HEAD_DIM
    for qb in range(QB_PER_TILE):
        if qb == 0:
            edge = jnp.where(i == 0, 1, 0)
        elif qb == QB_PER_TILE - 1:
            edge = jnp.where(i == last_i, 2, 0)
        else:
            edge = 0
        rows = slice(qb * BLOCK, qb * BLOCK + 3 * BLOCK)
        for pair in range(N_Q_HEADS // 2):
            kvh = pair // (GROUP // 2)
            q2 = q_scr[qb * BLOCK:(qb + 1) * BLOCK, pair * 2 * HEAD_DIM:(pair + 1) * 2 * HEAD_DIM]
            probs, inv = [], []
            for sub in range(2):
                h = 2 * pair + sub
                logits = _dot_nt(q2, kvar[2 * kvh + sub, rows]) + bias_ref[edge, h]
                sink = sink_ref[h]
                m = jnp.maximum(jnp.max(logits, axis=-1, keepdims=True), sink)
                p = jnp.exp(logits - m)
                denom = jnp.sum(p, axis=-1, keepdims=True) + jnp.exp(sink - m)
                probs.append(p.astype(BF16))
                inv.append(1.0 / denom)
            o = _dot(probs[0], vvar[2 * kvh, rows]) + _dot(probs[1], vvar[2 * kvh + 1, rows])
            o = o * jnp.where(lane_low, inv[0], inv[1])
            att_scr[qb * BLOCK:(qb + 1) * BLOCK, pair * 2 * HEAD_DIM:(pair + 1) * 2 * HEAD_DIM] = o.astype(BF16)

    lo = BLOCK - CONV_HALO
    hi = BLOCK + TILE + CONV_HALO
    ccx = _dot(hext[lo:hi], win_ref[:, CC_OFF:CC_OFF + 2 * CONV_WIDTH])
    row = lax.broadcasted_iota(jnp.int32, (TILE + 2 * CONV_HALO, 1), 0)
    inside = ((row >= CONV_HALO) | (i > 0)) & ((row < CONV_HALO + TILE) | (i < last_i))
    u_scr[...] = jnp.where(inside, ccx[:, :CONV_WIDTH] * ccx[:, CONV_WIDTH:], 0.0)
    cw = cw_ref[...]
    dw = (u_scr[CONV_HALO - 1:CONV_HALO - 1 + TILE] * cw[0:1]
          + u_scr[CONV_HALO:CONV_HALO + TILE] * cw[1:2]
          + u_scr[CONV_HALO + 1:CONV_HALO + 1 + TILE] * cw[2:3]
          + cbias_ref[...])
    conv = (_dot(hcb, win_ref[:, CB_OFF:CB_OFF + CONV_WIDTH]) * dw).astype(BF16)

    bg = bg_ref[...]
    g_att = jax.nn.sigmoid(_dot(hcb, win_ref[:, GATE_OFF:GATE_OFF + D_MODEL]) + bg[:, :D_MODEL])
    merged = g_att * _dot(att_scr[...], wa_ref[...])
    g_conv = jax.nn.sigmoid(_dot(hcb, win_ref[:, GATE_OFF + D_MODEL:]) + bg[:, D_MODEL:])
    merged = merged + g_conv * _dot(conv, wc_ref[...])
    mix = _dot(merged.astype(BF16), wo_ref[...])
    out_ref[0] = _layer_norm(DEEPNORM_ALPHA * hres[...] + mix, g2_ref[...], b2_ref[...])


def _const_spec(shape):
    return pl.BlockSpec(shape, lambda b, i: (0,) * len(shape), pipeline_mode=pl.Buffered(1))


def _mixer(x, ln_g, ln_b, w_in, b_gates, sink, bias_tbl, conv_w, conv_b, w_a, w_c, w_o, g2, b2):
    bsz, seq, _ = x.shape
    n_tiles = seq // TILE
    blocks_per_tile = TILE // BLOCK
    n_blocks = seq // BLOCK
    in_specs = [
        pl.BlockSpec((1, BLOCK, D_MODEL), lambda b, i: (b, jnp.maximum(i * blocks_per_tile - 1, 0), 0)),
        pl.BlockSpec((1, TILE, D_MODEL), lambda b, i: (b, i, 0)),
        pl.BlockSpec((1, BLOCK, D_MODEL),
                     lambda b, i: (b, jnp.minimum((i + 1) * blocks_per_tile, n_blocks - 1), 0)),
        _const_spec((1, D_MODEL)), _const_spec((1, D_MODEL)),
        _const_spec((D_MODEL, IN_PROJ_WIDTH)),
        _const_spec((1, 2 * D_MODEL)),
        pl.BlockSpec(memory_space=pltpu.SMEM),
        _const_spec((3, N_Q_HEADS, BLOCK, 3 * BLOCK)),
        _const_spec((3, CONV_WIDTH)), _const_spec((1, CONV_WIDTH)),
        _const_spec((ATT_WIDTH, D_MODEL)), _const_spec((CONV_WIDTH, D_MODEL)),
        _const_spec((D_MODEL, D_MODEL)),
        _const_spec((1, D_MODEL)), _const_spec((1, D_MODEL)),
    ]
    ext = TILE + 2 * BLOCK
    return pl.pallas_call(
        _mixer_kernel,
        out_shape=jax.ShapeDtypeStruct((bsz, seq, D_MODEL), F32),
        grid=(bsz, n_tiles),
        in_specs=in_specs,
        out_specs=pl.BlockSpec((1, TILE, D_MODEL), lambda b, i: (b, i, 0)),
        scratch_shapes=[
            pltpu.VMEM((ext, D_MODEL), BF16),
            pltpu.VMEM((TILE, D_MODEL), F32),
            pltpu.VMEM((TILE, ATT_WIDTH), BF16),
            pltpu.VMEM((4, ext, KV_WIDTH), BF16),
            pltpu.VMEM((4, ext, KV_WIDTH), BF16),
            pltpu.VMEM((TILE, ATT_WIDTH), BF16),
            pltpu.VMEM((TILE + 2 * CONV_HALO, CONV_WIDTH), F32),
        ],
        compiler_params=pltpu.CompilerParams(
            dimension_semantics=("arbitrary", "arbitrary"), vmem_limit_bytes=VMEM_LIMIT_BYTES),
        name="mixer",
    )(x, x, x, ln_g, ln_b, w_in, b_gates, sink, bias_tbl, conv_w, conv_b, w_a, w_c, w_o, g2, b2)


def _ffn_kernel(hp_ref, hc_ref, hn_ref, wup_ref, cw_ref, cb_ref, wdn_ref, g_ref, b_ref, out_ref,
                hext, a_scr, u_scr, acc):
    i = pl.program_id(1)
    last_i = pl.num_programs(1) - 1
    hext[0:CONV_HALO] = jnp.where(i > 0, hp_ref[0], 0.0).astype(BF16)
    hext[CONV_HALO:CONV_HALO + TILE] = hc_ref[0].astype(BF16)
    hext[CONV_HALO + TILE:] = jnp.where(i < last_i, hn_ref[0], 0.0).astype(BF16)

    def conv3(scr, col):
        w = cw_ref[:, col:col + FF_CHUNK]
        return (scr[CONV_HALO - 1:CONV_HALO - 1 + TILE] * w[0:1]
                + scr[CONV_HALO:CONV_HALO + TILE] * w[1:2]
                + scr[CONV_HALO + 1:CONV_HALO + 1 + TILE] * w[2:3]
                + cb_ref[:, col:col + FF_CHUNK])

    for c in range(N_FF_CHUNKS):
        col = c * FF_CHUNK
        a_scr[...] = _dot(hext[...], wup_ref[:, col:col + FF_CHUNK])
        u_scr[...] = _dot(hext[...], wup_ref[:, D_FF + col:D_FF + col + FF_CHUNK])
        a = conv3(a_scr, col)
        u = conv3(u_scr, D_FF + col)
        act = (a * jax.nn.sigmoid(a) * u).astype(BF16)
        part = _dot(act, wdn_ref[col:col + FF_CHUNK, :])
        if c == 0:
            acc[...] = part
        else:
            acc[...] += part
    out_ref[0] = _layer_norm(DEEPNORM_ALPHA * hc_ref[0] + acc[...], g_ref[...], b_ref[...])


def _ffn(h, w_up, conv_w, conv_b, w_down, g, b):
    bsz, seq, _ = h.shape
    n_tiles = seq // TILE
    halo_per_tile = TILE // CONV_HALO
    n_halo_blocks = seq // CONV_HALO
    in_specs = [
        pl.BlockSpec((1, CONV_HALO, D_MODEL), lambda b, i: (b, jnp.maximum(i * halo_per_tile - 1, 0), 0)),
        pl.BlockSpec((1, TILE, D_MODEL), lambda b, i: (b, i, 0)),
        pl.BlockSpec((1, CONV_HALO, D_MODEL),
                     lambda b, i: (b, jnp.minimum((i + 1) * halo_per_tile, n_halo_blocks - 1), 0)),
        _const_spec((D_MODEL, 2 * D_FF)),
        _const_spec((3, 2 * D_FF)), _const_spec((1, 2 * D_FF)),
        _const_spec((D_FF, D_MODEL)),
        _const_spec((1, D_MODEL)), _const_spec((1, D_MODEL)),
    ]
    ext = TILE + 2 * CONV_HALO
    return pl.pallas_call(
        _ffn_kernel,
        out_shape=jax.ShapeDtypeStruct((bsz, seq, D_MODEL), F32),
        grid=(bsz, n_tiles),
        in_specs=in_specs,
        out_specs=pl.BlockSpec((1, TILE, D_MODEL), lambda b, i: (b, i, 0)),
        scratch_shapes=[
            pltpu.VMEM((ext, D_MODEL), BF16),
            pltpu.VMEM((ext, FF_CHUNK), F32),
            pltpu.VMEM((ext, FF_CHUNK), F32),
            pltpu.VMEM((TILE, D_MODEL), F32),
        ],
        compiler_params=pltpu.CompilerParams(
            dimension_semantics=("arbitrary", "arbitrary"), vmem_limit_bytes=VMEM_LIMIT_BYTES),
        name="ffn",
    )(h, h, h, w_up, conv_w, conv_b, w_down, g, b)


def kernel(x, ln_in_g, ln_in_b, w_in, b_gates, attn_sink, rel_bias, conv_w, conv_b, w_att_branch,
           w_conv_branch, w_o, ln_mix_g, ln_mix_b, w_ffn_up, ffn_conv_w, ffn_conv_b, w_ffn_down,
           ln_ffn_g, ln_ffn_b):
    assert w_in.shape[0] == DEPTH == 1
    row = lambda v: v.reshape(1, -1)
    bias_tbl = _bias_table(rel_bias)
    h = _mixer(x, row(ln_in_g), row(ln_in_b), w_in[0].astype(BF16), row(b_gates[0]), attn_sink[0],
               bias_tbl, conv_w[0], row(conv_b[0]), w_att_branch[0].astype(BF16),
               w_conv_branch[0].astype(BF16), w_o[0].astype(BF16), row(ln_mix_g[0]), row(ln_mix_b[0]))
    return _ffn(h, w_ffn_up[0].astype(BF16), ffn_conv_w[0], row(ffn_conv_b[0]),
                w_ffn_down[0].astype(BF16), row(ln_ffn_g[0]), row(ln_ffn_b[0]))
```

```python
import functools
import math

import jax
import jax.numpy as jnp
import numpy as np
from jax import lax
from jax.experimental import pallas as pl
from jax.experimental.pallas import tpu as pltpu

D_MODEL = 1024
HEAD_DIM = 64
N_Q_HEADS = 8
N_KV_HEADS = 2
GROUP = N_Q_HEADS // N_KV_HEADS
ATT_WIDTH = N_Q_HEADS * HEAD_DIM
KV_WIDTH = N_KV_HEADS * HEAD_DIM
WINDOW = 128
BLOCK = 128
CONV_WIDTH = D_MODEL // 2
D_FF = 2816
N_BUCKETS = 32
MAX_DISTANCE = 128
LN_EPS = 1e-5
DEPTH = 1
DEEPNORM_ALPHA = (2 * DEPTH) ** 0.25
MASK_VALUE = -1e30

Q_OFF = 0
K_OFF = ATT_WIDTH
V_OFF = K_OFF + KV_WIDTH
CB_OFF = V_OFF + KV_WIDTH
CC_OFF = CB_OFF + CONV_WIDTH
CX_OFF = CC_OFF + CONV_WIDTH
GATE_OFF = CX_OFF + CONV_WIDTH
IN_PROJ_WIDTH = GATE_OFF + 2 * D_MODEL

TILE = 512
QB_PER_TILE = TILE // BLOCK
CONV_HALO = 16
FF_CHUNK = 256
N_FF_CHUNKS = D_FF // FF_CHUNK
FF_ROWS = 32
LANES = 128
ROW_PITCH = 2
VMEM_LIMIT_BYTES = 60 * 1024 * 1024

F32 = jnp.float32
BF16 = jnp.bfloat16


def _layer_norm(x, g, b):
    mu = jnp.mean(x, axis=-1, keepdims=True)
    xc = x - mu
    var = jnp.mean(xc * xc, axis=-1, keepdims=True)
    return xc * lax.rsqrt(var + LN_EPS) * g + b


def _dot(a, b):
    return jnp.dot(a, b, preferred_element_type=F32)


def _dot_nt(a, b):
    return lax.dot_general(a, b, (((1,), (1,)), ((), ())), preferred_element_type=F32)


def _bucket_ids():
    qi = jnp.arange(BLOCK)[:, None]
    kc = jnp.arange(3 * BLOCK)[None, :]
    rel = kc - BLOCK - qi
    half = N_BUCKETS // 2
    max_exact = half // 2
    offset = jnp.where(rel > 0, half, 0)
    n = jnp.abs(rel)
    nf = jnp.maximum(n, 1).astype(jnp.float32)
    large = max_exact + (jnp.log(nf / max_exact) / math.log(MAX_DISTANCE / max_exact)
                         * (half - max_exact)).astype(jnp.int32)
    large = jnp.minimum(large, half - 1)
    bucket = offset + jnp.where(n < max_exact, n, large)
    return jnp.where(n <= WINDOW, bucket, -1).astype(jnp.int32)


def _bias_table_kernel(ids_ref, rb_ref, out_ref):
    ids = ids_ref[...]
    col = lax.broadcasted_iota(jnp.int32, ids.shape, 1)
    for h in range(N_Q_HEADS):
        t = jnp.full(ids.shape, MASK_VALUE, F32)
        for bkt in range(N_BUCKETS):
            t = jnp.where(ids == bkt, rb_ref[bkt, h], t)
        out_ref[0, h] = t
        out_ref[1, h] = jnp.where(col < BLOCK, MASK_VALUE, t)
        out_ref[2, h] = jnp.where(col >= 2 * BLOCK, MASK_VALUE, t)


def _bias_table(rel_bias):
    ids = _bucket_ids()
    return pl.pallas_call(
        _bias_table_kernel,
        out_shape=jax.ShapeDtypeStruct((3, N_Q_HEADS, BLOCK, 3 * BLOCK), F32),
        in_specs=[pl.BlockSpec(memory_space=pltpu.VMEM), pl.BlockSpec(memory_space=pltpu.SMEM)],
        out_specs=pl.BlockSpec(memory_space=pltpu.VMEM),
        name="bias_table",
    )(ids, rel_bias)


def _mixer_kernel(xp_ref, xc_ref, xn_ref, lng_ref, lnb_ref, win_ref, bg_ref, sink_ref, bias_ref,
                  cw_ref, cbias_ref, wa_ref, wc_ref, wo_ref, g2_ref, b2_ref, out_ref,
                  hext, hres, q_scr, kvar, vvar, att_scr, u_scr):
    i = pl.program_id(1)
    last_i = pl.num_programs(1) - 1
    lng = lng_ref[...]
    lnb = lnb_ref[...]

    hc = _layer_norm(xc_ref[0], lng, lnb)
    hres[...] = hc
    hext[0:BLOCK] = _layer_norm(xp_ref[0], lng, lnb).astype(BF16)
    hext[BLOCK:BLOCK + TILE] = hc.astype(BF16)
    hext[BLOCK + TILE:] = _layer_norm(xn_ref[0], lng, lnb).astype(BF16)

    kv = _dot(hext[...], win_ref[:, K_OFF:K_OFF + 2 * KV_WIDTH])
    low = lax.broadcasted_iota(jnp.int32, (TILE + 2 * BLOCK, KV_WIDTH), 1) < HEAD_DIM
    for src, dst in ((kv[:, :KV_WIDTH], kvar), (kv[:, KV_WIDTH:], vvar)):
        rolled = pltpu.roll(src, HEAD_DIM, axis=1)
        dst[0] = jnp.where(low, src, 0.0).astype(BF16)
        dst[1] = jnp.where(low, 0.0, rolled).astype(BF16)
        dst[2] = jnp.where(low, rolled, 0.0).astype(BF16)
        dst[3] = jnp.where(low, 0.0, src).astype(BF16)

    hcb = hext[BLOCK:BLOCK + TILE]
    q_scr[...] = (_dot(hcb, win_ref[:, Q_OFF:Q_OFF + ATT_WIDTH]) * (HEAD_DIM ** -0.5)).astype(BF16)

    lane_low = lax.broadcasted_iota(jnp.int32, (BLOCK, 2 * HEAD_DIM), 1) < HEAD_DIM
    for qb in range(QB_PER_TILE):
        if qb == 0:
            edge = jnp.where(i == 0, 1, 0)
        elif qb == QB_PER_TILE - 1:
            edge = jnp.where(i == last_i, 2, 0)
        else:
            edge = 0
        rows = slice(qb * BLOCK, qb * BLOCK + 3 * BLOCK)
        for pair in range(N_Q_HEADS // 2):
            kvh = pair // (GROUP // 2)
            q2 = q_scr[qb * BLOCK:(qb + 1) * BLOCK, pair * 2 * HEAD_DIM:(pair + 1) * 2 * HEAD_DIM]
            probs, inv = [], []
            for sub in range(2):
                h = 2 * pair + sub
                logits = _dot_nt(q2, kvar[2 * kvh + sub, rows]) + bias_ref[edge, h]
                sink = sink_ref[h]
                m = jnp.maximum(jnp.max(logits, axis=-1, keepdims=True), sink)
                p = jnp.exp(logits - m)
                denom = jnp.sum(p, axis=-1, keepdims=True) + jnp.exp(sink - m)
                probs.append(p.astype(BF16))
                inv.append(1.0 / denom)
            o = _dot(probs[0], vvar[2 * kvh, rows]) + _dot(probs[1], vvar[2 * kvh + 1, rows])
            o = o * jnp.where(lane_low, inv[0], inv[1])
            att_scr[qb * BLOCK:(qb + 1) * BLOCK, pair * 2 * HEAD_DIM:(pair + 1) * 2 * HEAD_DIM] = o.astype(BF16)

    lo = BLOCK - CONV_HALO
    hi = BLOCK + TILE + CONV_HALO
    ccx = _dot(hext[lo:hi], win_ref[:, CC_OFF:CC_OFF + 2 * CONV_WIDTH])
    row = lax.broadcasted_iota(jnp.int32, (TILE + 2 * CONV_HALO, 1), 0)
    inside = ((row >= CONV_HALO) | (i > 0)) & ((row < CONV_HALO + TILE) | (i < last_i))
    u_scr[...] = jnp.where(inside, ccx[:, :CONV_WIDTH] * ccx[:, CONV_WIDTH:], 0.0)
    cw = cw_ref[...]
    dw = (u_scr[CONV_HALO - 1:CONV_HALO - 1 + TILE] * cw[0:1]
          + u_scr[CONV_HALO:CONV_HALO + TILE] * cw[1:2]
          + u_scr[CONV_HALO + 1:CONV_HALO + 1 + TILE] * cw[2:3]
          + cbias_ref[...])
    conv = (_dot(hcb, win_ref[:, CB_OFF:CB_OFF + CONV_WIDTH]) * dw).astype(BF16)

    bg = bg_ref[...]
    g_att = jax.nn.sigmoid(_dot(hcb, win_ref[:, GATE_OFF:GATE_OFF + D_MODEL]) + bg[:, :D_MODEL])
    merged = g_att * _dot(att_scr[...], wa_ref[...])
    g_conv = jax.nn.sigmoid(_dot(hcb, win_ref[:, GATE_OFF + D_MODEL:]) + bg[:, D_MODEL:])
    merged = merged + g_conv * _dot(conv, wc_ref[...])
    mix = _dot(merged.astype(BF16), wo_ref[...])
    out_ref[0] = _layer_norm(DEEPNORM_ALPHA * hres[...] + mix, g2_ref[...], b2_ref[...])


def _const_spec(shape):
    return pl.BlockSpec(shape, lambda b, i: (0,) * len(shape), pipeline_mode=pl.Buffered(1))


def _mixer(x, ln_g, ln_b, w_in, b_gates, sink, bias_tbl, conv_w, conv_b, w_a, w_c, w_o, g2, b2):
    bsz, seq, _ = x.shape
    n_tiles = seq // TILE
    blocks_per_tile = TILE // BLOCK
    n_blocks = seq // BLOCK
    in_specs = [
        pl.BlockSpec((1, BLOCK, D_MODEL), lambda b, i: (b, jnp.maximum(i * blocks_per_tile - 1, 0), 0)),
        pl.BlockSpec((1, TILE, D_MODEL), lambda b, i: (b, i, 0)),
        pl.BlockSpec((1, BLOCK, D_MODEL),
                     lambda b, i: (b, jnp.minimum((i + 1) * blocks_per_tile, n_blocks - 1), 0)),
        _const_spec((1, D_MODEL)), _const_spec((1, D_MODEL)),
        _const_spec((D_MODEL, IN_PROJ_WIDTH)),
        _const_spec((1, 2 * D_MODEL)),
        pl.BlockSpec(memory_space=pltpu.SMEM),
        _const_spec((3, N_Q_HEADS, BLOCK, 3 * BLOCK)),
        _const_spec((3, CONV_WIDTH)), _const_spec((1, CONV_WIDTH)),
        _const_spec((ATT_WIDTH, D_MODEL)), _const_spec((CONV_WIDTH, D_MODEL)),
        _const_spec((D_MODEL, D_MODEL)),
        _const_spec((1, D_MODEL)), _const_spec((1, D_MODEL)),
    ]
    ext = TILE + 2 * BLOCK
    return pl.pallas_call(
        _mixer_kernel,
        out_shape=jax.ShapeDtypeStruct((bsz, seq, D_MODEL), F32),
        grid=(bsz, n_tiles),
        in_specs=in_specs,
        out_specs=pl.BlockSpec((1, TILE, D_MODEL), lambda b, i: (b, i, 0)),
        scratch_shapes=[
            pltpu.VMEM((ext, D_MODEL), BF16),
            pltpu.VMEM((TILE, D_MODEL), F32),
            pltpu.VMEM((TILE, ATT_WIDTH), BF16),
            pltpu.VMEM((4, ext, KV_WIDTH), BF16),
            pltpu.VMEM((4, ext, KV_WIDTH), BF16),
            pltpu.VMEM((TILE, ATT_WIDTH), BF16),
            pltpu.VMEM((TILE + 2 * CONV_HALO, CONV_WIDTH), F32),
        ],
        compiler_params=pltpu.CompilerParams(
            dimension_semantics=("arbitrary", "arbitrary"), vmem_limit_bytes=VMEM_LIMIT_BYTES),
        name="mixer",
    )(x, x, x, ln_g, ln_b, w_in, b_gates, sink, bias_tbl, conv_w, conv_b, w_a, w_c, w_o, g2, b2)


def _ffn_kernel(hp_ref, hc_ref, hn_ref, wup_ref, cw_ref, cb_ref, wdn_ref, g_ref, b_ref, out_ref,
                hext, a_scr, u_scr, act_scr, acc):
    i = pl.program_id(1)
    last_i = pl.num_programs(1) - 1
    hext[0:CONV_HALO] = jnp.where(i > 0, hp_ref[0], 0.0).astype(BF16)
    hext[CONV_HALO:CONV_HALO + TILE] = hc_ref[0].astype(BF16)
    hext[CONV_HALO + TILE:] = jnp.where(i < last_i, hn_ref[0], 0.0).astype(BF16)

    ext_half = (TILE + 2 * CONV_HALO) // 2
    tile_half = TILE // 2

    def up_proj(c, half):
        col = c * FF_CHUNK
        rows = slice(half * ext_half, (half + 1) * ext_half)
        dst = pl.ds(ROW_PITCH * half * ext_half, ext_half, stride=ROW_PITCH)
        for scr, off in ((a_scr, col), (u_scr, D_FF + col)):
            res = _dot(hext[rows], wup_ref[:, off:off + FF_CHUNK])
            for slab in range(FF_CHUNK // LANES):
                scr[c % 2, slab, dst, :] = res[:, slab * LANES:(slab + 1) * LANES]

    def down_proj(c, half):
        rows = slice(half * tile_half, (half + 1) * tile_half)
        part = _dot(act_scr[c % 2, rows], wdn_ref[c * FF_CHUNK:(c + 1) * FF_CHUNK, :])
        if c == 0:
            acc[rows] = part
        else:
            acc[rows] += part

    def conv_taps(col):
        out = []
        for slab in range(FF_CHUNK // LANES):
            lanes = slice(col + slab * LANES, col + (slab + 1) * LANES)
            w = cw_ref[:, lanes]
            out.append([jnp.broadcast_to(w[k:k + 1], (FF_ROWS, LANES)) for k in range(3)]
                       + [jnp.broadcast_to(cb_ref[:, lanes], (FF_ROWS, LANES))])
        return out

    def conv3(scr, slot, slab, r0, taps):
        rows = lambda r: scr[slot, slab, pl.ds(ROW_PITCH * r, FF_ROWS, stride=ROW_PITCH), :]
        return rows(r0 - 1) * taps[0] + rows(r0) * taps[1] + rows(r0 + 1) * taps[2] + taps[3]

    def gate_rows(c, rb, a_taps, u_taps):
        r0 = CONV_HALO + rb * FF_ROWS
        parts = []
        for slab in range(FF_CHUNK // LANES):
            a = conv3(a_scr, c % 2, slab, r0, a_taps[slab])
            u = conv3(u_scr, c % 2, slab, r0, u_taps[slab])
            parts.append(a * jax.nn.sigmoid(a) * u)
        act_scr[c % 2, rb * FF_ROWS:(rb + 1) * FF_ROWS] = jnp.concatenate(parts, axis=1).astype(BF16)

    up_proj(0, 0)
    up_proj(0, 1)
    n_row_blocks = TILE // FF_ROWS
    for c in range(N_FF_CHUNKS + 1):
        mxu_units = []
        if c + 1 < N_FF_CHUNKS:
            mxu_units += [functools.partial(up_proj, c + 1, half) for half in range(2)]
        if c >= 1:
            mxu_units += [functools.partial(down_proj, c - 1, half) for half in range(2)]
        if c == N_FF_CHUNKS:
            for unit in mxu_units:
                unit()
            break
        a_taps = conv_taps(c * FF_CHUNK)
        u_taps = conv_taps(D_FF + c * FF_CHUNK)
        per_unit = n_row_blocks // len(mxu_units)
        for k, unit in enumerate(mxu_units):
            unit()
            for rb in range(k * per_unit, (k + 1) * per_unit):
                gate_rows(c, rb, a_taps, u_taps)
    out_ref[0] = _layer_norm(DEEPNORM_ALPHA * hc_ref[0] + acc[...], g_ref[...], b_ref[...])


def _ffn(h, w_up, conv_w, conv_b, w_down, g, b):
    bsz, seq, _ = h.shape
    n_tiles = seq // TILE
    halo_per_tile = TILE // CONV_HALO
    n_halo_blocks = seq // CONV_HALO
    in_specs = [
        pl.BlockSpec((1, CONV_HALO, D_MODEL), lambda b, i: (b, jnp.maximum(i * halo_per_tile - 1, 0), 0)),
        pl.BlockSpec((1, TILE, D_MODEL), lambda b, i: (b, i, 0)),
        pl.BlockSpec((1, CONV_HALO, D_MODEL),
                     lambda b, i: (b, jnp.minimum((i + 1) * halo_per_tile, n_halo_blocks - 1), 0)),
        _const_spec((D_MODEL, 2 * D_FF)),
        _const_spec((3, 2 * D_FF)), _const_spec((1, 2 * D_FF)),
        _const_spec((D_FF, D_MODEL)),
        _const_spec((1, D_MODEL)), _const_spec((1, D_MODEL)),
    ]
    ext = TILE + 2 * CONV_HALO
    return pl.pallas_call(
        _ffn_kernel,
        out_shape=jax.ShapeDtypeStruct((bsz, seq, D_MODEL), F32),
        grid=(bsz, n_tiles),
        in_specs=in_specs,
        out_specs=pl.BlockSpec((1, TILE, D_MODEL), lambda b, i: (b, i, 0)),
        scratch_shapes=[
            pltpu.VMEM((ext, D_MODEL), BF16),
            pltpu.VMEM((2, FF_CHUNK // LANES, ROW_PITCH * ext, LANES), F32),
            pltpu.VMEM((2, FF_CHUNK // LANES, ROW_PITCH * ext, LANES), F32),
            pltpu.VMEM((2, TILE, FF_CHUNK), BF16),
            pltpu.VMEM((TILE, D_MODEL), F32),
        ],
        compiler_params=pltpu.CompilerParams(
            dimension_semantics=("arbitrary", "arbitrary"), vmem_limit_bytes=VMEM_LIMIT_BYTES),
        name="ffn",
    )(h, h, h, w_up, conv_w, conv_b, w_down, g, b)


def kernel(x, ln_in_g, ln_in_b, w_in, b_gates, attn_sink, rel_bias, conv_w, conv_b, w_att_branch,
           w_conv_branch, w_o, ln_mix_g, ln_mix_b, w_ffn_up, ffn_conv_w, ffn_conv_b, w_ffn_down,
           ln_ffn_g, ln_ffn_b):
    assert w_in.shape[0] == DEPTH == 1
    row = lambda v: v.reshape(1, -1)
    bias_tbl = _bias_table(rel_bias)
    h = _mixer(x, row(ln_in_g), row(ln_in_b), w_in[0].astype(BF16), row(b_gates[0]), attn_sink[0],
               bias_tbl, conv_w[0], row(conv_b[0]), w_att_branch[0].astype(BF16),
               w_conv_branch[0].astype(BF16), w_o[0].astype(BF16), row(ln_mix_g[0]), row(ln_mix_b[0]))
    return _ffn(h, w_ffn_up[0].astype(BF16), ffn_conv_w[0], row(ffn_conv_b[0]),
                w_ffn_down[0].astype(BF16), row(ln_ffn_g[0]), row(ln_ffn_b[0]))
```

```python
import functools
import math

import jax
import jax.numpy as jnp
import numpy as np
from jax import lax
from jax.experimental import pallas as pl
from jax.experimental.pallas import tpu as pltpu

D_MODEL = 1024
HEAD_DIM = 64
N_Q_HEADS = 8
N_KV_HEADS = 2
GROUP = N_Q_HEADS // N_KV_HEADS
ATT_WIDTH = N_Q_HEADS * HEAD_DIM
KV_WIDTH = N_KV_HEADS * HEAD_DIM
WINDOW = 128
BLOCK = 128
CONV_WIDTH = D_MODEL // 2
D_FF = 2816
N_BUCKETS = 32
MAX_DISTANCE = 128
LN_EPS = 1e-5
DEPTH = 1
DEEPNORM_ALPHA = (2 * DEPTH) ** 0.25
MASK_VALUE = -1e30

Q_OFF = 0
K_OFF = ATT_WIDTH
V_OFF = K_OFF + KV_WIDTH
CB_OFF = V_OFF + KV_WIDTH
CC_OFF = CB_OFF + CONV_WIDTH
CX_OFF = CC_OFF + CONV_WIDTH
GATE_OFF = CX_OFF + CONV_WIDTH
IN_PROJ_WIDTH = GATE_OFF + 2 * D_MODEL

TILE = 512
QB_PER_TILE = TILE // BLOCK
CONV_HALO = 16
FF_CHUNK = 256
N_FF_CHUNKS = D_FF // FF_CHUNK
FF_ROWS = 32
PROJ_COLS = 256
MIX_ROWS = 32
LANES = 128
ROW_PITCH = 2
VMEM_LIMIT_BYTES = 60 * 1024 * 1024

F32 = jnp.float32
BF16 = jnp.bfloat16


def _layer_norm(x, g, b):
    mu = jnp.mean(x, axis=-1, keepdims=True)
    xc = x - mu
    var = jnp.mean(xc * xc, axis=-1, keepdims=True)
    return xc * lax.rsqrt(var + LN_EPS) * g + b


def _sigmoid(x):
    return 0.5 * jnp.tanh(0.5 * x) + 0.5


def _dot(a, b):
    return jnp.dot(a, b, preferred_element_type=F32)


def _dot_nt(a, b):
    return lax.dot_general(a, b, (((1,), (1,)), ((), ())), preferred_element_type=F32)


def _band_bias(rel_bias):
    qi = jnp.arange(BLOCK)[:, None]
    kc = jnp.arange(3 * BLOCK)[None, :]
    rel = kc - BLOCK - qi
    half = N_BUCKETS // 2
    max_exact = half // 2
    offset = jnp.where(rel > 0, half, 0)
    n = jnp.abs(rel)
    nf = jnp.maximum(n, 1).astype(jnp.float32)
    large = max_exact + (jnp.log(nf / max_exact) / math.log(MAX_DISTANCE / max_exact)
                         * (half - max_exact)).astype(jnp.int32)
    large = jnp.minimum(large, half - 1)
    bucket = offset + jnp.where(n < max_exact, n, large)
    return jnp.transpose(rel_bias.astype(F32)[bucket], (2, 0, 1))


def _bias_table_kernel(bias_ref, out_ref):
    shape = (BLOCK, 3 * BLOCK)
    col = lax.broadcasted_iota(jnp.int32, shape, 1)
    rel = col - BLOCK - lax.broadcasted_iota(jnp.int32, shape, 0)
    in_window = jnp.abs(rel) <= WINDOW
    for h in range(N_Q_HEADS):
        t = jnp.where(in_window, bias_ref[h], MASK_VALUE)
        out_ref[0, h] = t
        out_ref[1, h] = jnp.where(col < BLOCK, MASK_VALUE, t)
        out_ref[2, h] = jnp.where(col >= 2 * BLOCK, MASK_VALUE, t)


def _bias_table(rel_bias):
    return pl.pallas_call(
        _bias_table_kernel,
        out_shape=jax.ShapeDtypeStruct((3, N_Q_HEADS, BLOCK, 3 * BLOCK), F32),
        in_specs=[pl.BlockSpec(memory_space=pltpu.VMEM)],
        out_specs=pl.BlockSpec(memory_space=pltpu.VMEM),
        name="bias_table",
    )(_band_bias(rel_bias))


def _mixer_kernel(xp_ref, xc_ref, xn_ref, lng_ref, lnb_ref, win_ref, bg_ref, sink_ref, bias_ref,
                  cw_ref, cbias_ref, wa_ref, wc_ref, wo_ref, g2_ref, b2_ref, out_ref,
                  hext, hres, q_scr, kvar, vvar, att_scr, u_scr, conv_scr, g_scr):
    i = pl.program_id(1)
    last_i = pl.num_programs(1) - 1
    lng = lng_ref[...]
    lnb = lnb_ref[...]

    hc = _layer_norm(xc_ref[0], lng, lnb)
    hres[...] = hc
    hext[0:BLOCK] = _layer_norm(xp_ref[0], lng, lnb).astype(BF16)
    hext[BLOCK:BLOCK + TILE] = hc.astype(BF16)
    hext[BLOCK + TILE:] = _layer_norm(xn_ref[0], lng, lnb).astype(BF16)

    kv = _dot(hext[...], win_ref[:, K_OFF:K_OFF + 2 * KV_WIDTH])
    low = lax.broadcasted_iota(jnp.int32, (TILE + 2 * BLOCK, KV_WIDTH), 1) < HEAD_DIM
    for src, dst in ((kv[:, :KV_WIDTH], kvar), (kv[:, KV_WIDTH:], vvar)):
        rolled = pltpu.roll(src, HEAD_DIM, axis=1)
        dst[0] = jnp.where(low, src, 0.0).astype(BF16)
        dst[1] = jnp.where(low, 0.0, rolled).astype(BF16)
        dst[2] = jnp.where(low, rolled, 0.0).astype(BF16)
        dst[3] = jnp.where(low, 0.0, src).astype(BF16)

    hcb = hext[BLOCK:BLOCK + TILE]
    q_scr[...] = (_dot(hcb, win_ref[:, Q_OFF:Q_OFF + ATT_WIDTH]) * (HEAD_DIM ** -0.5)).astype(BF16)

    lane_low = lax.broadcasted_iota(jnp.int32, (BLOCK, 2 * HEAD_DIM), 1) < HEAD_DIM

    def attend(qb, pair):
        if qb == 0:
            edge = jnp.where(i == 0, 1, 0)
        elif qb == QB_PER_TILE - 1:
            edge = jnp.where(i == last_i, 2, 0)
        else:
            edge = 0
        rows = slice(qb * BLOCK, qb * BLOCK + 3 * BLOCK)
        kvh = pair // (GROUP // 2)
        q2 = q_scr[qb * BLOCK:(qb + 1) * BLOCK, pair * 2 * HEAD_DIM:(pair + 1) * 2 * HEAD_DIM]
        probs, inv = [], []
        for sub in range(2):
            h = 2 * pair + sub
            logits = _dot_nt(q2, kvar[2 * kvh + sub, rows]) + bias_ref[edge, h]
            sink = sink_ref[h]
            m = jnp.maximum(jnp.max(logits, axis=-1, keepdims=True), sink)
            p = jnp.exp(logits - m)
            denom = jnp.sum(p, axis=-1, keepdims=True) + jnp.exp(sink - m)
            probs.append(p.astype(BF16))
            inv.append(1.0 / denom)
        o = _dot(probs[0], vvar[2 * kvh, rows]) + _dot(probs[1], vvar[2 * kvh + 1, rows])
        o = o * jnp.where(lane_low, inv[0], inv[1])
        att_scr[qb * BLOCK:(qb + 1) * BLOCK, pair * 2 * HEAD_DIM:(pair + 1) * 2 * HEAD_DIM] = o.astype(BF16)

    lo = BLOCK - CONV_HALO
    hi = BLOCK + TILE + CONV_HALO
    slabs = PROJ_COLS // LANES

    def conv_input(j):
        cc = _dot(hext[lo:hi], win_ref[:, CC_OFF + j * PROJ_COLS:CC_OFF + (j + 1) * PROJ_COLS])
        cx = _dot(hext[lo:hi], win_ref[:, CX_OFF + j * PROJ_COLS:CX_OFF + (j + 1) * PROJ_COLS])
        row = lax.broadcasted_iota(jnp.int32, (TILE + 2 * CONV_HALO, 1), 0)
        inside = ((row >= CONV_HALO) | (i > 0)) & ((row < CONV_HALO + TILE) | (i < last_i))
        u = jnp.where(inside, cc * cx, 0.0)
        for s in range(slabs):
            u_scr[j * slabs + s, pl.ds(0, TILE + 2 * CONV_HALO, stride=ROW_PITCH), :] = (
                u[:, s * LANES:(s + 1) * LANES])

    def conv_branch(j):
        cb = _dot(hcb, win_ref[:, CB_OFF + j * PROJ_COLS:CB_OFF + (j + 1) * PROJ_COLS])
        taps = []
        for s in range(slabs):
            lanes = slice(j * PROJ_COLS + s * LANES, j * PROJ_COLS + (s + 1) * LANES)
            w = cw_ref[:, lanes]
            taps.append([jnp.broadcast_to(w[k:k + 1], (MIX_ROWS, LANES)) for k in range(3)]
                        + [jnp.broadcast_to(cbias_ref[:, lanes], (MIX_ROWS, LANES))])
        for rb in range(TILE // MIX_ROWS):
            r0 = CONV_HALO + rb * MIX_ROWS
            parts = []
            for s in range(slabs):
                ld = lambda r: u_scr[j * slabs + s, pl.ds(ROW_PITCH * r, MIX_ROWS, stride=ROW_PITCH), :]
                t = taps[s]
                dw = ld(r0 - 1) * t[0] + ld(r0) * t[1] + ld(r0 + 1) * t[2] + t[3]
                parts.append(cb[rb * MIX_ROWS:(rb + 1) * MIX_ROWS, s * LANES:(s + 1) * LANES] * dw)
            conv_scr[j, rb * MIX_ROWS:(rb + 1) * MIX_ROWS] = jnp.concatenate(parts, axis=1).astype(BF16)

    def gate(n):
        cols = slice(n * PROJ_COLS, (n + 1) * PROJ_COLS)
        pre = _dot(hcb, win_ref[:, GATE_OFF + n * PROJ_COLS:GATE_OFF + (n + 1) * PROJ_COLS])
        bias = bg_ref[:, cols]
        for rb in range(TILE // BLOCK):
            rows = slice(rb * BLOCK, (rb + 1) * BLOCK)
            g_scr[n, rows] = _sigmoid(pre[rows] + bias)

    n_conv = CONV_WIDTH // PROJ_COLS
    n_gate = 2 * D_MODEL // PROJ_COLS
    proj_units = ([functools.partial(conv_input, j) for j in range(n_conv)]
                  + [functools.partial(conv_branch, j) for j in range(n_conv)]
                  + [functools.partial(gate, n) for n in range(n_gate)])
    att_units = [(qb, pair) for qb in range(QB_PER_TILE) for pair in range(N_Q_HEADS // 2)]
    for k, (qb, pair) in enumerate(att_units):
        attend(qb, pair)
        if k < len(proj_units):
            proj_units[k]()
    for unit in proj_units[len(att_units):]:
        unit()

    n_out = D_MODEL // PROJ_COLS
    for half in range(2):
        rows = slice(half * (TILE // 2), (half + 1) * (TILE // 2))
        conv = jnp.concatenate([conv_scr[j, rows] for j in range(n_conv)], axis=1)
        merged = []
        for n in range(n_out):
            cols = slice(n * PROJ_COLS, (n + 1) * PROJ_COLS)
            ya = _dot(att_scr[rows], wa_ref[:, cols])
            yc = _dot(conv, wc_ref[:, cols])
            merged.append((g_scr[n, rows] * ya + g_scr[n_out + n, rows] * yc).astype(BF16))
        mix = _dot(jnp.concatenate(merged, axis=1), wo_ref[...])
        out_ref[0, rows] = _layer_norm(DEEPNORM_ALPHA * hres[rows] + mix, g2_ref[...], b2_ref[...])


def _const_spec(shape):
    return pl.BlockSpec(shape, lambda b, i: (0,) * len(shape), pipeline_mode=pl.Buffered(1))


def _mixer(x, ln_g, ln_b, w_in, b_gates, sink, bias_tbl, conv_w, conv_b, w_a, w_c, w_o, g2, b2):
    bsz, seq, _ = x.shape
    n_tiles = seq // TILE
    blocks_per_tile = TILE // BLOCK
    n_blocks = seq // BLOCK
    in_specs = [
        pl.BlockSpec((1, BLOCK, D_MODEL), lambda b, i: (b, jnp.maximum(i * blocks_per_tile - 1, 0), 0)),
        pl.BlockSpec((1, TILE, D_MODEL), lambda b, i: (b, i, 0)),
        pl.BlockSpec((1, BLOCK, D_MODEL),
                     lambda b, i: (b, jnp.minimum((i + 1) * blocks_per_tile, n_blocks - 1), 0)),
        _const_spec((1, D_MODEL)), _const_spec((1, D_MODEL)),
        _const_spec((D_MODEL, IN_PROJ_WIDTH)),
        _const_spec((1, 2 * D_MODEL)),
        pl.BlockSpec(memory_space=pltpu.SMEM),
        _const_spec((3, N_Q_HEADS, BLOCK, 3 * BLOCK)),
        _const_spec((3, CONV_WIDTH)), _const_spec((1, CONV_WIDTH)),
        _const_spec((ATT_WIDTH, D_MODEL)), _const_spec((CONV_WIDTH, D_MODEL)),
        _const_spec((D_MODEL, D_MODEL)),
        _const_spec((1, D_MODEL)), _const_spec((1, D_MODEL)),
    ]
    ext = TILE + 2 * BLOCK
    return pl.pallas_call(
        _mixer_kernel,
        out_shape=jax.ShapeDtypeStruct((bsz, seq, D_MODEL), F32),
        grid=(bsz, n_tiles),
        in_specs=in_specs,
        out_specs=pl.BlockSpec((1, TILE, D_MODEL), lambda b, i: (b, i, 0)),
        scratch_shapes=[
            pltpu.VMEM((ext, D_MODEL), BF16),
            pltpu.VMEM((TILE, D_MODEL), F32),
            pltpu.VMEM((TILE, ATT_WIDTH), BF16),
            pltpu.VMEM((4, ext, KV_WIDTH), BF16),
            pltpu.VMEM((4, ext, KV_WIDTH), BF16),
            pltpu.VMEM((TILE, ATT_WIDTH), BF16),
            pltpu.VMEM((CONV_WIDTH // LANES, ROW_PITCH * (TILE + 2 * CONV_HALO), LANES), F32),
            pltpu.VMEM((CONV_WIDTH // PROJ_COLS, TILE, PROJ_COLS), BF16),
            pltpu.VMEM((2 * D_MODEL // PROJ_COLS, TILE, PROJ_COLS), F32),
        ],
        compiler_params=pltpu.CompilerParams(
            dimension_semantics=("arbitrary", "arbitrary"), vmem_limit_bytes=VMEM_LIMIT_BYTES),
        name="mixer",
    )(x, x, x, ln_g, ln_b, w_in, b_gates, sink, bias_tbl, conv_w, conv_b, w_a, w_c, w_o, g2, b2)


def _ffn_kernel(hp_ref, hc_ref, hn_ref, wup_ref, cw_ref, cb_ref, wdn_ref, g_ref, b_ref, out_ref,
                hext, a_scr, u_scr, act_scr, acc):
    i = pl.program_id(1)
    last_i = pl.num_programs(1) - 1
    hext[0:CONV_HALO] = jnp.where(i > 0, hp_ref[0], 0.0).astype(BF16)
    hext[CONV_HALO:CONV_HALO + TILE] = hc_ref[0].astype(BF16)
    hext[CONV_HALO + TILE:] = jnp.where(i < last_i, hn_ref[0], 0.0).astype(BF16)

    ext_half = (TILE + 2 * CONV_HALO) // 2
    tile_half = TILE // 2

    def up_proj(c, half):
        col = c * FF_CHUNK
        rows = slice(half * ext_half, (half + 1) * ext_half)
        dst = pl.ds(ROW_PITCH * half * ext_half, ext_half, stride=ROW_PITCH)
        for scr, off in ((a_scr, col), (u_scr, D_FF + col)):
            res = _dot(hext[rows], wup_ref[:, off:off + FF_CHUNK])
            for slab in range(FF_CHUNK // LANES):
                scr[c % 2, slab, dst, :] = res[:, slab * LANES:(slab + 1) * LANES]

    def down_proj(c, half):
        rows = slice(half * tile_half, (half + 1) * tile_half)
        part = _dot(act_scr[c % 2, rows], wdn_ref[c * FF_CHUNK:(c + 1) * FF_CHUNK, :])
        if c == 0:
            acc[rows] = part
        else:
            acc[rows] += part

    def conv_taps(col):
        out = []
        for slab in range(FF_CHUNK // LANES):
            lanes = slice(col + slab * LANES, col + (slab + 1) * LANES)
            w = cw_ref[:, lanes]
            out.append([jnp.broadcast_to(w[k:k + 1], (FF_ROWS, LANES)) for k in range(3)]
                       + [jnp.broadcast_to(cb_ref[:, lanes], (FF_ROWS, LANES))])
        return out

    def conv3(scr, slot, slab, r0, taps):
        rows = lambda r: scr[slot, slab, pl.ds(ROW_PITCH * r, FF_ROWS, stride=ROW_PITCH), :]
        return rows(r0 - 1) * taps[0] + rows(r0) * taps[1] + rows(r0 + 1) * taps[2] + taps[3]

    def gate_rows(c, rb, a_taps, u_taps):
        r0 = CONV_HALO + rb * FF_ROWS
        parts = []
        for slab in range(FF_CHUNK // LANES):
            a = conv3(a_scr, c % 2, slab, r0, a_taps[slab])
            u = conv3(u_scr, c % 2, slab, r0, u_taps[slab])
            parts.append(a * _sigmoid(a) * u)
        act_scr[c % 2, rb * FF_ROWS:(rb + 1) * FF_ROWS] = jnp.concatenate(parts, axis=1).astype(BF16)

    up_proj(0, 0)
    up_proj(0, 1)
    n_row_blocks = TILE // FF_ROWS
    for c in range(N_FF_CHUNKS + 1):
        mxu_units = []
        if c + 1 < N_FF_CHUNKS:
            mxu_units += [functools.partial(up_proj, c + 1, half) for half in range(2)]
        if c >= 1:
            mxu_units += [functools.partial(down_proj, c - 1, half) for half in range(2)]
        if c == N_FF_CHUNKS:
            for unit in mxu_units:
                unit()
            break
        a_taps = conv_taps(c * FF_CHUNK)
        u_taps = conv_taps(D_FF + c * FF_CHUNK)
        per_unit = n_row_blocks // len(mxu_units)
        for k, unit in enumerate(mxu_units):
            unit()
            for rb in range(k * per_unit, (k + 1) * per_unit):
                gate_rows(c, rb, a_taps, u_taps)
    out_ref[0] = _layer_norm(DEEPNORM_ALPHA * hc_ref[0] + acc[...], g_ref[...], b_ref[...])


def _ffn(h, w_up, conv_w, conv_b, w_down, g, b):
    bsz, seq, _ = h.shape
    n_tiles = seq // TILE
    halo_per_tile = TILE // CONV_HALO
    n_halo_blocks = seq // CONV_HALO
    in_specs = [
        pl.BlockSpec((1, CONV_HALO, D_MODEL), lambda b, i: (b, jnp.maximum(i * halo_per_tile - 1, 0), 0)),
        pl.BlockSpec((1, TILE, D_MODEL), lambda b, i: (b, i, 0)),
        pl.BlockSpec((1, CONV_HALO, D_MODEL),
                     lambda b, i: (b, jnp.minimum((i + 1) * halo_per_tile, n_halo_blocks - 1), 0)),
        _const_spec((D_MODEL, 2 * D_FF)),
        _const_spec((3, 2 * D_FF)), _const_spec((1, 2 * D_FF)),
        _const_spec((D_FF, D_MODEL)),
        _const_spec((1, D_MODEL)), _const_spec((1, D_MODEL)),
    ]
    ext = TILE + 2 * CONV_HALO
    return pl.pallas_call(
        _ffn_kernel,
        out_shape=jax.ShapeDtypeStruct((bsz, seq, D_MODEL), F32),
        grid=(bsz, n_tiles),
        in_specs=in_specs,
        out_specs=pl.BlockSpec((1, TILE, D_MODEL), lambda b, i: (b, i, 0)),
        scratch_shapes=[
            pltpu.VMEM((ext, D_MODEL), BF16),
            pltpu.VMEM((2, FF_CHUNK // LANES, ROW_PITCH * ext, LANES), F32),
            pltpu.VMEM((2, FF_CHUNK // LANES, ROW_PITCH * ext, LANES), F32),
            pltpu.VMEM((2, TILE, FF_CHUNK), BF16),
            pltpu.VMEM((TILE, D_MODEL), F32),
        ],
        compiler_params=pltpu.CompilerParams(
            dimension_semantics=("arbitrary", "arbitrary"), vmem_limit_bytes=VMEM_LIMIT_BYTES),
        name="ffn",
    )(h, h, h, w_up, conv_w, conv_b, w_down, g, b)


def kernel(x, ln_in_g, ln_in_b, w_in, b_gates, attn_sink, rel_bias, conv_w, conv_b, w_att_branch,
           w_conv_branch, w_o, ln_mix_g, ln_mix_b, w_ffn_up, ffn_conv_w, ffn_conv_b, w_ffn_down,
           ln_ffn_g, ln_ffn_b):
    assert w_in.shape[0] == DEPTH == 1
    row = lambda v: v.reshape(1, -1)
    bias_tbl = _bias_table(rel_bias)
    h = _mixer(x, row(ln_in_g), row(ln_in_b), w_in[0].astype(BF16), row(b_gates[0]), attn_sink[0],
               bias_tbl, conv_w[0], row(conv_b[0]), w_att_branch[0].astype(BF16),
               w_conv_branch[0].astype(BF16), w_o[0].astype(BF16), row(ln_mix_g[0]), row(ln_mix_b[0]))
    return _ffn(h, w_ffn_up[0].astype(BF16), ffn_conv_w[0], row(ffn_conv_b[0]),
                w_ffn_down[0].astype(BF16), row(ln_ffn_g[0]), row(ln_ffn_b[0]))
```

```python
import functools
import math

import jax
import jax.numpy as jnp
import numpy as np
from jax import lax
from jax.experimental import pallas as pl
from jax.experimental.pallas import tpu as pltpu

D_MODEL = 1024
HEAD_DIM = 64
N_Q_HEADS = 8
N_KV_HEADS = 2
GROUP = N_Q_HEADS // N_KV_HEADS
ATT_WIDTH = N_Q_HEADS * HEAD_DIM
KV_WIDTH = N_KV_HEADS * HEAD_DIM
WINDOW = 128
BLOCK = 128
CONV_WIDTH = D_MODEL // 2
D_FF = 2816
N_BUCKETS = 32
MAX_DISTANCE = 128
LN_EPS = 1e-5
DEPTH = 1
DEEPNORM_ALPHA = (2 * DEPTH) ** 0.25
MASK_VALUE = -1e30

Q_OFF = 0
K_OFF = ATT_WIDTH
V_OFF = K_OFF + KV_WIDTH
CB_OFF = V_OFF + KV_WIDTH
CC_OFF = CB_OFF + CONV_WIDTH
CX_OFF = CC_OFF + CONV_WIDTH
GATE_OFF = CX_OFF + CONV_WIDTH
IN_PROJ_WIDTH = GATE_OFF + 2 * D_MODEL

TILE = 512
QB_PER_TILE = TILE // BLOCK
CONV_HALO = 16
FF_CHUNK = 256
N_FF_CHUNKS = D_FF // FF_CHUNK
FF_ROWS = 32
PROJ_COLS = 256
MIX_ROWS = 32
LANES = 128
ROW_PITCH = 2
VMEM_LIMIT_BYTES = 60 * 1024 * 1024

F32 = jnp.float32
BF16 = jnp.bfloat16


def _layer_norm(x, g, b):
    mu = jnp.mean(x, axis=-1, keepdims=True)
    xc = x - mu
    var = jnp.mean(xc * xc, axis=-1, keepdims=True)
    return xc * lax.rsqrt(var + LN_EPS) * g + b


def _sigmoid(x):
    return 0.5 * jnp.tanh(0.5 * x) + 0.5


def _dot(a, b):
    return jnp.dot(a, b, preferred_element_type=F32)


def _dot_nt(a, b):
    return lax.dot_general(a, b, (((1,), (1,)), ((), ())), preferred_element_type=F32)


def _bias_by_rel(rel_bias):
    rel = jnp.arange(4 * BLOCK) - (2 * BLOCK - 1)
    half = N_BUCKETS // 2
    max_exact = half // 2
    offset = jnp.where(rel > 0, half, 0)
    n = jnp.abs(rel)
    nf = jnp.maximum(n, 1).astype(jnp.float32)
    large = max_exact + (jnp.log(nf / max_exact) / math.log(MAX_DISTANCE / max_exact)
                         * (half - max_exact)).astype(jnp.int32)
    large = jnp.minimum(large, half - 1)
    bucket = offset + jnp.where(n < max_exact, n, large)
    return jnp.transpose(rel_bias.astype(F32)[bucket], (1, 0))


def _bias_table_kernel(bias_ref, out_ref):
    shape = (BLOCK, 3 * BLOCK)
    col = lax.broadcasted_iota(jnp.int32, shape, 1)
    rel = col - BLOCK - lax.broadcasted_iota(jnp.int32, shape, 0)
    in_window = jnp.abs(rel) <= WINDOW
    for h in range(N_Q_HEADS):
        by_rel = jnp.broadcast_to(bias_ref[h:h + 1, :], (BLOCK, 4 * BLOCK))
        band = pltpu.roll(by_rel, 3 * BLOCK + 1, axis=1, stride=1, stride_axis=0)[:, :3 * BLOCK]
        t = jnp.where(in_window, band, MASK_VALUE)
        out_ref[0, h] = t
        out_ref[1, h] = jnp.where(col < BLOCK, MASK_VALUE, t)
        out_ref[2, h] = jnp.where(col >= 2 * BLOCK, MASK_VALUE, t)


def _bias_table(rel_bias):
    return pl.pallas_call(
        _bias_table_kernel,
        out_shape=jax.ShapeDtypeStruct((3, N_Q_HEADS, BLOCK, 3 * BLOCK), F32),
        in_specs=[pl.BlockSpec(memory_space=pltpu.VMEM)],
        out_specs=pl.BlockSpec(memory_space=pltpu.VMEM),
        name="bias_table",
    )(_bias_by_rel(rel_bias))


def _mixer_kernel(xp_ref, xc_ref, xn_ref, lng_ref, lnb_ref, win_ref, bg_ref, sink_ref, bias_ref,
                  cw_ref, cbias_ref, wa_ref, wc_ref, wo_ref, g2_ref, b2_ref, out_ref,
                  hext, hres, q_scr, kvar, vvar, att_scr, u_scr, conv_scr, g_scr):
    i = pl.program_id(1)
    last_i = pl.num_programs(1) - 1
    lng = lng_ref[...]
    lnb = lnb_ref[...]

    hc = _layer_norm(xc_ref[0], lng, lnb)
    hres[...] = hc
    hext[0:BLOCK] = _layer_norm(xp_ref[0], lng, lnb).astype(BF16)
    hext[BLOCK:BLOCK + TILE] = hc.astype(BF16)
    hext[BLOCK + TILE:] = _layer_norm(xn_ref[0], lng, lnb).astype(BF16)

    kv = _dot(hext[...], win_ref[:, K_OFF:K_OFF + 2 * KV_WIDTH])
    low = lax.broadcasted_iota(jnp.int32, (TILE + 2 * BLOCK, KV_WIDTH), 1) < HEAD_DIM
    for src, dst in ((kv[:, :KV_WIDTH], kvar), (kv[:, KV_WIDTH:], vvar)):
        rolled = pltpu.roll(src, HEAD_DIM, axis=1)
        dst[0] = jnp.where(low, src, 0.0).astype(BF16)
        dst[1] = jnp.where(low, 0.0, rolled).astype(BF16)
        dst[2] = jnp.where(low, rolled, 0.0).astype(BF16)
        dst[3] = jnp.where(low, 0.0, src).astype(BF16)

    hcb = hext[BLOCK:BLOCK + TILE]
    q_scr[...] = (_dot(hcb, win_ref[:, Q_OFF:Q_OFF + ATT_WIDTH]) * (HEAD_DIM ** -0.5)).astype(BF16)

    lane_low = lax.broadcasted_iota(jnp.int32, (BLOCK, 2 * HEAD_DIM), 1) < HEAD_DIM

    def attend(qb, pair):
        if qb == 0:
            edge = jnp.where(i == 0, 1, 0)
        elif qb == QB_PER_TILE - 1:
            edge = jnp.where(i == last_i, 2, 0)
        else:
            edge = 0
        rows = slice(qb * BLOCK, qb * BLOCK + 3 * BLOCK)
        kvh = pair // (GROUP // 2)
        q2 = q_scr[qb * BLOCK:(qb + 1) * BLOCK, pair * 2 * HEAD_DIM:(pair + 1) * 2 * HEAD_DIM]
        probs, inv = [], []
        for sub in range(2):
            h = 2 * pair + sub
            logits = _dot_nt(q2, kvar[2 * kvh + sub, rows]) + bias_ref[edge, h]
            sink = sink_ref[h]
            m = jnp.maximum(jnp.max(logits, axis=-1, keepdims=True), sink)
            p = jnp.exp(logits - m)
            denom = jnp.sum(p, axis=-1, keepdims=True) + jnp.exp(sink - m)
            probs.append(p.astype(BF16))
            inv.append(1.0 / denom)
        o = _dot(probs[0], vvar[2 * kvh, rows]) + _dot(probs[1], vvar[2 * kvh + 1, rows])
        o = o * jnp.where(lane_low, inv[0], inv[1])
        att_scr[qb * BLOCK:(qb + 1) * BLOCK, pair * 2 * HEAD_DIM:(pair + 1) * 2 * HEAD_DIM] = o.astype(BF16)

    lo = BLOCK - CONV_HALO
    hi = BLOCK + TILE + CONV_HALO
    slabs = PROJ_COLS // LANES

    def conv_input(j):
        cc = _dot(hext[lo:hi], win_ref[:, CC_OFF + j * PROJ_COLS:CC_OFF + (j + 1) * PROJ_COLS])
        cx = _dot(hext[lo:hi], win_ref[:, CX_OFF + j * PROJ_COLS:CX_OFF + (j + 1) * PROJ_COLS])
        row = lax.broadcasted_iota(jnp.int32, (TILE + 2 * CONV_HALO, 1), 0)
        inside = ((row >= CONV_HALO) | (i > 0)) & ((row < CONV_HALO + TILE) | (i < last_i))
        u = jnp.where(inside, cc * cx, 0.0)
        for s in range(slabs):
            u_scr[j * slabs + s, pl.ds(0, TILE + 2 * CONV_HALO, stride=ROW_PITCH), :] = (
                u[:, s * LANES:(s + 1) * LANES])

    def conv_branch(j):
        cb = _dot(hcb, win_ref[:, CB_OFF + j * PROJ_COLS:CB_OFF + (j + 1) * PROJ_COLS])
        taps = []
        for s in range(slabs):
            lanes = slice(j * PROJ_COLS + s * LANES, j * PROJ_COLS + (s + 1) * LANES)
            w = cw_ref[:, lanes]
            taps.append([jnp.broadcast_to(w[k:k + 1], (MIX_ROWS, LANES)) for k in range(3)]
                        + [jnp.broadcast_to(cbias_ref[:, lanes], (MIX_ROWS, LANES))])
        for rb in range(TILE // MIX_ROWS):
            r0 = CONV_HALO + rb * MIX_ROWS
            parts = []
            for s in range(slabs):
                ld = lambda r: u_scr[j * slabs + s, pl.ds(ROW_PITCH * r, MIX_ROWS, stride=ROW_PITCH), :]
                t = taps[s]
                dw = ld(r0 - 1) * t[0] + ld(r0) * t[1] + ld(r0 + 1) * t[2] + t[3]
                parts.append(cb[rb * MIX_ROWS:(rb + 1) * MIX_ROWS, s * LANES:(s + 1) * LANES] * dw)
            conv_scr[j, rb * MIX_ROWS:(rb + 1) * MIX_ROWS] = jnp.concatenate(parts, axis=1).astype(BF16)

    def gate(n):
        cols = slice(n * PROJ_COLS, (n + 1) * PROJ_COLS)
        pre = _dot(hcb, win_ref[:, GATE_OFF + n * PROJ_COLS:GATE_OFF + (n + 1) * PROJ_COLS])
        bias = bg_ref[:, cols]
        for rb in range(TILE // BLOCK):
            rows = slice(rb * BLOCK, (rb + 1) * BLOCK)
            g_scr[n, rows] = _sigmoid(pre[rows] + bias)

    n_conv = CONV_WIDTH // PROJ_COLS
    n_gate = 2 * D_MODEL // PROJ_COLS
    proj_units = ([functools.partial(conv_input, j) for j in range(n_conv)]
                  + [functools.partial(conv_branch, j) for j in range(n_conv)]
                  + [functools.partial(gate, n) for n in range(n_gate)])
    att_units = [(qb, pair) for qb in range(QB_PER_TILE) for pair in range(N_Q_HEADS // 2)]
    for k, (qb, pair) in enumerate(att_units):
        attend(qb, pair)
        if k < len(proj_units):
            proj_units[k]()
    for unit in proj_units[len(att_units):]:
        unit()

    n_out = D_MODEL // PROJ_COLS
    for half in range(2):
        rows = slice(half * (TILE // 2), (half + 1) * (TILE // 2))
        conv = jnp.concatenate([conv_scr[j, rows] for j in range(n_conv)], axis=1)
        merged = []
        for n in range(n_out):
            cols = slice(n * PROJ_COLS, (n + 1) * PROJ_COLS)
            ya = _dot(att_scr[rows], wa_ref[:, cols])
            yc = _dot(conv, wc_ref[:, cols])
            merged.append((g_scr[n, rows] * ya + g_scr[n_out + n, rows] * yc).astype(BF16))
        mix = _dot(jnp.concatenate(merged, axis=1), wo_ref[...])
        out_ref[0, rows] = _layer_norm(DEEPNORM_ALPHA * hres[rows] + mix, g2_ref[...], b2_ref[...])


def _const_spec(shape):
    return pl.BlockSpec(shape, lambda b, i: (0,) * len(shape), pipeline_mode=pl.Buffered(1))


def _mixer(x, ln_g, ln_b, w_in, b_gates, sink, bias_tbl, conv_w, conv_b, w_a, w_c, w_o, g2, b2):
    bsz, seq, _ = x.shape
    n_tiles = seq // TILE
    blocks_per_tile = TILE // BLOCK
    n_blocks = seq // BLOCK
    in_specs = [
        pl.BlockSpec((1, BLOCK, D_MODEL), lambda b, i: (b, jnp.maximum(i * blocks_per_tile - 1, 0), 0)),
        pl.BlockSpec((1, TILE, D_MODEL), lambda b, i: (b, i, 0)),
        pl.BlockSpec((1, BLOCK, D_MODEL),
                     lambda b, i: (b, jnp.minimum((i + 1) * blocks_per_tile, n_blocks - 1), 0)),
        _const_spec((1, D_MODEL)), _const_spec((1, D_MODEL)),
        _const_spec((D_MODEL, IN_PROJ_WIDTH)),
        _const_spec((1, 2 * D_MODEL)),
        pl.BlockSpec(memory_space=pltpu.SMEM),
        _const_spec((3, N_Q_HEADS, BLOCK, 3 * BLOCK)),
        _const_spec((3, CONV_WIDTH)), _const_spec((1, CONV_WIDTH)),
        _const_spec((ATT_WIDTH, D_MODEL)), _const_spec((CONV_WIDTH, D_MODEL)),
        _const_spec((D_MODEL, D_MODEL)),
        _const_spec((1, D_MODEL)), _const_spec((1, D_MODEL)),
    ]
    ext = TILE + 2 * BLOCK
    return pl.pallas_call(
        _mixer_kernel,
        out_shape=jax.ShapeDtypeStruct((bsz, seq, D_MODEL), F32),
        grid=(bsz, n_tiles),
        in_specs=in_specs,
        out_specs=pl.BlockSpec((1, TILE, D_MODEL), lambda b, i: (b, i, 0)),
        scratch_shapes=[
            pltpu.VMEM((ext, D_MODEL), BF16),
            pltpu.VMEM((TILE, D_MODEL), F32),
            pltpu.VMEM((TILE, ATT_WIDTH), BF16),
            pltpu.VMEM((4, ext, KV_WIDTH), BF16),
            pltpu.VMEM((4, ext, KV_WIDTH), BF16),
            pltpu.VMEM((TILE, ATT_WIDTH), BF16),
            pltpu.VMEM((CONV_WIDTH // LANES, ROW_PITCH * (TILE + 2 * CONV_HALO), LANES), F32),
            pltpu.VMEM((CONV_WIDTH // PROJ_COLS, TILE, PROJ_COLS), BF16),
            pltpu.VMEM((2 * D_MODEL // PROJ_COLS, TILE, PROJ_COLS), F32),
        ],
        compiler_params=pltpu.CompilerParams(
            dimension_semantics=("arbitrary", "arbitrary"), vmem_limit_bytes=VMEM_LIMIT_BYTES),
        name="mixer",
    )(x, x, x, ln_g, ln_b, w_in, b_gates, sink, bias_tbl, conv_w, conv_b, w_a, w_c, w_o, g2, b2)


def _ffn_kernel(hp_ref, hc_ref, hn_ref, wup_ref, cw_ref, cb_ref, wdn_ref, g_ref, b_ref, out_ref,
                hext, a_scr, u_scr, act_scr, acc):
    i = pl.program_id(1)
    last_i = pl.num_programs(1) - 1
    hext[0:CONV_HALO] = jnp.where(i > 0, hp_ref[0], 0.0).astype(BF16)
    hext[CONV_HALO:CONV_HALO + TILE] = hc_ref[0].astype(BF16)
    hext[CONV_HALO + TILE:] = jnp.where(i < last_i, hn_ref[0], 0.0).astype(BF16)

    ext_half = (TILE + 2 * CONV_HALO) // 2
    tile_half = TILE // 2

    def up_proj(c, half):
        col = c * FF_CHUNK
        rows = slice(half * ext_half, (half + 1) * ext_half)
        dst = pl.ds(ROW_PITCH * half * ext_half, ext_half, stride=ROW_PITCH)
        for scr, off in ((a_scr, col), (u_scr, D_FF + col)):
            res = _dot(hext[rows], wup_ref[:, off:off + FF_CHUNK])
            for slab in range(FF_CHUNK // LANES):
                scr[c % 2, slab, dst, :] = res[:, slab * LANES:(slab + 1) * LANES]

    def down_proj(c, half):
        rows = slice(half * tile_half, (half + 1) * tile_half)
        part = _dot(act_scr[c % 2, rows], wdn_ref[c * FF_CHUNK:(c + 1) * FF_CHUNK, :])
        if c == 0:
            acc[rows] = part
        else:
            acc[rows] += part

    def conv_taps(col):
        out = []
        for slab in range(FF_CHUNK // LANES):
            lanes = slice(col + slab * LANES, col + (slab + 1) * LANES)
            w = cw_ref[:, lanes]
            out.append([jnp.broadcast_to(w[k:k + 1], (FF_ROWS, LANES)) for k in range(3)]
                       + [jnp.broadcast_to(cb_ref[:, lanes], (FF_ROWS, LANES))])
        return out

    def conv3(scr, slot, slab, r0, taps):
        rows = lambda r: scr[slot, slab, pl.ds(ROW_PITCH * r, FF_ROWS, stride=ROW_PITCH), :]
        return rows(r0 - 1) * taps[0] + rows(r0) * taps[1] + rows(r0 + 1) * taps[2] + taps[3]

    def gate_rows(c, rb, a_taps, u_taps):
        r0 = CONV_HALO + rb * FF_ROWS
        parts = []
        for slab in range(FF_CHUNK // LANES):
            a = conv3(a_scr, c % 2, slab, r0, a_taps[slab])
            u = conv3(u_scr, c % 2, slab, r0, u_taps[slab])
            parts.append(a * _sigmoid(a) * u)
        act_scr[c % 2, rb * FF_ROWS:(rb + 1) * FF_ROWS] = jnp.concatenate(parts, axis=1).astype(BF16)

    up_proj(0, 0)
    up_proj(0, 1)
    n_row_blocks = TILE // FF_ROWS
    for c in range(N_FF_CHUNKS + 1):
        mxu_units = []
        if c + 1 < N_FF_CHUNKS:
            mxu_units += [functools.partial(up_proj, c + 1, half) for half in range(2)]
        if c >= 1:
            mxu_units += [functools.partial(down_proj, c - 1, half) for half in range(2)]
        if c == N_FF_CHUNKS:
            for unit in mxu_units:
                unit()
            break
        a_taps = conv_taps(c * FF_CHUNK)
        u_taps = conv_taps(D_FF + c * FF_CHUNK)
        per_unit = n_row_blocks // len(mxu_units)
        for k, unit in enumerate(mxu_units):
            unit()
            for rb in range(k * per_unit, (k + 1) * per_unit):
                gate_rows(c, rb, a_taps, u_taps)
    out_ref[0] = _layer_norm(DEEPNORM_ALPHA * hc_ref[0] + acc[...], g_ref[...], b_ref[...])


def _ffn(h, w_up, conv_w, conv_b, w_down, g, b):
    bsz, seq, _ = h.shape
    n_tiles = seq // TILE
    halo_per_tile = TILE // CONV_HALO
    n_halo_blocks = seq // CONV_HALO
    in_specs = [
        pl.BlockSpec((1, CONV_HALO, D_MODEL), lambda b, i: (b, jnp.maximum(i * halo_per_tile - 1, 0), 0)),
        pl.BlockSpec((1, TILE, D_MODEL), lambda b, i: (b, i, 0)),
        pl.BlockSpec((1, CONV_HALO, D_MODEL),
                     lambda b, i: (b, jnp.minimum((i + 1) * halo_per_tile, n_halo_blocks - 1), 0)),
        _const_spec((D_MODEL, 2 * D_FF)),
        _const_spec((3, 2 * D_FF)), _const_spec((1, 2 * D_FF)),
        _const_spec((D_FF, D_MODEL)),
        _const_spec((1, D_MODEL)), _const_spec((1, D_MODEL)),
    ]
    ext = TILE + 2 * CONV_HALO
    return pl.pallas_call(
        _ffn_kernel,
        out_shape=jax.ShapeDtypeStruct((bsz, seq, D_MODEL), F32),
        grid=(bsz, n_tiles),
        in_specs=in_specs,
        out_specs=pl.BlockSpec((1, TILE, D_MODEL), lambda b, i: (b, i, 0)),
        scratch_shapes=[
            pltpu.VMEM((ext, D_MODEL), BF16),
            pltpu.VMEM((2, FF_CHUNK // LANES, ROW_PITCH * ext, LANES), F32),
            pltpu.VMEM((2, FF_CHUNK // LANES, ROW_PITCH * ext, LANES), F32),
            pltpu.VMEM((2, TILE, FF_CHUNK), BF16),
            pltpu.VMEM((TILE, D_MODEL), F32),
        ],
        compiler_params=pltpu.CompilerParams(
            dimension_semantics=("arbitrary", "arbitrary"), vmem_limit_bytes=VMEM_LIMIT_BYTES),
        name="ffn",
    )(h, h, h, w_up, conv_w, conv_b, w_down, g, b)


def kernel(x, ln_in_g, ln_in_b, w_in, b_gates, attn_sink, rel_bias, conv_w, conv_b, w_att_branch,
           w_conv_branch, w_o, ln_mix_g, ln_mix_b, w_ffn_up, ffn_conv_w, ffn_conv_b, w_ffn_down,
           ln_ffn_g, ln_ffn_b):
    assert w_in.shape[0] == DEPTH == 1
    row = lambda v: v.reshape(1, -1)
    bias_tbl = _bias_table(rel_bias)
    h = _mixer(x, row(ln_in_g), row(ln_in_b), w_in[0].astype(BF16), row(b_gates[0]), attn_sink[0],
               bias_tbl, conv_w[0], row(conv_b[0]), w_att_branch[0].astype(BF16),
               w_conv_branch[0].astype(BF16), w_o[0].astype(BF16), row(ln_mix_g[0]), row(ln_mix_b[0]))
    return _ffn(h, w_ffn_up[0].astype(BF16), ffn_conv_w[0], row(ffn_conv_b[0]),
                w_ffn_down[0].astype(BF16), row(ln_ffn_g[0]), row(ln_ffn_b[0]))
```

```python
import functools
import math

import jax
import jax.numpy as jnp
import numpy as np
from jax import lax
from jax.experimental import pallas as pl
from jax.experimental.pallas import tpu as pltpu

D_MODEL = 1024
HEAD_DIM = 64
N_Q_HEADS = 8
N_KV_HEADS = 2
GROUP = N_Q_HEADS // N_KV_HEADS
ATT_WIDTH = N_Q_HEADS * HEAD_DIM
KV_WIDTH = N_KV_HEADS * HEAD_DIM
WINDOW = 128
BLOCK = 128
CONV_WIDTH = D_MODEL // 2
D_FF = 2816
N_BUCKETS = 32
MAX_DISTANCE = 128
LN_EPS = 1e-5
DEPTH = 1
DEEPNORM_ALPHA = (2 * DEPTH) ** 0.25
MASK_VALUE = -1e30

Q_OFF = 0
K_OFF = ATT_WIDTH
V_OFF = K_OFF + KV_WIDTH
CB_OFF = V_OFF + KV_WIDTH
CC_OFF = CB_OFF + CONV_WIDTH
CX_OFF = CC_OFF + CONV_WIDTH
GATE_OFF = CX_OFF + CONV_WIDTH
IN_PROJ_WIDTH = GATE_OFF + 2 * D_MODEL

TILE = 512
QB_PER_TILE = TILE // BLOCK
CONV_HALO = 16
FF_CHUNK = 256
N_FF_CHUNKS = D_FF // FF_CHUNK
FF_ROWS = 32
PROJ_COLS = 256
MIX_ROWS = 32
LANES = 128
ROW_PITCH = 2
VMEM_LIMIT_BYTES = 60 * 1024 * 1024

F32 = jnp.float32
BF16 = jnp.bfloat16


def _layer_norm(x, g, b):
    mu = jnp.mean(x, axis=-1, keepdims=True)
    xc = x - mu
    var = jnp.mean(xc * xc, axis=-1, keepdims=True)
    return xc * lax.rsqrt(var + LN_EPS) * g + b


def _sigmoid(x):
    return 0.5 * jnp.tanh(0.5 * x) + 0.5


def _dot(a, b):
    return jnp.dot(a, b, preferred_element_type=F32)


def _dot_nt(a, b):
    return lax.dot_general(a, b, (((1,), (1,)), ((), ())), preferred_element_type=F32)


def _bias_by_rel(rel_bias):
    rel = jnp.arange(4 * BLOCK) - (2 * BLOCK - 1)
    half = N_BUCKETS // 2
    max_exact = half // 2
    offset = jnp.where(rel > 0, half, 0)
    n = jnp.abs(rel)
    nf = jnp.maximum(n, 1).astype(jnp.float32)
    large = max_exact + (jnp.log(nf / max_exact) / math.log(MAX_DISTANCE / max_exact)
                         * (half - max_exact)).astype(jnp.int32)
    large = jnp.minimum(large, half - 1)
    bucket = offset + jnp.where(n < max_exact, n, large)
    return jnp.transpose(rel_bias.astype(F32)[bucket], (1, 0))


def _bias_table_kernel(bias_ref, out_ref):
    shape = (BLOCK, 3 * BLOCK)
    col = lax.broadcasted_iota(jnp.int32, shape, 1)
    rel = col - BLOCK - lax.broadcasted_iota(jnp.int32, shape, 0)
    in_window = jnp.abs(rel) <= WINDOW
    for h in range(N_Q_HEADS):
        by_rel = jnp.broadcast_to(bias_ref[h:h + 1, :], (BLOCK, 4 * BLOCK))
        band = pltpu.roll(by_rel, 3 * BLOCK + 1, axis=1, stride=1, stride_axis=0)[:, :3 * BLOCK]
        t = jnp.where(in_window, band, MASK_VALUE)
        out_ref[0, h] = t
        out_ref[1, h] = jnp.where(col < BLOCK, MASK_VALUE, t)
        out_ref[2, h] = jnp.where(col >= 2 * BLOCK, MASK_VALUE, t)


def _bias_table(rel_bias):
    return pl.pallas_call(
        _bias_table_kernel,
        out_shape=jax.ShapeDtypeStruct((3, N_Q_HEADS, BLOCK, 3 * BLOCK), F32),
        in_specs=[pl.BlockSpec(memory_space=pltpu.VMEM)],
        out_specs=pl.BlockSpec(memory_space=pltpu.VMEM),
        name="bias_table",
    )(_bias_by_rel(rel_bias))


def _mixer_kernel(xp_ref, xc_ref, xn_ref, lng_ref, lnb_ref, win_ref, bg_ref, sink_ref, bias_ref,
                  cw_ref, cbias_ref, wa_ref, wc_ref, wo_ref, g2_ref, b2_ref, out_ref,
                  hext, hres, q_scr, kvar, vvar, att_scr, u_scr, conv_scr, g_scr):
    i = pl.program_id(1)
    last_i = pl.num_programs(1) - 1
    lng = lng_ref[...]
    lnb = lnb_ref[...]

    hc = _layer_norm(xc_ref[0], lng, lnb)
    hres[...] = hc
    hext[0:BLOCK] = _layer_norm(xp_ref[0], lng, lnb).astype(BF16)
    hext[BLOCK:BLOCK + TILE] = hc.astype(BF16)
    hext[BLOCK + TILE:] = _layer_norm(xn_ref[0], lng, lnb).astype(BF16)

    kv = _dot(hext[...], win_ref[:, K_OFF:K_OFF + 2 * KV_WIDTH])
    low = lax.broadcasted_iota(jnp.int32, (TILE + 2 * BLOCK, KV_WIDTH), 1) < HEAD_DIM
    for src, dst in ((kv[:, :KV_WIDTH], kvar), (kv[:, KV_WIDTH:], vvar)):
        rolled = pltpu.roll(src, HEAD_DIM, axis=1)
        dst[0] = jnp.where(low, src, 0.0).astype(BF16)
        dst[1] = jnp.where(low, 0.0, rolled).astype(BF16)
        dst[2] = jnp.where(low, rolled, 0.0).astype(BF16)
        dst[3] = jnp.where(low, 0.0, src).astype(BF16)

    hcb = hext[BLOCK:BLOCK + TILE]
    q_scr[...] = (_dot(hcb, win_ref[:, Q_OFF:Q_OFF + ATT_WIDTH]) * (HEAD_DIM ** -0.5)).astype(BF16)

    lane_low = lax.broadcasted_iota(jnp.int32, (BLOCK, 2 * HEAD_DIM), 1) < HEAD_DIM

    def attend(qb, pair):
        if qb == 0:
            edge = jnp.where(i == 0, 1, 0)
        elif qb == QB_PER_TILE - 1:
            edge = jnp.where(i == last_i, 2, 0)
        else:
            edge = 0
        rows = slice(qb * BLOCK, qb * BLOCK + 3 * BLOCK)
        kvh = pair // (GROUP // 2)
        q2 = q_scr[qb * BLOCK:(qb + 1) * BLOCK, pair * 2 * HEAD_DIM:(pair + 1) * 2 * HEAD_DIM]
        scores = _dot_nt(q2, jnp.concatenate([kvar[2 * kvh, rows], kvar[2 * kvh + 1, rows]], axis=0))
        probs, inv = [], []
        for sub in range(2):
            h = 2 * pair + sub
            logits = scores[:, sub * 3 * BLOCK:(sub + 1) * 3 * BLOCK] + bias_ref[edge, h]
            sink = sink_ref[h]
            m = jnp.maximum(jnp.max(logits, axis=-1, keepdims=True), sink)
            p = jnp.exp(logits - m)
            denom = jnp.sum(p, axis=-1, keepdims=True) + jnp.exp(sink - m)
            probs.append(p.astype(BF16))
            inv.append(1.0 / denom)
        o = _dot(jnp.concatenate(probs, axis=1),
                 jnp.concatenate([vvar[2 * kvh, rows], vvar[2 * kvh + 1, rows]], axis=0))
        o = o * jnp.where(lane_low, inv[0], inv[1])
        att_scr[qb * BLOCK:(qb + 1) * BLOCK, pair * 2 * HEAD_DIM:(pair + 1) * 2 * HEAD_DIM] = o.astype(BF16)

    lo = BLOCK - CONV_HALO
    hi = BLOCK + TILE + CONV_HALO
    slabs = PROJ_COLS // LANES

    def conv_input(j):
        cc = _dot(hext[lo:hi], win_ref[:, CC_OFF + j * PROJ_COLS:CC_OFF + (j + 1) * PROJ_COLS])
        cx = _dot(hext[lo:hi], win_ref[:, CX_OFF + j * PROJ_COLS:CX_OFF + (j + 1) * PROJ_COLS])
        row = lax.broadcasted_iota(jnp.int32, (TILE + 2 * CONV_HALO, 1), 0)
        inside = ((row >= CONV_HALO) | (i > 0)) & ((row < CONV_HALO + TILE) | (i < last_i))
        u = jnp.where(inside, cc * cx, 0.0)
        for s in range(slabs):
            u_scr[j * slabs + s, pl.ds(0, TILE + 2 * CONV_HALO, stride=ROW_PITCH), :] = (
                u[:, s * LANES:(s + 1) * LANES])

    def conv_branch(j):
        cb = _dot(hcb, win_ref[:, CB_OFF + j * PROJ_COLS:CB_OFF + (j + 1) * PROJ_COLS])
        taps = []
        for s in range(slabs):
            lanes = slice(j * PROJ_COLS + s * LANES, j * PROJ_COLS + (s + 1) * LANES)
            w = cw_ref[:, lanes]
            taps.append([jnp.broadcast_to(w[k:k + 1], (MIX_ROWS, LANES)) for k in range(3)]
                        + [jnp.broadcast_to(cbias_ref[:, lanes], (MIX_ROWS, LANES))])
        for rb in range(TILE // MIX_ROWS):
            r0 = CONV_HALO + rb * MIX_ROWS
            parts = []
            for s in range(slabs):
                ld = lambda r: u_scr[j * slabs + s, pl.ds(ROW_PITCH * r, MIX_ROWS, stride=ROW_PITCH), :]
                t = taps[s]
                dw = ld(r0 - 1) * t[0] + ld(r0) * t[1] + ld(r0 + 1) * t[2] + t[3]
                parts.append(cb[rb * MIX_ROWS:(rb + 1) * MIX_ROWS, s * LANES:(s + 1) * LANES] * dw)
            conv_scr[j, rb * MIX_ROWS:(rb + 1) * MIX_ROWS] = jnp.concatenate(parts, axis=1).astype(BF16)

    def gate(n):
        cols = slice(n * PROJ_COLS, (n + 1) * PROJ_COLS)
        pre = _dot(hcb, win_ref[:, GATE_OFF + n * PROJ_COLS:GATE_OFF + (n + 1) * PROJ_COLS])
        bias = bg_ref[:, cols]
        for rb in range(TILE // BLOCK):
            rows = slice(rb * BLOCK, (rb + 1) * BLOCK)
            g_scr[n, rows] = _sigmoid(pre[rows] + bias)

    n_conv = CONV_WIDTH // PROJ_COLS
    n_gate = 2 * D_MODEL // PROJ_COLS
    proj_units = ([functools.partial(conv_input, j) for j in range(n_conv)]
                  + [functools.partial(conv_branch, j) for j in range(n_conv)]
                  + [functools.partial(gate, n) for n in range(n_gate)])
    att_units = [(qb, pair) for qb in range(QB_PER_TILE) for pair in range(N_Q_HEADS // 2)]
    for k, (qb, pair) in enumerate(att_units):
        attend(qb, pair)
        if k < len(proj_units):
            proj_units[k]()
    for unit in proj_units[len(att_units):]:
        unit()

    n_out = D_MODEL // PROJ_COLS
    for half in range(2):
        rows = slice(half * (TILE // 2), (half + 1) * (TILE // 2))
        conv = jnp.concatenate([conv_scr[j, rows] for j in range(n_conv)], axis=1)
        merged = []
        for n in range(n_out):
            cols = slice(n * PROJ_COLS, (n + 1) * PROJ_COLS)
            ya = _dot(att_scr[rows], wa_ref[:, cols])
            yc = _dot(conv, wc_ref[:, cols])
            merged.append((g_scr[n, rows] * ya + g_scr[n_out + n, rows] * yc).astype(BF16))
        mix = _dot(jnp.concatenate(merged, axis=1), wo_ref[...])
        out_ref[0, rows] = _layer_norm(DEEPNORM_ALPHA * hres[rows] + mix, g2_ref[...], b2_ref[...])


def _const_spec(shape):
    return pl.BlockSpec(shape, lambda b, i: (0,) * len(shape), pipeline_mode=pl.Buffered(1))


def _mixer(x, ln_g, ln_b, w_in, b_gates, sink, bias_tbl, conv_w, conv_b, w_a, w_c, w_o, g2, b2):
    bsz, seq, _ = x.shape
    n_tiles = seq // TILE
    blocks_per_tile = TILE // BLOCK
    n_blocks = seq // BLOCK
    in_specs = [
        pl.BlockSpec((1, BLOCK, D_MODEL), lambda b, i: (b, jnp.maximum(i * blocks_per_tile - 1, 0), 0)),
        pl.BlockSpec((1, TILE, D_MODEL), lambda b, i: (b, i, 0)),
        pl.BlockSpec((1, BLOCK, D_MODEL),
                     lambda b, i: (b, jnp.minimum((i + 1) * blocks_per_tile, n_blocks - 1), 0)),
        _const_spec((1, D_MODEL)), _const_spec((1, D_MODEL)),
        _const_spec((D_MODEL, IN_PROJ_WIDTH)),
        _const_spec((1, 2 * D_MODEL)),
        pl.BlockSpec(memory_space=pltpu.SMEM),
        _const_spec((3, N_Q_HEADS, BLOCK, 3 * BLOCK)),
        _const_spec((3, CONV_WIDTH)), _const_spec((1, CONV_WIDTH)),
        _const_spec((ATT_WIDTH, D_MODEL)), _const_spec((CONV_WIDTH, D_MODEL)),
        _const_spec((D_MODEL, D_MODEL)),
        _const_spec((1, D_MODEL)), _const_spec((1, D_MODEL)),
    ]
    ext = TILE + 2 * BLOCK
    return pl.pallas_call(
        _mixer_kernel,
        out_shape=jax.ShapeDtypeStruct((bsz, seq, D_MODEL), F32),
        grid=(bsz, n_tiles),
        in_specs=in_specs,
        out_specs=pl.BlockSpec((1, TILE, D_MODEL), lambda b, i: (b, i, 0)),
        scratch_shapes=[
            pltpu.VMEM((ext, D_MODEL), BF16),
            pltpu.VMEM((TILE, D_MODEL), F32),
            pltpu.VMEM((TILE, ATT_WIDTH), BF16),
            pltpu.VMEM((4, ext, KV_WIDTH), BF16),
            pltpu.VMEM((4, ext, KV_WIDTH), BF16),
            pltpu.VMEM((TILE, ATT_WIDTH), BF16),
            pltpu.VMEM((CONV_WIDTH // LANES, ROW_PITCH * (TILE + 2 * CONV_HALO), LANES), F32),
            pltpu.VMEM((CONV_WIDTH // PROJ_COLS, TILE, PROJ_COLS), BF16),
            pltpu.VMEM((2 * D_MODEL // PROJ_COLS, TILE, PROJ_COLS), F32),
        ],
        compiler_params=pltpu.CompilerParams(
            dimension_semantics=("arbitrary", "arbitrary"), vmem_limit_bytes=VMEM_LIMIT_BYTES),
        name="mixer",
    )(x, x, x, ln_g, ln_b, w_in, b_gates, sink, bias_tbl, conv_w, conv_b, w_a, w_c, w_o, g2, b2)


def _ffn_kernel(hp_ref, hc_ref, hn_ref, wup_ref, cw_ref, cb_ref, wdn_ref, g_ref, b_ref, out_ref,
                hext, a_scr, u_scr, act_scr, acc):
    i = pl.program_id(1)
    last_i = pl.num_programs(1) - 1
    hext[0:CONV_HALO] = jnp.where(i > 0, hp_ref[0], 0.0).astype(BF16)
    hext[CONV_HALO:CONV_HALO + TILE] = hc_ref[0].astype(BF16)
    hext[CONV_HALO + TILE:] = jnp.where(i < last_i, hn_ref[0], 0.0).astype(BF16)

    tile_half = TILE // 2
    ext_split = (0, tile_half + 2 * CONV_HALO, TILE + 2 * CONV_HALO)

    def up_proj(c, half):
        col = c * FF_CHUNK
        start, stop = ext_split[half], ext_split[half + 1]
        dst = pl.ds(ROW_PITCH * start, stop - start, stride=ROW_PITCH)
        for scr, off in ((a_scr, col), (u_scr, D_FF + col)):
            res = _dot(hext[start:stop], wup_ref[:, off:off + FF_CHUNK])
            for slab in range(FF_CHUNK // LANES):
                scr[c % 2, slab, dst, :] = res[:, slab * LANES:(slab + 1) * LANES]

    def down_proj(c, half):
        rows = slice(half * tile_half, (half + 1) * tile_half)
        part = _dot(act_scr[c % 2, rows], wdn_ref[c * FF_CHUNK:(c + 1) * FF_CHUNK, :])
        if c == 0:
            acc[rows] = part
        else:
            acc[rows] += part

    def finish(half):
        rows = slice(half * tile_half, (half + 1) * tile_half)
        out_ref[0, rows] = _layer_norm(DEEPNORM_ALPHA * hc_ref[0, rows] + acc[rows], g_ref[...], b_ref[...])

    def conv_taps(col):
        out = []
        for slab in range(FF_CHUNK // LANES):
            lanes = slice(col + slab * LANES, col + (slab + 1) * LANES)
            w = cw_ref[:, lanes]
            out.append([jnp.broadcast_to(w[k:k + 1], (FF_ROWS, LANES)) for k in range(3)]
                       + [jnp.broadcast_to(cb_ref[:, lanes], (FF_ROWS, LANES))])
        return out

    def conv3(scr, slot, slab, r0, taps):
        rows = lambda r: scr[slot, slab, pl.ds(ROW_PITCH * r, FF_ROWS, stride=ROW_PITCH), :]
        return rows(r0 - 1) * taps[0] + rows(r0) * taps[1] + rows(r0 + 1) * taps[2] + taps[3]

    def gate_rows(c, rb, a_taps, u_taps):
        r0 = CONV_HALO + rb * FF_ROWS
        parts = []
        for slab in range(FF_CHUNK // LANES):
            a = conv3(a_scr, c % 2, slab, r0, a_taps[slab])
            u = conv3(u_scr, c % 2, slab, r0, u_taps[slab])
            parts.append(a * _sigmoid(a) * u)
        act_scr[c % 2, rb * FF_ROWS:(rb + 1) * FF_ROWS] = jnp.concatenate(parts, axis=1).astype(BF16)

    units = [(c, half) for c in range(N_FF_CHUNKS) for half in range(2)]
    blocks_per_half = tile_half // FF_ROWS
    up_proj(*units[0])
    taps = {}
    for k, (c, half) in enumerate(units):
        if half == 0:
            taps = {"a": conv_taps(c * FF_CHUNK), "u": conv_taps(D_FF + c * FF_CHUNK)}
        mxu_units = []
        if k + 1 < len(units):
            mxu_units.append(functools.partial(up_proj, *units[k + 1]))
        if k >= 1:
            mxu_units.append(functools.partial(down_proj, *units[k - 1]))
        per_unit = blocks_per_half // len(mxu_units)
        for m, unit in enumerate(mxu_units):
            unit()
            for rb in range(m * per_unit, (m + 1) * per_unit):
                gate_rows(c, half * blocks_per_half + rb, taps["a"], taps["u"])
    finish(0)
    down_proj(*units[-1])
    finish(1)


def _ffn(h, w_up, conv_w, conv_b, w_down, g, b):
    bsz, seq, _ = h.shape
    n_tiles = seq // TILE
    halo_per_tile = TILE // CONV_HALO
    n_halo_blocks = seq // CONV_HALO
    in_specs = [
        pl.BlockSpec((1, CONV_HALO, D_MODEL), lambda b, i: (b, jnp.maximum(i * halo_per_tile - 1, 0), 0)),
        pl.BlockSpec((1, TILE, D_MODEL), lambda b, i: (b, i, 0)),
        pl.BlockSpec((1, CONV_HALO, D_MODEL),
                     lambda b, i: (b, jnp.minimum((i + 1) * halo_per_tile, n_halo_blocks - 1), 0)),
        _const_spec((D_MODEL, 2 * D_FF)),
        _const_spec((3, 2 * D_FF)), _const_spec((1, 2 * D_FF)),
        _const_spec((D_FF, D_MODEL)),
        _const_spec((1, D_MODEL)), _const_spec((1, D_MODEL)),
    ]
    ext = TILE + 2 * CONV_HALO
    return pl.pallas_call(
        _ffn_kernel,
        out_shape=jax.ShapeDtypeStruct((bsz, seq, D_MODEL), F32),
        grid=(bsz, n_tiles),
        in_specs=in_specs,
        out_specs=pl.BlockSpec((1, TILE, D_MODEL), lambda b, i: (b, i, 0)),
        scratch_shapes=[
            pltpu.VMEM((ext, D_MODEL), BF16),
            pltpu.VMEM((2, FF_CHUNK // LANES, ROW_PITCH * ext, LANES), F32),
            pltpu.VMEM((2, FF_CHUNK // LANES, ROW_PITCH * ext, LANES), F32),
            pltpu.VMEM((2, TILE, FF_CHUNK), BF16),
            pltpu.VMEM((TILE, D_MODEL), F32),
        ],
        compiler_params=pltpu.CompilerParams(
            dimension_semantics=("arbitrary", "arbitrary"), vmem_limit_bytes=VMEM_LIMIT_BYTES),
        name="ffn",
    )(h, h, h, w_up, conv_w, conv_b, w_down, g, b)


def kernel(x, ln_in_g, ln_in_b, w_in, b_gates, attn_sink, rel_bias, conv_w, conv_b, w_att_branch,
           w_conv_branch, w_o, ln_mix_g, ln_mix_b, w_ffn_up, ffn_conv_w, ffn_conv_b, w_ffn_down,
           ln_ffn_g, ln_ffn_b):
    assert w_in.shape[0] == DEPTH == 1
    row = lambda v: v.reshape(1, -1)
    bias_tbl = _bias_table(rel_bias)
    h = _mixer(x, row(ln_in_g), row(ln_in_b), w_in[0].astype(BF16), row(b_gates[0]), attn_sink[0],
               bias_tbl, conv_w[0], row(conv_b[0]), w_att_branch[0].astype(BF16),
               w_conv_branch[0].astype(BF16), w_o[0].astype(BF16), row(ln_mix_g[0]), row(ln_mix_b[0]))
    return _ffn(h, w_ffn_up[0].astype(BF16), ffn_conv_w[0], row(ffn_conv_b[0]),
                w_ffn_down[0].astype(BF16), row(ln_ffn_g[0]), row(ln_ffn_b[0]))
```

```python
import functools
import math

import jax
import jax.numpy as jnp
import numpy as np
from jax import lax
from jax.experimental import pallas as pl
from jax.experimental.pallas import tpu as pltpu

D_MODEL = 1024
HEAD_DIM = 64
N_Q_HEADS = 8
N_KV_HEADS = 2
GROUP = N_Q_HEADS // N_KV_HEADS
ATT_WIDTH = N_Q_HEADS * HEAD_DIM
KV_WIDTH = N_KV_HEADS * HEAD_DIM
WINDOW = 128
BLOCK = 128
CONV_WIDTH = D_MODEL // 2
D_FF = 2816
N_BUCKETS = 32
MAX_DISTANCE = 128
LN_EPS = 1e-5
DEPTH = 1
DEEPNORM_ALPHA = (2 * DEPTH) ** 0.25
MASK_VALUE = -1e30

Q_OFF = 0
K_OFF = ATT_WIDTH
V_OFF = K_OFF + KV_WIDTH
CB_OFF = V_OFF + KV_WIDTH
CC_OFF = CB_OFF + CONV_WIDTH
CX_OFF = CC_OFF + CONV_WIDTH
GATE_OFF = CX_OFF + CONV_WIDTH
IN_PROJ_WIDTH = GATE_OFF + 2 * D_MODEL

TILE = 512
QB_PER_TILE = TILE // BLOCK
CONV_HALO = 16
FF_CHUNK = 256
N_FF_CHUNKS = D_FF // FF_CHUNK
FF_ROWS = 32
PROJ_COLS = 256
MIX_ROWS = 32
LN_ROWS = 32
SM_ROWS = 32
LANES = 128
ROW_PITCH = 2
VMEM_LIMIT_BYTES = 60 * 1024 * 1024

F32 = jnp.float32
BF16 = jnp.bfloat16


def _layer_norm(x, g, b):
    mu = jnp.mean(x, axis=-1, keepdims=True)
    xc = x - mu
    var = jnp.mean(xc * xc, axis=-1, keepdims=True)
    return xc * lax.rsqrt(var + LN_EPS) * g + b


def _sigmoid(x):
    return 0.5 * jnp.tanh(0.5 * x) + 0.5


def _dot(a, b):
    return jnp.dot(a, b, preferred_element_type=F32)


def _dot_nt(a, b):
    return lax.dot_general(a, b, (((1,), (1,)), ((), ())), preferred_element_type=F32)


def _bias_by_rel(rel_bias):
    rel = jnp.arange(4 * BLOCK) - (2 * BLOCK - 1)
    half = N_BUCKETS // 2
    max_exact = half // 2
    offset = jnp.where(rel > 0, half, 0)
    n = jnp.abs(rel)
    nf = jnp.maximum(n, 1).astype(jnp.float32)
    large = max_exact + (jnp.log(nf / max_exact) / math.log(MAX_DISTANCE / max_exact)
                         * (half - max_exact)).astype(jnp.int32)
    large = jnp.minimum(large, half - 1)
    bucket = offset + jnp.where(n < max_exact, n, large)
    return jnp.transpose(rel_bias.astype(F32)[bucket], (1, 0))


def _bias_table_kernel(bias_ref, out_ref):
    shape = (BLOCK, 3 * BLOCK)
    col = lax.broadcasted_iota(jnp.int32, shape, 1)
    rel = col - BLOCK - lax.broadcasted_iota(jnp.int32, shape, 0)
    in_window = jnp.abs(rel) <= WINDOW
    for h in range(N_Q_HEADS):
        by_rel = jnp.broadcast_to(bias_ref[h:h + 1, :], (BLOCK, 4 * BLOCK))
        band = pltpu.roll(by_rel, 3 * BLOCK + 1, axis=1, stride=1, stride_axis=0)[:, :3 * BLOCK]
        t = jnp.where(in_window, band, MASK_VALUE)
        out_ref[0, h] = t
        out_ref[1, h] = jnp.where(col < BLOCK, MASK_VALUE, t)
        out_ref[2, h] = jnp.where(col >= 2 * BLOCK, MASK_VALUE, t)


def _bias_table(rel_bias):
    return pl.pallas_call(
        _bias_table_kernel,
        out_shape=jax.ShapeDtypeStruct((3, N_Q_HEADS, BLOCK, 3 * BLOCK), F32),
        in_specs=[pl.BlockSpec(memory_space=pltpu.VMEM)],
        out_specs=pl.BlockSpec(memory_space=pltpu.VMEM),
        name="bias_table",
    )(_bias_by_rel(rel_bias))


def _mixer_kernel(xp_ref, xc_ref, xn_ref, lng_ref, lnb_ref, win_ref, bg_ref, sink_ref, bias_ref,
                  cw_ref, cbias_ref, wa_ref, wc_ref, wo_ref, g2_ref, b2_ref, out_ref,
                  hext, hres, q_scr, kvar, vvar, att_scr, u_scr, conv_scr, g_scr, merged_scr):
    i = pl.program_id(1)
    last_i = pl.num_programs(1) - 1
    lng = lng_ref[...]
    lnb = lnb_ref[...]
    ext = TILE + 2 * BLOCK

    def norm_rows(lo, hi):
        for r in range(lo, hi, LN_ROWS):
            if r < BLOCK:
                x = xp_ref[0, r:r + LN_ROWS]
            elif r < BLOCK + TILE:
                x = xc_ref[0, r - BLOCK:r - BLOCK + LN_ROWS]
            else:
                x = xn_ref[0, r - BLOCK - TILE:r - BLOCK - TILE + LN_ROWS]
            y = _layer_norm(x, lng, lnb)
            if BLOCK <= r < BLOCK + TILE:
                hres[r - BLOCK:r - BLOCK + LN_ROWS] = y
            hext[r:r + LN_ROWS] = y.astype(BF16)

    def project_qkv(lo, hi):
        rows = slice(lo, hi)
        kv = _dot(hext[rows], win_ref[:, K_OFF:K_OFF + 2 * KV_WIDTH])
        low = lax.broadcasted_iota(jnp.int32, (hi - lo, KV_WIDTH), 1) < HEAD_DIM
        for src, dst in ((kv[:, :KV_WIDTH], kvar), (kv[:, KV_WIDTH:], vvar)):
            rolled = pltpu.roll(src, HEAD_DIM, axis=1)
            dst[0, rows] = jnp.where(low, src, 0.0).astype(BF16)
            dst[1, rows] = jnp.where(low, 0.0, rolled).astype(BF16)
            dst[2, rows] = jnp.where(low, rolled, 0.0).astype(BF16)
            dst[3, rows] = jnp.where(low, 0.0, src).astype(BF16)
        q_lo, q_hi = max(lo, BLOCK), min(hi, BLOCK + TILE)
        q_scr[q_lo - BLOCK:q_hi - BLOCK] = (
            _dot(hext[q_lo:q_hi], win_ref[:, Q_OFF:Q_OFF + ATT_WIDTH]) * (HEAD_DIM ** -0.5)).astype(BF16)

    norm_rows(0, ext // 2)
    project_qkv(0, ext // 2)
    norm_rows(ext // 2, ext)
    project_qkv(ext // 2, ext)
    hcb = hext[BLOCK:BLOCK + TILE]

    lane_low = lax.broadcasted_iota(jnp.int32, (BLOCK, 2 * HEAD_DIM), 1) < HEAD_DIM

    def attend(qb, pair):
        if qb == 0:
            edge = jnp.where(i == 0, 1, 0)
        elif qb == QB_PER_TILE - 1:
            edge = jnp.where(i == last_i, 2, 0)
        else:
            edge = 0
        rows = slice(qb * BLOCK, qb * BLOCK + 3 * BLOCK)
        kvh = pair // (GROUP // 2)
        q2 = q_scr[qb * BLOCK:(qb + 1) * BLOCK, pair * 2 * HEAD_DIM:(pair + 1) * 2 * HEAD_DIM]
        scores = _dot_nt(q2, jnp.concatenate([kvar[2 * kvh, rows], kvar[2 * kvh + 1, rows]], axis=0))
        probs, inv = [], []
        for sub in range(2):
            h = 2 * pair + sub
            sink = sink_ref[h]
            p_blocks, inv_blocks = [], []
            for r in range(0, BLOCK, SM_ROWS):
                logits = (scores[r:r + SM_ROWS, sub * 3 * BLOCK:(sub + 1) * 3 * BLOCK]
                          + bias_ref[edge, h, r:r + SM_ROWS, :])
                m = jnp.maximum(jnp.max(logits, axis=-1, keepdims=True), sink)
                p = jnp.exp(logits - m)
                denom = jnp.sum(p, axis=-1, keepdims=True) + jnp.exp(sink - m)
                p_blocks.append(p.astype(BF16))
                inv_blocks.append(1.0 / denom)
            probs.append(jnp.concatenate(p_blocks, axis=0))
            inv.append(jnp.concatenate(inv_blocks, axis=0))
        o = _dot(jnp.concatenate(probs, axis=1),
                 jnp.concatenate([vvar[2 * kvh, rows], vvar[2 * kvh + 1, rows]], axis=0))
        o = o * jnp.where(lane_low, inv[0], inv[1])
        att_scr[qb * BLOCK:(qb + 1) * BLOCK, pair * 2 * HEAD_DIM:(pair + 1) * 2 * HEAD_DIM] = o.astype(BF16)

    lo = BLOCK - CONV_HALO
    hi = BLOCK + TILE + CONV_HALO
    slabs = PROJ_COLS // LANES

    def conv_input(j):
        cc = _dot(hext[lo:hi], win_ref[:, CC_OFF + j * PROJ_COLS:CC_OFF + (j + 1) * PROJ_COLS])
        cx = _dot(hext[lo:hi], win_ref[:, CX_OFF + j * PROJ_COLS:CX_OFF + (j + 1) * PROJ_COLS])
        row = lax.broadcasted_iota(jnp.int32, (TILE + 2 * CONV_HALO, 1), 0)
        inside = ((row >= CONV_HALO) | (i > 0)) & ((row < CONV_HALO + TILE) | (i < last_i))
        u = jnp.where(inside, cc * cx, 0.0)
        for s in range(slabs):
            u_scr[j * slabs + s, pl.ds(0, TILE + 2 * CONV_HALO, stride=ROW_PITCH), :] = (
                u[:, s * LANES:(s + 1) * LANES])

    def conv_branch(j):
        cb = _dot(hcb, win_ref[:, CB_OFF + j * PROJ_COLS:CB_OFF + (j + 1) * PROJ_COLS])
        taps = []
        for s in range(slabs):
            lanes = slice(j * PROJ_COLS + s * LANES, j * PROJ_COLS + (s + 1) * LANES)
            w = cw_ref[:, lanes]
            taps.append([jnp.broadcast_to(w[k:k + 1], (MIX_ROWS, LANES)) for k in range(3)]
                        + [jnp.broadcast_to(cbias_ref[:, lanes], (MIX_ROWS, LANES))])
        for rb in range(TILE // MIX_ROWS):
            r0 = CONV_HALO + rb * MIX_ROWS
            parts = []
            for s in range(slabs):
                ld = lambda r: u_scr[j * slabs + s, pl.ds(ROW_PITCH * r, MIX_ROWS, stride=ROW_PITCH), :]
                t = taps[s]
                dw = ld(r0 - 1) * t[0] + ld(r0) * t[1] + ld(r0 + 1) * t[2] + t[3]
                parts.append(cb[rb * MIX_ROWS:(rb + 1) * MIX_ROWS, s * LANES:(s + 1) * LANES] * dw)
            conv_scr[j, rb * MIX_ROWS:(rb + 1) * MIX_ROWS] = jnp.concatenate(parts, axis=1).astype(BF16)

    def gate(n):
        cols = slice(n * PROJ_COLS, (n + 1) * PROJ_COLS)
        pre = _dot(hcb, win_ref[:, GATE_OFF + n * PROJ_COLS:GATE_OFF + (n + 1) * PROJ_COLS])
        bias = bg_ref[:, cols]
        for rb in range(TILE // BLOCK):
            rows = slice(rb * BLOCK, (rb + 1) * BLOCK)
            g_scr[n, rows] = _sigmoid(pre[rows] + bias)

    n_conv = CONV_WIDTH // PROJ_COLS
    n_gate = 2 * D_MODEL // PROJ_COLS
    proj_units = ([functools.partial(conv_input, j) for j in range(n_conv)]
                  + [functools.partial(conv_branch, j) for j in range(n_conv)]
                  + [functools.partial(gate, n) for n in range(n_gate)])
    n_out = D_MODEL // PROJ_COLS
    half_rows = TILE // 2

    def merge(half, n):
        rows = slice(half * half_rows, (half + 1) * half_rows)
        cols = slice(n * PROJ_COLS, (n + 1) * PROJ_COLS)
        conv = jnp.concatenate([conv_scr[j, rows] for j in range(n_conv)], axis=1)
        ya = _dot(att_scr[rows], wa_ref[:, cols])
        yc = _dot(conv, wc_ref[:, cols])
        merged_scr[n, rows] = (g_scr[n, rows] * ya + g_scr[n_out + n, rows] * yc).astype(BF16)

    def out_proj(half):
        rows = slice(half * half_rows, (half + 1) * half_rows)
        return _dot(jnp.concatenate([merged_scr[n, rows] for n in range(n_out)], axis=1), wo_ref[...])

    def finish(half, mix, blocks):
        for blk in blocks:
            r = half * half_rows + blk * LN_ROWS
            out_ref[0, r:r + LN_ROWS] = _layer_norm(
                DEEPNORM_ALPHA * hres[r:r + LN_ROWS] + mix[blk * LN_ROWS:(blk + 1) * LN_ROWS],
                g2_ref[...], b2_ref[...])

    att_units = [(qb, pair) for qb in range(QB_PER_TILE) for pair in range(N_Q_HEADS // 2)]
    filler = proj_units + [functools.partial(merge, 0, n) for n in range(n_out)]
    assert len(filler) == len(att_units) and len(proj_units) >= len(att_units) // 2
    for (qb, pair), unit in zip(att_units, filler):
        attend(qb, pair)
        unit()
    ln_blocks = half_rows // LN_ROWS
    mix = out_proj(0)
    for n in range(n_out):
        finish(0, mix, range(n * ln_blocks // n_out, (n + 1) * ln_blocks // n_out))
        merge(1, n)
    finish(1, out_proj(1), range(ln_blocks))


def _const_spec(shape):
    return pl.BlockSpec(shape, lambda b, i: (0,) * len(shape), pipeline_mode=pl.Buffered(1))


def _mixer(x, ln_g, ln_b, w_in, b_gates, sink, bias_tbl, conv_w, conv_b, w_a, w_c, w_o, g2, b2):
    bsz, seq, _ = x.shape
    n_tiles = seq // TILE
    blocks_per_tile = TILE // BLOCK
    n_blocks = seq // BLOCK
    in_specs = [
        pl.BlockSpec((1, BLOCK, D_MODEL), lambda b, i: (b, jnp.maximum(i * blocks_per_tile - 1, 0), 0)),
        pl.BlockSpec((1, TILE, D_MODEL), lambda b, i: (b, i, 0)),
        pl.BlockSpec((1, BLOCK, D_MODEL),
                     lambda b, i: (b, jnp.minimum((i + 1) * blocks_per_tile, n_blocks - 1), 0)),
        _const_spec((1, D_MODEL)), _const_spec((1, D_MODEL)),
        _const_spec((D_MODEL, IN_PROJ_WIDTH)),
        _const_spec((1, 2 * D_MODEL)),
        pl.BlockSpec(memory_space=pltpu.SMEM),
        _const_spec((3, N_Q_HEADS, BLOCK, 3 * BLOCK)),
        _const_spec((3, CONV_WIDTH)), _const_spec((1, CONV_WIDTH)),
        _const_spec((ATT_WIDTH, D_MODEL)), _const_spec((CONV_WIDTH, D_MODEL)),
        _const_spec((D_MODEL, D_MODEL)),
        _const_spec((1, D_MODEL)), _const_spec((1, D_MODEL)),
    ]
    ext = TILE + 2 * BLOCK
    return pl.pallas_call(
        _mixer_kernel,
        out_shape=jax.ShapeDtypeStruct((bsz, seq, D_MODEL), F32),
        grid=(bsz, n_tiles),
        in_specs=in_specs,
        out_specs=pl.BlockSpec((1, TILE, D_MODEL), lambda b, i: (b, i, 0)),
        scratch_shapes=[
            pltpu.VMEM((ext, D_MODEL), BF16),
            pltpu.VMEM((TILE, D_MODEL), F32),
            pltpu.VMEM((TILE, ATT_WIDTH), BF16),
            pltpu.VMEM((4, ext, KV_WIDTH), BF16),
            pltpu.VMEM((4, ext, KV_WIDTH), BF16),
            pltpu.VMEM((TILE, ATT_WIDTH), BF16),
            pltpu.VMEM((CONV_WIDTH // LANES, ROW_PITCH * (TILE + 2 * CONV_HALO), LANES), F32),
            pltpu.VMEM((CONV_WIDTH // PROJ_COLS, TILE, PROJ_COLS), BF16),
            pltpu.VMEM((2 * D_MODEL // PROJ_COLS, TILE, PROJ_COLS), F32),
            pltpu.VMEM((D_MODEL // PROJ_COLS, TILE, PROJ_COLS), BF16),
        ],
        compiler_params=pltpu.CompilerParams(
            dimension_semantics=("arbitrary", "arbitrary"), vmem_limit_bytes=VMEM_LIMIT_BYTES),
        name="mixer",
    )(x, x, x, ln_g, ln_b, w_in, b_gates, sink, bias_tbl, conv_w, conv_b, w_a, w_c, w_o, g2, b2)


def _ffn_kernel(hp_ref, hc_ref, hn_ref, wup_ref, cw_ref, cb_ref, wdn_ref, g_ref, b_ref, out_ref,
                hext, a_scr, u_scr, act_scr, acc):
    i = pl.program_id(1)
    last_i = pl.num_programs(1) - 1
    hext[0:CONV_HALO] = jnp.where(i > 0, hp_ref[0], 0.0).astype(BF16)
    hext[CONV_HALO:CONV_HALO + TILE] = hc_ref[0].astype(BF16)
    hext[CONV_HALO + TILE:] = jnp.where(i < last_i, hn_ref[0], 0.0).astype(BF16)

    tile_half = TILE // 2
    ext_split = (0, tile_half + 2 * CONV_HALO, TILE + 2 * CONV_HALO)

    def up_proj(c, half):
        col = c * FF_CHUNK
        start, stop = ext_split[half], ext_split[half + 1]
        dst = pl.ds(ROW_PITCH * start, stop - start, stride=ROW_PITCH)
        for scr, off in ((a_scr, col), (u_scr, D_FF + col)):
            res = _dot(hext[start:stop], wup_ref[:, off:off + FF_CHUNK])
            for slab in range(FF_CHUNK // LANES):
                scr[c % 2, slab, dst, :] = res[:, slab * LANES:(slab + 1) * LANES]

    def down_proj(c, half):
        rows = slice(half * tile_half, (half + 1) * tile_half)
        part = _dot(act_scr[c % 2, rows], wdn_ref[c * FF_CHUNK:(c + 1) * FF_CHUNK, :])
        if c == 0:
            acc[rows] = part
        else:
            acc[rows] += part

    def finish(half):
        rows = slice(half * tile_half, (half + 1) * tile_half)
        out_ref[0, rows] = _layer_norm(DEEPNORM_ALPHA * hc_ref[0, rows] + acc[rows], g_ref[...], b_ref[...])

    def conv_taps(col):
        out = []
        for slab in range(FF_CHUNK // LANES):
            lanes = slice(col + slab * LANES, col + (slab + 1) * LANES)
            w = cw_ref[:, lanes]
            out.append([jnp.broadcast_to(w[k:k + 1], (FF_ROWS, LANES)) for k in range(3)]
                       + [jnp.broadcast_to(cb_ref[:, lanes], (FF_ROWS, LANES))])
        return out

    def conv3(scr, slot, slab, r0, taps):
        rows = lambda r: scr[slot, slab, pl.ds(ROW_PITCH * r, FF_ROWS, stride=ROW_PITCH), :]
        return rows(r0 - 1) * taps[0] + rows(r0) * taps[1] + rows(r0 + 1) * taps[2] + taps[3]

    def gate_rows(c, rb, a_taps, u_taps):
        r0 = CONV_HALO + rb * FF_ROWS
        parts = []
        for slab in range(FF_CHUNK // LANES):
            a = conv3(a_scr, c % 2, slab, r0, a_taps[slab])
            u = conv3(u_scr, c % 2, slab, r0, u_taps[slab])
            parts.append(a * _sigmoid(a) * u)
        act_scr[c % 2, rb * FF_ROWS:(rb + 1) * FF_ROWS] = jnp.concatenate(parts, axis=1).astype(BF16)

    units = [(c, half) for c in range(N_FF_CHUNKS) for half in range(2)]
    blocks_per_half = tile_half // FF_ROWS
    up_proj(*units[0])
    taps = {}
    for k, (c, half) in enumerate(units):
        if half == 0:
            taps = {"a": conv_taps(c * FF_CHUNK), "u": conv_taps(D_FF + c * FF_CHUNK)}
        mxu_units = []
        if k + 1 < len(units):
            mxu_units.append(functools.partial(up_proj, *units[k + 1]))
        if k >= 1:
            mxu_units.append(functools.partial(down_proj, *units[k - 1]))
        per_unit = blocks_per_half // len(mxu_units)
        for m, unit in enumerate(mxu_units):
            unit()
            for rb in range(m * per_unit, (m + 1) * per_unit):
                gate_rows(c, half * blocks_per_half + rb, taps["a"], taps["u"])
    finish(0)
    down_proj(*units[-1])
    finish(1)


def _ffn(h, w_up, conv_w, conv_b, w_down, g, b):
    bsz, seq, _ = h.shape
    n_tiles = seq // TILE
    halo_per_tile = TILE // CONV_HALO
    n_halo_blocks = seq // CONV_HALO
    in_specs = [
        pl.BlockSpec((1, CONV_HALO, D_MODEL), lambda b, i: (b, jnp.maximum(i * halo_per_tile - 1, 0), 0)),
        pl.BlockSpec((1, TILE, D_MODEL), lambda b, i: (b, i, 0)),
        pl.BlockSpec((1, CONV_HALO, D_MODEL),
                     lambda b, i: (b, jnp.minimum((i + 1) * halo_per_tile, n_halo_blocks - 1), 0)),
        _const_spec((D_MODEL, 2 * D_FF)),
        _const_spec((3, 2 * D_FF)), _const_spec((1, 2 * D_FF)),
        _const_spec((D_FF, D_MODEL)),
        _const_spec((1, D_MODEL)), _const_spec((1, D_MODEL)),
    ]
    ext = TILE + 2 * CONV_HALO
    return pl.pallas_call(
        _ffn_kernel,
        out_shape=jax.ShapeDtypeStruct((bsz, seq, D_MODEL), F32),
        grid=(bsz, n_tiles),
        in_specs=in_specs,
        out_specs=pl.BlockSpec((1, TILE, D_MODEL), lambda b, i: (b, i, 0)),
        scratch_shapes=[
            pltpu.VMEM((ext, D_MODEL), BF16),
            pltpu.VMEM((2, FF_CHUNK // LANES, ROW_PITCH * ext, LANES), F32),
            pltpu.VMEM((2, FF_CHUNK // LANES, ROW_PITCH * ext, LANES), F32),
            pltpu.VMEM((2, TILE, FF_CHUNK), BF16),
            pltpu.VMEM((TILE, D_MODEL), F32),
        ],
        compiler_params=pltpu.CompilerParams(
            dimension_semantics=("arbitrary", "arbitrary"), vmem_limit_bytes=VMEM_LIMIT_BYTES),
        name="ffn",
    )(h, h, h, w_up, conv_w, conv_b, w_down, g, b)


def kernel(x, ln_in_g, ln_in_b, w_in, b_gates, attn_sink, rel_bias, conv_w, conv_b, w_att_branch,
           w_conv_branch, w_o, ln_mix_g, ln_mix_b, w_ffn_up, ffn_conv_w, ffn_conv_b, w_ffn_down,
           ln_ffn_g, ln_ffn_b):
    assert w_in.shape[0] == DEPTH == 1
    row = lambda v: v.reshape(1, -1)
    bias_tbl = _bias_table(rel_bias)
    h = _mixer(x, row(ln_in_g), row(ln_in_b), w_in[0].astype(BF16), row(b_gates[0]), attn_sink[0],
               bias_tbl, conv_w[0], row(conv_b[0]), w_att_branch[0].astype(BF16),
               w_conv_branch[0].astype(BF16), w_o[0].astype(BF16), row(ln_mix_g[0]), row(ln_mix_b[0]))
    return _ffn(h, w_ffn_up[0].astype(BF16), ffn_conv_w[0], row(ffn_conv_b[0]),
                w_ffn_down[0].astype(BF16), row(ln_ffn_g[0]), row(ln_ffn_b[0]))
```

```python
import functools
import math

import jax
import jax.numpy as jnp
import numpy as np
from jax import lax
from jax.experimental import pallas as pl
from jax.experimental.pallas import tpu as pltpu

D_MODEL = 1024
HEAD_DIM = 64
N_Q_HEADS = 8
N_KV_HEADS = 2
GROUP = N_Q_HEADS // N_KV_HEADS
ATT_WIDTH = N_Q_HEADS * HEAD_DIM
KV_WIDTH = N_KV_HEADS * HEAD_DIM
WINDOW = 128
BLOCK = 128
CONV_WIDTH = D_MODEL // 2
D_FF = 2816
N_BUCKETS = 32
MAX_DISTANCE = 128
LN_EPS = 1e-5
DEPTH = 1
DEEPNORM_ALPHA = (2 * DEPTH) ** 0.25
MASK_VALUE = -1e30

Q_OFF = 0
K_OFF = ATT_WIDTH
V_OFF = K_OFF + KV_WIDTH
CB_OFF = V_OFF + KV_WIDTH
CC_OFF = CB_OFF + CONV_WIDTH
CX_OFF = CC_OFF + CONV_WIDTH
GATE_OFF = CX_OFF + CONV_WIDTH
IN_PROJ_WIDTH = GATE_OFF + 2 * D_MODEL

TILE = 512
QB_PER_TILE = TILE // BLOCK
CONV_HALO = 16
FF_CHUNK = 256
N_FF_CHUNKS = D_FF // FF_CHUNK
FF_ROWS = 32
PROJ_COLS = 256
MIX_ROWS = 32
STAGE_ROWS = 256
STAGE_ROWS_WIDE = 128
LN_ROWS = 32
SM_ROWS = 32
LANES = 128
ROW_PITCH = 2
VMEM_LIMIT_BYTES = 60 * 1024 * 1024

F32 = jnp.float32
BF16 = jnp.bfloat16


def _layer_norm(x, g, b):
    mu = jnp.mean(x, axis=-1, keepdims=True)
    xc = x - mu
    var = jnp.mean(xc * xc, axis=-1, keepdims=True)
    return xc * lax.rsqrt(var + LN_EPS) * g + b


def _sigmoid(x):
    return 0.5 * jnp.tanh(0.5 * x) + 0.5


def _dot(a, b):
    return jnp.dot(a, b, preferred_element_type=F32)


def _dot_nt(a, b):
    return lax.dot_general(a, b, (((1,), (1,)), ((), ())), preferred_element_type=F32)


def _bias_by_rel(rel_bias):
    rel = jnp.arange(4 * BLOCK) - (2 * BLOCK - 1)
    half = N_BUCKETS // 2
    max_exact = half // 2
    offset = jnp.where(rel > 0, half, 0)
    n = jnp.abs(rel)
    nf = jnp.maximum(n, 1).astype(jnp.float32)
    large = max_exact + (jnp.log(nf / max_exact) / math.log(MAX_DISTANCE / max_exact)
                         * (half - max_exact)).astype(jnp.int32)
    large = jnp.minimum(large, half - 1)
    bucket = offset + jnp.where(n < max_exact, n, large)
    return jnp.transpose(rel_bias.astype(F32)[bucket], (1, 0))


def _bias_table_kernel(bias_ref, out_ref):
    shape = (BLOCK, 3 * BLOCK)
    col = lax.broadcasted_iota(jnp.int32, shape, 1)
    rel = col - BLOCK - lax.broadcasted_iota(jnp.int32, shape, 0)
    in_window = jnp.abs(rel) <= WINDOW
    for h in range(N_Q_HEADS):
        by_rel = jnp.broadcast_to(bias_ref[h:h + 1, :], (BLOCK, 4 * BLOCK))
        band = pltpu.roll(by_rel, 3 * BLOCK + 1, axis=1, stride=1, stride_axis=0)[:, :3 * BLOCK]
        t = jnp.where(in_window, band, MASK_VALUE)
        out_ref[0, h] = t
        out_ref[1, h] = jnp.where(col < BLOCK, MASK_VALUE, t)
        out_ref[2, h] = jnp.where(col >= 2 * BLOCK, MASK_VALUE, t)


def _bias_table(rel_bias):
    return pl.pallas_call(
        _bias_table_kernel,
        out_shape=jax.ShapeDtypeStruct((3, N_Q_HEADS, BLOCK, 3 * BLOCK), F32),
        in_specs=[pl.BlockSpec(memory_space=pltpu.VMEM)],
        out_specs=pl.BlockSpec(memory_space=pltpu.VMEM),
        name="bias_table",
    )(_bias_by_rel(rel_bias))


def _stage_weight(src_hbm, dst, stage, sem):
    chunk = stage.shape[1]
    n_chunks = src_hbm.shape[0] // chunk
    assert n_chunks * chunk == src_hbm.shape[0] and stage.shape[2] == src_hbm.shape[1]

    def copy(k):
        return pltpu.make_async_copy(src_hbm.at[pl.ds(k * chunk, chunk)], stage.at[k % 2], sem.at[k % 2])

    copy(0).start()
    for k in range(n_chunks):
        if k + 1 < n_chunks:
            copy(k + 1).start()
        copy(k).wait()
        dst[k * chunk:(k + 1) * chunk] = stage[k % 2].astype(dst.dtype)


def _mixer_kernel(xp_ref, xc_ref, xn_ref, lng_ref, lnb_ref, win_hbm, bg_ref, sink_ref, bias_ref,
                  cw_ref, cbias_ref, wa_hbm, wc_hbm, wo_hbm, g2_ref, b2_ref, out_ref,
                  hext, hres, q_scr, kvar, vvar, att_scr, u_scr, conv_scr, g_scr, merged_scr,
                  win_ref, wa_ref, wc_ref, wo_ref, stage_in, stage_sq, dma_sem):
    i = pl.program_id(1)
    last_i = pl.num_programs(1) - 1

    @pl.when((pl.program_id(0) == 0) & (i == 0))
    def _():
        _stage_weight(win_hbm, win_ref, stage_in, dma_sem)
        _stage_weight(wa_hbm, wa_ref, stage_sq, dma_sem)
        _stage_weight(wc_hbm, wc_ref, stage_sq, dma_sem)
        _stage_weight(wo_hbm, wo_ref, stage_sq, dma_sem)

    lng = lng_ref[...]
    lnb = lnb_ref[...]
    ext = TILE + 2 * BLOCK

    def norm_rows(lo, hi):
        for r in range(lo, hi, LN_ROWS):
            if r < BLOCK:
                x = xp_ref[0, r:r + LN_ROWS]
            elif r < BLOCK + TILE:
                x = xc_ref[0, r - BLOCK:r - BLOCK + LN_ROWS]
            else:
                x = xn_ref[0, r - BLOCK - TILE:r - BLOCK - TILE + LN_ROWS]
            y = _layer_norm(x, lng, lnb)
            if BLOCK <= r < BLOCK + TILE:
                hres[r - BLOCK:r - BLOCK + LN_ROWS] = y
            hext[r:r + LN_ROWS] = y.astype(BF16)

    def project_qkv(lo, hi):
        rows = slice(lo, hi)
        kv = _dot(hext[rows], win_ref[:, K_OFF:K_OFF + 2 * KV_WIDTH])
        low = lax.broadcasted_iota(jnp.int32, (hi - lo, KV_WIDTH), 1) < HEAD_DIM
        for src, dst in ((kv[:, :KV_WIDTH], kvar), (kv[:, KV_WIDTH:], vvar)):
            rolled = pltpu.roll(src, HEAD_DIM, axis=1)
            dst[0, rows] = jnp.where(low, src, 0.0).astype(BF16)
            dst[1, rows] = jnp.where(low, 0.0, rolled).astype(BF16)
            dst[2, rows] = jnp.where(low, rolled, 0.0).astype(BF16)
            dst[3, rows] = jnp.where(low, 0.0, src).astype(BF16)
        q_lo, q_hi = max(lo, BLOCK), min(hi, BLOCK + TILE)
        q_scr[q_lo - BLOCK:q_hi - BLOCK] = (
            _dot(hext[q_lo:q_hi], win_ref[:, Q_OFF:Q_OFF + ATT_WIDTH]) * (HEAD_DIM ** -0.5)).astype(BF16)

    norm_rows(0, ext // 2)
    project_qkv(0, ext // 2)
    norm_rows(ext // 2, ext)
    project_qkv(ext // 2, ext)
    hcb = hext[BLOCK:BLOCK + TILE]

    lane_low = lax.broadcasted_iota(jnp.int32, (BLOCK, 2 * HEAD_DIM), 1) < HEAD_DIM

    def attend(qb, pair):
        if qb == 0:
            edge = jnp.where(i == 0, 1, 0)
        elif qb == QB_PER_TILE - 1:
            edge = jnp.where(i == last_i, 2, 0)
        else:
            edge = 0
        rows = slice(qb * BLOCK, qb * BLOCK + 3 * BLOCK)
        kvh = pair // (GROUP // 2)
        q2 = q_scr[qb * BLOCK:(qb + 1) * BLOCK, pair * 2 * HEAD_DIM:(pair + 1) * 2 * HEAD_DIM]
        scores = _dot_nt(q2, jnp.concatenate([kvar[2 * kvh, rows], kvar[2 * kvh + 1, rows]], axis=0))
        probs, inv = [], []
        for sub in range(2):
            h = 2 * pair + sub
            sink = sink_ref[h]
            p_blocks, inv_blocks = [], []
            for r in range(0, BLOCK, SM_ROWS):
                logits = (scores[r:r + SM_ROWS, sub * 3 * BLOCK:(sub + 1) * 3 * BLOCK]
                          + bias_ref[edge, h, r:r + SM_ROWS, :])
                m = jnp.maximum(jnp.max(logits, axis=-1, keepdims=True), sink)
                p = jnp.exp(logits - m)
                denom = jnp.sum(p, axis=-1, keepdims=True) + jnp.exp(sink - m)
                p_blocks.append(p.astype(BF16))
                inv_blocks.append(1.0 / denom)
            probs.append(jnp.concatenate(p_blocks, axis=0))
            inv.append(jnp.concatenate(inv_blocks, axis=0))
        o = _dot(jnp.concatenate(probs, axis=1),
                 jnp.concatenate([vvar[2 * kvh, rows], vvar[2 * kvh + 1, rows]], axis=0))
        o = o * jnp.where(lane_low, inv[0], inv[1])
        att_scr[qb * BLOCK:(qb + 1) * BLOCK, pair * 2 * HEAD_DIM:(pair + 1) * 2 * HEAD_DIM] = o.astype(BF16)

    lo = BLOCK - CONV_HALO
    hi = BLOCK + TILE + CONV_HALO
    slabs = PROJ_COLS // LANES

    def conv_input(j):
        cc = _dot(hext[lo:hi], win_ref[:, CC_OFF + j * PROJ_COLS:CC_OFF + (j + 1) * PROJ_COLS])
        cx = _dot(hext[lo:hi], win_ref[:, CX_OFF + j * PROJ_COLS:CX_OFF + (j + 1) * PROJ_COLS])
        row = lax.broadcasted_iota(jnp.int32, (TILE + 2 * CONV_HALO, 1), 0)
        inside = ((row >= CONV_HALO) | (i > 0)) & ((row < CONV_HALO + TILE) | (i < last_i))
        u = jnp.where(inside, cc * cx, 0.0)
        for s in range(slabs):
            u_scr[j * slabs + s, pl.ds(0, TILE + 2 * CONV_HALO, stride=ROW_PITCH), :] = (
                u[:, s * LANES:(s + 1) * LANES])

    def conv_branch(j):
        cb = _dot(hcb, win_ref[:, CB_OFF + j * PROJ_COLS:CB_OFF + (j + 1) * PROJ_COLS])
        taps = []
        for s in range(slabs):
            lanes = slice(j * PROJ_COLS + s * LANES, j * PROJ_COLS + (s + 1) * LANES)
            w = cw_ref[:, lanes]
            taps.append([jnp.broadcast_to(w[k:k + 1], (MIX_ROWS, LANES)) for k in range(3)]
                        + [jnp.broadcast_to(cbias_ref[:, lanes], (MIX_ROWS, LANES))])
        for rb in range(TILE // MIX_ROWS):
            r0 = CONV_HALO + rb * MIX_ROWS
            parts = []
            for s in range(slabs):
                ld = lambda r: u_scr[j * slabs + s, pl.ds(ROW_PITCH * r, MIX_ROWS, stride=ROW_PITCH), :]
                t = taps[s]
                dw = ld(r0 - 1) * t[0] + ld(r0) * t[1] + ld(r0 + 1) * t[2] + t[3]
                parts.append(cb[rb * MIX_ROWS:(rb + 1) * MIX_ROWS, s * LANES:(s + 1) * LANES] * dw)
            conv_scr[j, rb * MIX_ROWS:(rb + 1) * MIX_ROWS] = jnp.concatenate(parts, axis=1).astype(BF16)

    def gate(n):
        cols = slice(n * PROJ_COLS, (n + 1) * PROJ_COLS)
        pre = _dot(hcb, win_ref[:, GATE_OFF + n * PROJ_COLS:GATE_OFF + (n + 1) * PROJ_COLS])
        bias = bg_ref[:, cols]
        for rb in range(TILE // BLOCK):
            rows = slice(rb * BLOCK, (rb + 1) * BLOCK)
            g_scr[n, rows] = _sigmoid(pre[rows] + bias)

    n_conv = CONV_WIDTH // PROJ_COLS
    n_gate = 2 * D_MODEL // PROJ_COLS
    proj_units = ([functools.partial(conv_input, j) for j in range(n_conv)]
                  + [functools.partial(conv_branch, j) for j in range(n_conv)]
                  + [functools.partial(gate, n) for n in range(n_gate)])
    n_out = D_MODEL // PROJ_COLS
    half_rows = TILE // 2

    def merge(half, n):
        rows = slice(half * half_rows, (half + 1) * half_rows)
        cols = slice(n * PROJ_COLS, (n + 1) * PROJ_COLS)
        conv = jnp.concatenate([conv_scr[j, rows] for j in range(n_conv)], axis=1)
        ya = _dot(att_scr[rows], wa_ref[:, cols])
        yc = _dot(conv, wc_ref[:, cols])
        merged_scr[n, rows] = (g_scr[n, rows] * ya + g_scr[n_out + n, rows] * yc).astype(BF16)

    def out_proj(half):
        rows = slice(half * half_rows, (half + 1) * half_rows)
        return _dot(jnp.concatenate([merged_scr[n, rows] for n in range(n_out)], axis=1), wo_ref[...])

    def finish(half, mix, blocks):
        for blk in blocks:
            r = half * half_rows + blk * LN_ROWS
            out_ref[0, r:r + LN_ROWS] = _layer_norm(
                DEEPNORM_ALPHA * hres[r:r + LN_ROWS] + mix[blk * LN_ROWS:(blk + 1) * LN_ROWS],
                g2_ref[...], b2_ref[...])

    att_units = [(qb, pair) for qb in range(QB_PER_TILE) for pair in range(N_Q_HEADS // 2)]
    filler = proj_units + [functools.partial(merge, 0, n) for n in range(n_out)]
    assert len(filler) == len(att_units) and len(proj_units) >= len(att_units) // 2
    for (qb, pair), unit in zip(att_units, filler):
        attend(qb, pair)
        unit()
    ln_blocks = half_rows // LN_ROWS
    mix = out_proj(0)
    for n in range(n_out):
        finish(0, mix, range(n * ln_blocks // n_out, (n + 1) * ln_blocks // n_out))
        merge(1, n)
    finish(1, out_proj(1), range(ln_blocks))


def _const_spec(shape):
    return pl.BlockSpec(shape, lambda b, i: (0,) * len(shape), pipeline_mode=pl.Buffered(1))


def _mixer(x, ln_g, ln_b, w_in, b_gates, sink, bias_tbl, conv_w, conv_b, w_a, w_c, w_o, g2, b2):
    bsz, seq, _ = x.shape
    n_tiles = seq // TILE
    blocks_per_tile = TILE // BLOCK
    n_blocks = seq // BLOCK
    in_specs = [
        pl.BlockSpec((1, BLOCK, D_MODEL), lambda b, i: (b, jnp.maximum(i * blocks_per_tile - 1, 0), 0)),
        pl.BlockSpec((1, TILE, D_MODEL), lambda b, i: (b, i, 0)),
        pl.BlockSpec((1, BLOCK, D_MODEL),
                     lambda b, i: (b, jnp.minimum((i + 1) * blocks_per_tile, n_blocks - 1), 0)),
        _const_spec((1, D_MODEL)), _const_spec((1, D_MODEL)),
        pl.BlockSpec(memory_space=pl.ANY),
        _const_spec((1, 2 * D_MODEL)),
        pl.BlockSpec(memory_space=pltpu.SMEM),
        _const_spec((3, N_Q_HEADS, BLOCK, 3 * BLOCK)),
        _const_spec((3, CONV_WIDTH)), _const_spec((1, CONV_WIDTH)),
        pl.BlockSpec(memory_space=pl.ANY), pl.BlockSpec(memory_space=pl.ANY),
        pl.BlockSpec(memory_space=pl.ANY),
        _const_spec((1, D_MODEL)), _const_spec((1, D_MODEL)),
    ]
    ext = TILE + 2 * BLOCK
    return pl.pallas_call(
        _mixer_kernel,
        out_shape=jax.ShapeDtypeStruct((bsz, seq, D_MODEL), F32),
        grid=(bsz, n_tiles),
        in_specs=in_specs,
        out_specs=pl.BlockSpec((1, TILE, D_MODEL), lambda b, i: (b, i, 0)),
        scratch_shapes=[
            pltpu.VMEM((ext, D_MODEL), BF16),
            pltpu.VMEM((TILE, D_MODEL), F32),
            pltpu.VMEM((TILE, ATT_WIDTH), BF16),
            pltpu.VMEM((4, ext, KV_WIDTH), BF16),
            pltpu.VMEM((4, ext, KV_WIDTH), BF16),
            pltpu.VMEM((TILE, ATT_WIDTH), BF16),
            pltpu.VMEM((CONV_WIDTH // LANES, ROW_PITCH * (TILE + 2 * CONV_HALO), LANES), F32),
            pltpu.VMEM((CONV_WIDTH // PROJ_COLS, TILE, PROJ_COLS), BF16),
            pltpu.VMEM((2 * D_MODEL // PROJ_COLS, TILE, PROJ_COLS), F32),
            pltpu.VMEM((D_MODEL // PROJ_COLS, TILE, PROJ_COLS), BF16),
            pltpu.VMEM((D_MODEL, IN_PROJ_WIDTH), BF16),
            pltpu.VMEM((ATT_WIDTH, D_MODEL), BF16),
            pltpu.VMEM((CONV_WIDTH, D_MODEL), BF16),
            pltpu.VMEM((D_MODEL, D_MODEL), BF16),
            pltpu.VMEM((2, STAGE_ROWS_WIDE, IN_PROJ_WIDTH), F32),
            pltpu.VMEM((2, STAGE_ROWS, D_MODEL), F32),
            pltpu.SemaphoreType.DMA((2,)),
        ],
        compiler_params=pltpu.CompilerParams(
            dimension_semantics=("arbitrary", "arbitrary"), vmem_limit_bytes=VMEM_LIMIT_BYTES),
        name="mixer",
    )(x, x, x, ln_g, ln_b, w_in, b_gates, sink, bias_tbl, conv_w, conv_b, w_a, w_c, w_o, g2, b2)


def _ffn_kernel(hp_ref, hc_ref, hn_ref, wup_hbm, cw_ref, cb_ref, wdn_hbm, g_ref, b_ref, out_ref,
                hext, a_scr, u_scr, act_scr, acc, wup_ref, wdn_ref, stage_up, stage_dn, dma_sem):
    i = pl.program_id(1)
    last_i = pl.num_programs(1) - 1

    @pl.when((pl.program_id(0) == 0) & (i == 0))
    def _():
        _stage_weight(wup_hbm, wup_ref, stage_up, dma_sem)
        _stage_weight(wdn_hbm, wdn_ref, stage_dn, dma_sem)

    hext[0:CONV_HALO] = jnp.where(i > 0, hp_ref[0], 0.0).astype(BF16)
    hext[CONV_HALO:CONV_HALO + TILE] = hc_ref[0].astype(BF16)
    hext[CONV_HALO + TILE:] = jnp.where(i < last_i, hn_ref[0], 0.0).astype(BF16)

    tile_half = TILE // 2
    ext_split = (0, tile_half + 2 * CONV_HALO, TILE + 2 * CONV_HALO)

    def up_proj(c, half):
        col = c * FF_CHUNK
        start, stop = ext_split[half], ext_split[half + 1]
        dst = pl.ds(ROW_PITCH * start, stop - start, stride=ROW_PITCH)
        for scr, off in ((a_scr, col), (u_scr, D_FF + col)):
            res = _dot(hext[start:stop], wup_ref[:, off:off + FF_CHUNK])
            for slab in range(FF_CHUNK // LANES):
                scr[c % 2, slab, dst, :] = res[:, slab * LANES:(slab + 1) * LANES]

    def down_proj(c, half):
        rows = slice(half * tile_half, (half + 1) * tile_half)
        part = _dot(act_scr[c % 2, rows], wdn_ref[c * FF_CHUNK:(c + 1) * FF_CHUNK, :])
        if c == 0:
            acc[rows] = part
        else:
            acc[rows] += part

    def finish(half):
        rows = slice(half * tile_half, (half + 1) * tile_half)
        out_ref[0, rows] = _layer_norm(DEEPNORM_ALPHA * hc_ref[0, rows] + acc[rows], g_ref[...], b_ref[...])

    def conv_taps(col):
        out = []
        for slab in range(FF_CHUNK // LANES):
            lanes = slice(col + slab * LANES, col + (slab + 1) * LANES)
            w = cw_ref[:, lanes]
            out.append([jnp.broadcast_to(w[k:k + 1], (FF_ROWS, LANES)) for k in range(3)]
                       + [jnp.broadcast_to(cb_ref[:, lanes], (FF_ROWS, LANES))])
        return out

    def conv3(scr, slot, slab, r0, taps):
        rows = lambda r: scr[slot, slab, pl.ds(ROW_PITCH * r, FF_ROWS, stride=ROW_PITCH), :]
        return rows(r0 - 1) * taps[0] + rows(r0) * taps[1] + rows(r0 + 1) * taps[2] + taps[3]

    def gate_rows(c, rb, a_taps, u_taps):
        r0 = CONV_HALO + rb * FF_ROWS
        parts = []
        for slab in range(FF_CHUNK // LANES):
            a = conv3(a_scr, c % 2, slab, r0, a_taps[slab])
            u = conv3(u_scr, c % 2, slab, r0, u_taps[slab])
            parts.append(a * _sigmoid(a) * u)
        act_scr[c % 2, rb * FF_ROWS:(rb + 1) * FF_ROWS] = jnp.concatenate(parts, axis=1).astype(BF16)

    units = [(c, half) for c in range(N_FF_CHUNKS) for half in range(2)]
    blocks_per_half = tile_half // FF_ROWS
    up_proj(*units[0])
    taps = {}
    for k, (c, half) in enumerate(units):
        if half == 0:
            taps = {"a": conv_taps(c * FF_CHUNK), "u": conv_taps(D_FF + c * FF_CHUNK)}
        mxu_units = []
        if k + 1 < len(units):
            mxu_units.append(functools.partial(up_proj, *units[k + 1]))
        if k >= 1:
            mxu_units.append(functools.partial(down_proj, *units[k - 1]))
        per_unit = blocks_per_half // len(mxu_units)
        for m, unit in enumerate(mxu_units):
            unit()
            for rb in range(m * per_unit, (m + 1) * per_unit):
                gate_rows(c, half * blocks_per_half + rb, taps["a"], taps["u"])
    finish(0)
    down_proj(*units[-1])
    finish(1)


def _ffn(h, w_up, conv_w, conv_b, w_down, g, b):
    bsz, seq, _ = h.shape
    n_tiles = seq // TILE
    halo_per_tile = TILE // CONV_HALO
    n_halo_blocks = seq // CONV_HALO
    in_specs = [
        pl.BlockSpec((1, CONV_HALO, D_MODEL), lambda b, i: (b, jnp.maximum(i * halo_per_tile - 1, 0), 0)),
        pl.BlockSpec((1, TILE, D_MODEL), lambda b, i: (b, i, 0)),
        pl.BlockSpec((1, CONV_HALO, D_MODEL),
                     lambda b, i: (b, jnp.minimum((i + 1) * halo_per_tile, n_halo_blocks - 1), 0)),
        pl.BlockSpec(memory_space=pl.ANY),
        _const_spec((3, 2 * D_FF)), _const_spec((1, 2 * D_FF)),
        pl.BlockSpec(memory_space=pl.ANY),
        _const_spec((1, D_MODEL)), _const_spec((1, D_MODEL)),
    ]
    ext = TILE + 2 * CONV_HALO
    return pl.pallas_call(
        _ffn_kernel,
        out_shape=jax.ShapeDtypeStruct((bsz, seq, D_MODEL), F32),
        grid=(bsz, n_tiles),
        in_specs=in_specs,
        out_specs=pl.BlockSpec((1, TILE, D_MODEL), lambda b, i: (b, i, 0)),
        scratch_shapes=[
            pltpu.VMEM((ext, D_MODEL), BF16),
            pltpu.VMEM((2, FF_CHUNK // LANES, ROW_PITCH * ext, LANES), F32),
            pltpu.VMEM((2, FF_CHUNK // LANES, ROW_PITCH * ext, LANES), F32),
            pltpu.VMEM((2, TILE, FF_CHUNK), BF16),
            pltpu.VMEM((TILE, D_MODEL), F32),
            pltpu.VMEM((D_MODEL, 2 * D_FF), BF16),
            pltpu.VMEM((D_FF, D_MODEL), BF16),
            pltpu.VMEM((2, STAGE_ROWS_WIDE // 2, 2 * D_FF), F32),
            pltpu.VMEM((2, STAGE_ROWS, D_MODEL), F32),
            pltpu.SemaphoreType.DMA((2,)),
        ],
        compiler_params=pltpu.CompilerParams(
            dimension_semantics=("arbitrary", "arbitrary"), vmem_limit_bytes=VMEM_LIMIT_BYTES),
        name="ffn",
    )(h, h, h, w_up, conv_w, conv_b, w_down, g, b)


def kernel(x, ln_in_g, ln_in_b, w_in, b_gates, attn_sink, rel_bias, conv_w, conv_b, w_att_branch,
           w_conv_branch, w_o, ln_mix_g, ln_mix_b, w_ffn_up, ffn_conv_w, ffn_conv_b, w_ffn_down,
           ln_ffn_g, ln_ffn_b):
    assert w_in.shape[0] == DEPTH == 1
    row = lambda v: v.reshape(1, -1)
    bias_tbl = _bias_table(rel_bias)
    h = _mixer(x, row(ln_in_g), row(ln_in_b), w_in[0], row(b_gates[0]), attn_sink[0],
               bias_tbl, conv_w[0], row(conv_b[0]), w_att_branch[0],
               w_conv_branch[0], w_o[0], row(ln_mix_g[0]), row(ln_mix_b[0]))
    return _ffn(h, w_ffn_up[0], ffn_conv_w[0], row(ffn_conv_b[0]),
                w_ffn_down[0], row(ln_ffn_g[0]), row(ln_ffn_b[0]))
```

```python
import functools
import math

import jax
import jax.numpy as jnp
import numpy as np
from jax import lax
from jax.experimental import pallas as pl
from jax.experimental.pallas import tpu as pltpu

D_MODEL = 1024
HEAD_DIM = 64
N_Q_HEADS = 8
N_KV_HEADS = 2
GROUP = N_Q_HEADS // N_KV_HEADS
ATT_WIDTH = N_Q_HEADS * HEAD_DIM
KV_WIDTH = N_KV_HEADS * HEAD_DIM
WINDOW = 128
BLOCK = 128
CONV_WIDTH = D_MODEL // 2
D_FF = 2816
N_BUCKETS = 32
MAX_DISTANCE = 128
LN_EPS = 1e-5
DEPTH = 1
DEEPNORM_ALPHA = (2 * DEPTH) ** 0.25
MASK_VALUE = -1e30

Q_OFF = 0
K_OFF = ATT_WIDTH
V_OFF = K_OFF + KV_WIDTH
CB_OFF = V_OFF + KV_WIDTH
CC_OFF = CB_OFF + CONV_WIDTH
CX_OFF = CC_OFF + CONV_WIDTH
GATE_OFF = CX_OFF + CONV_WIDTH
IN_PROJ_WIDTH = GATE_OFF + 2 * D_MODEL

TILE = 512
QB_PER_TILE = TILE // BLOCK
CONV_HALO = 16
FF_CHUNK = 256
N_FF_CHUNKS = D_FF // FF_CHUNK
FF_ROWS = 32
PROJ_COLS = 256
MIX_ROWS = 32
STAGE_ROWS = 256
STAGE_ROWS_WIDE = 128
LN_ROWS = 32
SM_ROWS = 32
LANES = 128
ROW_PITCH = 2
VMEM_LIMIT_BYTES = 60 * 1024 * 1024

F32 = jnp.float32
BF16 = jnp.bfloat16


def _layer_norm(x, g, b):
    mu = jnp.mean(x, axis=-1, keepdims=True)
    xc = x - mu
    var = jnp.mean(xc * xc, axis=-1, keepdims=True)
    return xc * lax.rsqrt(var + LN_EPS) * g + b


def _sigmoid(x):
    return 0.5 * jnp.tanh(0.5 * x) + 0.5


def _dot(a, b):
    return jnp.dot(a, b, preferred_element_type=F32)


def _dot_nt(a, b):
    return lax.dot_general(a, b, (((1,), (1,)), ((), ())), preferred_element_type=F32)


def _bias_by_rel(rel_bias):
    rel = jnp.arange(4 * BLOCK) - (2 * BLOCK - 1)
    half = N_BUCKETS // 2
    max_exact = half // 2
    offset = jnp.where(rel > 0, half, 0)
    n = jnp.abs(rel)
    nf = jnp.maximum(n, 1).astype(jnp.float32)
    large = max_exact + (jnp.log(nf / max_exact) / math.log(MAX_DISTANCE / max_exact)
                         * (half - max_exact)).astype(jnp.int32)
    large = jnp.minimum(large, half - 1)
    bucket = offset + jnp.where(n < max_exact, n, large)
    return jnp.transpose(rel_bias.astype(F32)[bucket], (1, 0))


def _bias_table_kernel(bias_ref, out_ref):
    shape = (BLOCK, 3 * BLOCK)
    col = lax.broadcasted_iota(jnp.int32, shape, 1)
    rel = col - BLOCK - lax.broadcasted_iota(jnp.int32, shape, 0)
    in_window = jnp.abs(rel) <= WINDOW
    for h in range(N_Q_HEADS):
        by_rel = jnp.broadcast_to(bias_ref[h:h + 1, :], (BLOCK, 4 * BLOCK))
        band = pltpu.roll(by_rel, 3 * BLOCK + 1, axis=1, stride=1, stride_axis=0)[:, :3 * BLOCK]
        t = jnp.where(in_window, band, MASK_VALUE)
        out_ref[0, h] = t
        out_ref[1, h] = jnp.where(col < BLOCK, MASK_VALUE, t)
        out_ref[2, h] = jnp.where(col >= 2 * BLOCK, MASK_VALUE, t)


def _bias_table(rel_bias):
    return pl.pallas_call(
        _bias_table_kernel,
        out_shape=jax.ShapeDtypeStruct((3, N_Q_HEADS, BLOCK, 3 * BLOCK), F32),
        in_specs=[pl.BlockSpec(memory_space=pltpu.VMEM)],
        out_specs=pl.BlockSpec(memory_space=pltpu.VMEM),
        name="bias_table",
    )(_bias_by_rel(rel_bias))


def _stage_weight(src_hbm, dst, stage, sem):
    chunk = stage.shape[1]
    n_chunks = src_hbm.shape[0] // chunk
    assert n_chunks * chunk == src_hbm.shape[0] and stage.shape[2] == src_hbm.shape[1]

    def copy(k):
        return pltpu.make_async_copy(src_hbm.at[pl.ds(k * chunk, chunk)], stage.at[k % 2], sem.at[k % 2])

    copy(0).start()
    for k in range(n_chunks):
        if k + 1 < n_chunks:
            copy(k + 1).start()
        copy(k).wait()
        dst[k * chunk:(k + 1) * chunk] = stage[k % 2].astype(dst.dtype)


def _mixer_kernel(xp_ref, xc_ref, xn_ref, lng_ref, lnb_ref, win_hbm, bg_ref, sink_ref, bias_ref,
                  cw_ref, cbias_ref, wa_hbm, wc_hbm, wo_hbm, g2_ref, b2_ref, out_ref,
                  hext, hres, q_scr, kvar, vvar, att_scr, u_scr, conv_scr, g_scr, merged_scr,
                  win_ref, wa_ref, wc_ref, wo_ref, stage_in, stage_sq, dma_sem):
    i = pl.program_id(1)
    last_i = pl.num_programs(1) - 1

    @pl.when((pl.program_id(0) == 0) & (i == 0))
    def _():
        _stage_weight(win_hbm, win_ref, stage_in, dma_sem)
        _stage_weight(wa_hbm, wa_ref, stage_sq, dma_sem)
        _stage_weight(wc_hbm, wc_ref, stage_sq, dma_sem)
        _stage_weight(wo_hbm, wo_ref, stage_sq, dma_sem)

    lng = lng_ref[...]
    lnb = lnb_ref[...]
    ext = TILE + 2 * BLOCK

    def norm_rows(lo, hi):
        for r in range(lo, hi, LN_ROWS):
            if r < BLOCK:
                x = xp_ref[0, r:r + LN_ROWS]
            elif r < BLOCK + TILE:
                x = xc_ref[0, r - BLOCK:r - BLOCK + LN_ROWS]
            else:
                x = xn_ref[0, r - BLOCK - TILE:r - BLOCK - TILE + LN_ROWS]
            y = _layer_norm(x, lng, lnb)
            if BLOCK <= r < BLOCK + TILE:
                hres[r - BLOCK:r - BLOCK + LN_ROWS] = y
            hext[r:r + LN_ROWS] = y.astype(BF16)

    def project_qkv(lo, hi):
        rows = slice(lo, hi)
        kv = _dot(hext[rows], win_ref[:, K_OFF:K_OFF + 2 * KV_WIDTH])
        low = lax.broadcasted_iota(jnp.int32, (hi - lo, KV_WIDTH), 1) < HEAD_DIM
        for src, dst in ((kv[:, :KV_WIDTH], kvar), (kv[:, KV_WIDTH:], vvar)):
            rolled = pltpu.roll(src, HEAD_DIM, axis=1)
            dst[0, rows] = jnp.where(low, src, 0.0).astype(BF16)
            dst[1, rows] = jnp.where(low, 0.0, rolled).astype(BF16)
            dst[2, rows] = jnp.where(low, rolled, 0.0).astype(BF16)
            dst[3, rows] = jnp.where(low, 0.0, src).astype(BF16)
        q_lo, q_hi = max(lo, BLOCK), min(hi, BLOCK + TILE)
        q_scr[q_lo - BLOCK:q_hi - BLOCK] = (
            _dot(hext[q_lo:q_hi], win_ref[:, Q_OFF:Q_OFF + ATT_WIDTH]) * (HEAD_DIM ** -0.5)).astype(BF16)

    norm_rows(0, ext // 2)
    project_qkv(0, ext // 2)
    norm_rows(ext // 2, ext)
    project_qkv(ext // 2, ext)
    hcb = hext[BLOCK:BLOCK + TILE]

    lane_low = lax.broadcasted_iota(jnp.int32, (BLOCK, 2 * HEAD_DIM), 1) < HEAD_DIM

    def att_scores(qb, pair):
        rows = slice(qb * BLOCK, qb * BLOCK + 3 * BLOCK)
        kvh = pair // (GROUP // 2)
        q2 = q_scr[qb * BLOCK:(qb + 1) * BLOCK, pair * 2 * HEAD_DIM:(pair + 1) * 2 * HEAD_DIM]
        return _dot_nt(q2, jnp.concatenate([kvar[2 * kvh, rows], kvar[2 * kvh + 1, rows]], axis=0))

    def att_softmax(qb, pair, scores):
        if qb == 0:
            edge = jnp.where(i == 0, 1, 0)
        elif qb == QB_PER_TILE - 1:
            edge = jnp.where(i == last_i, 2, 0)
        else:
            edge = 0
        probs, inv = [], []
        for sub in range(2):
            h = 2 * pair + sub
            sink = sink_ref[h]
            p_blocks, inv_blocks = [], []
            for r in range(0, BLOCK, SM_ROWS):
                logits = (scores[r:r + SM_ROWS, sub * 3 * BLOCK:(sub + 1) * 3 * BLOCK]
                          + bias_ref[edge, h, r:r + SM_ROWS, :])
                m = jnp.maximum(jnp.max(logits, axis=-1, keepdims=True), sink)
                p = jnp.exp(logits - m)
                denom = jnp.sum(p, axis=-1, keepdims=True) + jnp.exp(sink - m)
                p_blocks.append(p.astype(BF16))
                inv_blocks.append(1.0 / denom)
            probs.append(jnp.concatenate(p_blocks, axis=0))
            inv.append(jnp.concatenate(inv_blocks, axis=0))
        return jnp.concatenate(probs, axis=1), jnp.where(lane_low, inv[0], inv[1])

    def att_values(qb, pair, probs, scale):
        rows = slice(qb * BLOCK, qb * BLOCK + 3 * BLOCK)
        kvh = pair // (GROUP // 2)
        o = _dot(probs, jnp.concatenate([vvar[2 * kvh, rows], vvar[2 * kvh + 1, rows]], axis=0))
        att_scr[qb * BLOCK:(qb + 1) * BLOCK, pair * 2 * HEAD_DIM:(pair + 1) * 2 * HEAD_DIM] = (
            o * scale).astype(BF16)

    lo = BLOCK - CONV_HALO
    hi = BLOCK + TILE + CONV_HALO
    slabs = PROJ_COLS // LANES

    def conv_input(j):
        cc = _dot(hext[lo:hi], win_ref[:, CC_OFF + j * PROJ_COLS:CC_OFF + (j + 1) * PROJ_COLS])
        cx = _dot(hext[lo:hi], win_ref[:, CX_OFF + j * PROJ_COLS:CX_OFF + (j + 1) * PROJ_COLS])
        row = lax.broadcasted_iota(jnp.int32, (TILE + 2 * CONV_HALO, 1), 0)
        inside = ((row >= CONV_HALO) | (i > 0)) & ((row < CONV_HALO + TILE) | (i < last_i))
        u = jnp.where(inside, cc * cx, 0.0)
        for s in range(slabs):
            u_scr[j * slabs + s, pl.ds(0, TILE + 2 * CONV_HALO, stride=ROW_PITCH), :] = (
                u[:, s * LANES:(s + 1) * LANES])

    def conv_branch(j):
        cb = _dot(hcb, win_ref[:, CB_OFF + j * PROJ_COLS:CB_OFF + (j + 1) * PROJ_COLS])
        taps = []
        for s in range(slabs):
            lanes = slice(j * PROJ_COLS + s * LANES, j * PROJ_COLS + (s + 1) * LANES)
            w = cw_ref[:, lanes]
            taps.append([jnp.broadcast_to(w[k:k + 1], (MIX_ROWS, LANES)) for k in range(3)]
                        + [jnp.broadcast_to(cbias_ref[:, lanes], (MIX_ROWS, LANES))])
        for rb in range(TILE // MIX_ROWS):
            r0 = CONV_HALO + rb * MIX_ROWS
            parts = []
            for s in range(slabs):
                ld = lambda r: u_scr[j * slabs + s, pl.ds(ROW_PITCH * r, MIX_ROWS, stride=ROW_PITCH), :]
                t = taps[s]
                dw = ld(r0 - 1) * t[0] + ld(r0) * t[1] + ld(r0 + 1) * t[2] + t[3]
                parts.append(cb[rb * MIX_ROWS:(rb + 1) * MIX_ROWS, s * LANES:(s + 1) * LANES] * dw)
            conv_scr[j, rb * MIX_ROWS:(rb + 1) * MIX_ROWS] = jnp.concatenate(parts, axis=1).astype(BF16)

    def gate(n):
        cols = slice(n * PROJ_COLS, (n + 1) * PROJ_COLS)
        pre = _dot(hcb, win_ref[:, GATE_OFF + n * PROJ_COLS:GATE_OFF + (n + 1) * PROJ_COLS])
        bias = bg_ref[:, cols]
        for rb in range(TILE // BLOCK):
            rows = slice(rb * BLOCK, (rb + 1) * BLOCK)
            g_scr[n, rows] = _sigmoid(pre[rows] + bias)

    n_conv = CONV_WIDTH // PROJ_COLS
    n_gate = 2 * D_MODEL // PROJ_COLS
    proj_units = ([functools.partial(conv_input, j) for j in range(n_conv)]
                  + [functools.partial(conv_branch, j) for j in range(n_conv)]
                  + [functools.partial(gate, n) for n in range(n_gate)])
    n_out = D_MODEL // PROJ_COLS
    half_rows = TILE // 2

    def merge(half, n):
        rows = slice(half * half_rows, (half + 1) * half_rows)
        cols = slice(n * PROJ_COLS, (n + 1) * PROJ_COLS)
        conv = jnp.concatenate([conv_scr[j, rows] for j in range(n_conv)], axis=1)
        ya = _dot(att_scr[rows], wa_ref[:, cols])
        yc = _dot(conv, wc_ref[:, cols])
        merged_scr[n, rows] = (g_scr[n, rows] * ya + g_scr[n_out + n, rows] * yc).astype(BF16)

    def out_proj(half):
        rows = slice(half * half_rows, (half + 1) * half_rows)
        return _dot(jnp.concatenate([merged_scr[n, rows] for n in range(n_out)], axis=1), wo_ref[...])

    def finish(half, mix, blocks):
        for blk in blocks:
            r = half * half_rows + blk * LN_ROWS
            out_ref[0, r:r + LN_ROWS] = _layer_norm(
                DEEPNORM_ALPHA * hres[r:r + LN_ROWS] + mix[blk * LN_ROWS:(blk + 1) * LN_ROWS],
                g2_ref[...], b2_ref[...])

    att_units = [(qb, pair) for qb in range(QB_PER_TILE) for pair in range(N_Q_HEADS // 2)]
    filler = proj_units + [functools.partial(merge, 0, n) for n in range(n_out)]
    assert len(filler) == len(att_units) and len(proj_units) >= len(att_units) // 2
    scores = att_scores(*att_units[0])
    for k, unit in enumerate(att_units):
        ahead = att_scores(*att_units[k + 1]) if k + 1 < len(att_units) else None
        probs, scale = att_softmax(*unit, scores)
        filler[k]()
        att_values(*unit, probs, scale)
        scores = ahead
    ln_blocks = half_rows // LN_ROWS
    mix = out_proj(0)
    for n in range(n_out):
        finish(0, mix, range(n * ln_blocks // n_out, (n + 1) * ln_blocks // n_out))
        merge(1, n)
    finish(1, out_proj(1), range(ln_blocks))


def _const_spec(shape):
    return pl.BlockSpec(shape, lambda b, i: (0,) * len(shape), pipeline_mode=pl.Buffered(1))


def _mixer(x, ln_g, ln_b, w_in, b_gates, sink, bias_tbl, conv_w, conv_b, w_a, w_c, w_o, g2, b2):
    bsz, seq, _ = x.shape
    n_tiles = seq // TILE
    blocks_per_tile = TILE // BLOCK
    n_blocks = seq // BLOCK
    in_specs = [
        pl.BlockSpec((1, BLOCK, D_MODEL), lambda b, i: (b, jnp.maximum(i * blocks_per_tile - 1, 0), 0)),
        pl.BlockSpec((1, TILE, D_MODEL), lambda b, i: (b, i, 0)),
        pl.BlockSpec((1, BLOCK, D_MODEL),
                     lambda b, i: (b, jnp.minimum((i + 1) * blocks_per_tile, n_blocks - 1), 0)),
        _const_spec((1, D_MODEL)), _const_spec((1, D_MODEL)),
        pl.BlockSpec(memory_space=pl.ANY),
        _const_spec((1, 2 * D_MODEL)),
        pl.BlockSpec(memory_space=pltpu.SMEM),
        _const_spec((3, N_Q_HEADS, BLOCK, 3 * BLOCK)),
        _const_spec((3, CONV_WIDTH)), _const_spec((1, CONV_WIDTH)),
        pl.BlockSpec(memory_space=pl.ANY), pl.BlockSpec(memory_space=pl.ANY),
        pl.BlockSpec(memory_space=pl.ANY),
        _const_spec((1, D_MODEL)), _const_spec((1, D_MODEL)),
    ]
    ext = TILE + 2 * BLOCK
    return pl.pallas_call(
        _mixer_kernel,
        out_shape=jax.ShapeDtypeStruct((bsz, seq, D_MODEL), F32),
        grid=(bsz, n_tiles),
        in_specs=in_specs,
        out_specs=pl.BlockSpec((1, TILE, D_MODEL), lambda b, i: (b, i, 0)),
        scratch_shapes=[
            pltpu.VMEM((ext, D_MODEL), BF16),
            pltpu.VMEM((TILE, D_MODEL), F32),
            pltpu.VMEM((TILE, ATT_WIDTH), BF16),
            pltpu.VMEM((4, ext, KV_WIDTH), BF16),
            pltpu.VMEM((4, ext, KV_WIDTH), BF16),
            pltpu.VMEM((TILE, ATT_WIDTH), BF16),
            pltpu.VMEM((CONV_WIDTH // LANES, ROW_PITCH * (TILE + 2 * CONV_HALO), LANES), F32),
            pltpu.VMEM((CONV_WIDTH // PROJ_COLS, TILE, PROJ_COLS), BF16),
            pltpu.VMEM((2 * D_MODEL // PROJ_COLS, TILE, PROJ_COLS), F32),
            pltpu.VMEM((D_MODEL // PROJ_COLS, TILE, PROJ_COLS), BF16),
            pltpu.VMEM((D_MODEL, IN_PROJ_WIDTH), BF16),
            pltpu.VMEM((ATT_WIDTH, D_MODEL), BF16),
            pltpu.VMEM((CONV_WIDTH, D_MODEL), BF16),
            pltpu.VMEM((D_MODEL, D_MODEL), BF16),
            pltpu.VMEM((2, STAGE_ROWS_WIDE, IN_PROJ_WIDTH), F32),
            pltpu.VMEM((2, STAGE_ROWS, D_MODEL), F32),
            pltpu.SemaphoreType.DMA((2,)),
        ],
        compiler_params=pltpu.CompilerParams(
            dimension_semantics=("arbitrary", "arbitrary"), vmem_limit_bytes=VMEM_LIMIT_BYTES),
        name="mixer",
    )(x, x, x, ln_g, ln_b, w_in, b_gates, sink, bias_tbl, conv_w, conv_b, w_a, w_c, w_o, g2, b2)


def _ffn_kernel(hp_ref, hc_ref, hn_ref, wup_hbm, cw_ref, cb_ref, wdn_hbm, g_ref, b_ref, out_ref,
                hext, a_scr, u_scr, act_scr, acc, wup_ref, wdn_ref, stage_up, stage_dn, dma_sem):
    i = pl.program_id(1)
    last_i = pl.num_programs(1) - 1

    @pl.when((pl.program_id(0) == 0) & (i == 0))
    def _():
        _stage_weight(wup_hbm, wup_ref, stage_up, dma_sem)
        _stage_weight(wdn_hbm, wdn_ref, stage_dn, dma_sem)

    hext[0:CONV_HALO] = jnp.where(i > 0, hp_ref[0], 0.0).astype(BF16)
    hext[CONV_HALO:CONV_HALO + TILE] = hc_ref[0].astype(BF16)
    hext[CONV_HALO + TILE:] = jnp.where(i < last_i, hn_ref[0], 0.0).astype(BF16)

    tile_half = TILE // 2
    ext_split = (0, tile_half + 2 * CONV_HALO, TILE + 2 * CONV_HALO)

    def up_proj(c, half):
        col = c * FF_CHUNK
        start, stop = ext_split[half], ext_split[half + 1]
        dst = pl.ds(ROW_PITCH * start, stop - start, stride=ROW_PITCH)
        for scr, off in ((a_scr, col), (u_scr, D_FF + col)):
            res = _dot(hext[start:stop], wup_ref[:, off:off + FF_CHUNK])
            for slab in range(FF_CHUNK // LANES):
                scr[c % 2, slab, dst, :] = res[:, slab * LANES:(slab + 1) * LANES]

    def down_proj(c, half):
        rows = slice(half * tile_half, (half + 1) * tile_half)
        part = _dot(act_scr[c % 2, rows], wdn_ref[c * FF_CHUNK:(c + 1) * FF_CHUNK, :])
        if c == 0:
            acc[rows] = part
        else:
            acc[rows] += part

    def finish(half):
        rows = slice(half * tile_half, (half + 1) * tile_half)
        out_ref[0, rows] = _layer_norm(DEEPNORM_ALPHA * hc_ref[0, rows] + acc[rows], g_ref[...], b_ref[...])

    def conv_taps(col):
        out = []
        for slab in range(FF_CHUNK // LANES):
            lanes = slice(col + slab * LANES, col + (slab + 1) * LANES)
            w = cw_ref[:, lanes]
            out.append([jnp.broadcast_to(w[k:k + 1], (FF_ROWS, LANES)) for k in range(3)]
                       + [jnp.broadcast_to(cb_ref[:, lanes], (FF_ROWS, LANES))])
        return out

    def conv3(scr, slot, slab, r0, taps):
        rows = lambda r: scr[slot, slab, pl.ds(ROW_PITCH * r, FF_ROWS, stride=ROW_PITCH), :]
        return rows(r0 - 1) * taps[0] + rows(r0) * taps[1] + rows(r0 + 1) * taps[2] + taps[3]

    def gate_rows(c, rb, a_taps, u_taps):
        r0 = CONV_HALO + rb * FF_ROWS
        parts = []
        for slab in range(FF_CHUNK // LANES):
            a = conv3(a_scr, c % 2, slab, r0, a_taps[slab])
            u = conv3(u_scr, c % 2, slab, r0, u_taps[slab])
            parts.append(a * _sigmoid(a) * u)
        act_scr[c % 2, rb * FF_ROWS:(rb + 1) * FF_ROWS] = jnp.concatenate(parts, axis=1).astype(BF16)

    units = [(c, half) for c in range(N_FF_CHUNKS) for half in range(2)]
    blocks_per_half = tile_half // FF_ROWS
    up_proj(*units[0])
    taps = {}
    for k, (c, half) in enumerate(units):
        if half == 0:
            taps = {"a": conv_taps(c * FF_CHUNK), "u": conv_taps(D_FF + c * FF_CHUNK)}
        mxu_units = []
        if k + 1 < len(units):
            mxu_units.append(functools.partial(up_proj, *units[k + 1]))
        if k >= 1:
            mxu_units.append(functools.partial(down_proj, *units[k - 1]))
        per_unit = blocks_per_half // len(mxu_units)
        for m, unit in enumerate(mxu_units):
            unit()
            for rb in range(m * per_unit, (m + 1) * per_unit):
                gate_rows(c, half * blocks_per_half + rb, taps["a"], taps["u"])
    finish(0)
    down_proj(*units[-1])
    finish(1)


def _ffn(h, w_up, conv_w, conv_b, w_down, g, b):
    bsz, seq, _ = h.shape
    n_tiles = seq // TILE
    halo_per_tile = TILE // CONV_HALO
    n_halo_blocks = seq // CONV_HALO
    in_specs = [
        pl.BlockSpec((1, CONV_HALO, D_MODEL), lambda b, i: (b, jnp.maximum(i * halo_per_tile - 1, 0), 0)),
        pl.BlockSpec((1, TILE, D_MODEL), lambda b, i: (b, i, 0)),
        pl.BlockSpec((1, CONV_HALO, D_MODEL),
                     lambda b, i: (b, jnp.minimum((i + 1) * halo_per_tile, n_halo_blocks - 1), 0)),
        pl.BlockSpec(memory_space=pl.ANY),
        _const_spec((3, 2 * D_FF)), _const_spec((1, 2 * D_FF)),
        pl.BlockSpec(memory_space=pl.ANY),
        _const_spec((1, D_MODEL)), _const_spec((1, D_MODEL)),
    ]
    ext = TILE + 2 * CONV_HALO
    return pl.pallas_call(
        _ffn_kernel,
        out_shape=jax.ShapeDtypeStruct((bsz, seq, D_MODEL), F32),
        grid=(bsz, n_tiles),
        in_specs=in_specs,
        out_specs=pl.BlockSpec((1, TILE, D_MODEL), lambda b, i: (b, i, 0)),
        scratch_shapes=[
            pltpu.VMEM((ext, D_MODEL), BF16),
            pltpu.VMEM((2, FF_CHUNK // LANES, ROW_PITCH * ext, LANES), F32),
            pltpu.VMEM((2, FF_CHUNK // LANES, ROW_PITCH * ext, LANES), F32),
            pltpu.VMEM((2, TILE, FF_CHUNK), BF16),
            pltpu.VMEM((TILE, D_MODEL), F32),
            pltpu.VMEM((D_MODEL, 2 * D_FF), BF16),
            pltpu.VMEM((D_FF, D_MODEL), BF16),
            pltpu.VMEM((2, STAGE_ROWS_WIDE // 2, 2 * D_FF), F32),
            pltpu.VMEM((2, STAGE_ROWS, D_MODEL), F32),
            pltpu.SemaphoreType.DMA((2,)),
        ],
        compiler_params=pltpu.CompilerParams(
            dimension_semantics=("arbitrary", "arbitrary"), vmem_limit_bytes=VMEM_LIMIT_BYTES),
        name="ffn",
    )(h, h, h, w_up, conv_w, conv_b, w_down, g, b)


def kernel(x, ln_in_g, ln_in_b, w_in, b_gates, attn_sink, rel_bias, conv_w, conv_b, w_att_branch,
           w_conv_branch, w_o, ln_mix_g, ln_mix_b, w_ffn_up, ffn_conv_w, ffn_conv_b, w_ffn_down,
           ln_ffn_g, ln_ffn_b):
    assert w_in.shape[0] == DEPTH == 1
    row = lambda v: v.reshape(1, -1)
    bias_tbl = _bias_table(rel_bias)
    h = _mixer(x, row(ln_in_g), row(ln_in_b), w_in[0], row(b_gates[0]), attn_sink[0],
               bias_tbl, conv_w[0], row(conv_b[0]), w_att_branch[0],
               w_conv_branch[0], w_o[0], row(ln_mix_g[0]), row(ln_mix_b[0]))
    return _ffn(h, w_ffn_up[0], ffn_conv_w[0], row(ffn_conv_b[0]),
                w_ffn_down[0], row(ln_ffn_g[0]), row(ln_ffn_b[0]))
```

```python
import functools
import math

import jax
import jax.numpy as jnp
import numpy as np
from jax import lax
from jax.experimental import pallas as pl
from jax.experimental.pallas import tpu as pltpu

D_MODEL = 1024
HEAD_DIM = 64
N_Q_HEADS = 8
N_KV_HEADS = 2
GROUP = N_Q_HEADS // N_KV_HEADS
ATT_WIDTH = N_Q_HEADS * HEAD_DIM
KV_WIDTH = N_KV_HEADS * HEAD_DIM
WINDOW = 128
BLOCK = 128
CONV_WIDTH = D_MODEL // 2
D_FF = 2816
N_BUCKETS = 32
MAX_DISTANCE = 128
LN_EPS = 1e-5
DEPTH = 1
DEEPNORM_ALPHA = (2 * DEPTH) ** 0.25
MASK_VALUE = -1e30

Q_OFF = 0
K_OFF = ATT_WIDTH
V_OFF = K_OFF + KV_WIDTH
CB_OFF = V_OFF + KV_WIDTH
CC_OFF = CB_OFF + CONV_WIDTH
CX_OFF = CC_OFF + CONV_WIDTH
GATE_OFF = CX_OFF + CONV_WIDTH
IN_PROJ_WIDTH = GATE_OFF + 2 * D_MODEL

TILE = 512
QB_PER_TILE = TILE // BLOCK
CONV_HALO = 16
FF_CHUNK = 256
N_FF_CHUNKS = D_FF // FF_CHUNK
FF_ROWS = 32
PROJ_COLS = 256
MIX_ROWS = 32
STAGE_ROWS = 256
STAGE_ROWS_WIDE = 128
LN_ROWS = 32
SM_ROWS = 32
LANES = 128
ROW_PITCH = 2
VMEM_LIMIT_BYTES = 60 * 1024 * 1024

F32 = jnp.float32
BF16 = jnp.bfloat16


def _layer_norm(x, g, b):
    mu = jnp.mean(x, axis=-1, keepdims=True)
    xc = x - mu
    var = jnp.mean(xc * xc, axis=-1, keepdims=True)
    return xc * lax.rsqrt(var + LN_EPS) * g + b


def _sigmoid(x):
    return 0.5 * jnp.tanh(0.5 * x) + 0.5


def _zero_after(v):
    bits = lax.bitcast_convert_type(v, jnp.uint32)
    half_word = jnp.uint32(16)
    return lax.bitcast_convert_type(
        lax.shift_right_logical(lax.shift_right_logical(bits, half_word), half_word), F32)


def _dot(a, b):
    return jnp.dot(a, b, preferred_element_type=F32)


def _dot_nt(a, b):
    return lax.dot_general(a, b, (((1,), (1,)), ((), ())), preferred_element_type=F32)


def _bias_by_rel(rel_bias):
    rel = jnp.arange(4 * BLOCK) - (2 * BLOCK - 1)
    half = N_BUCKETS // 2
    max_exact = half // 2
    offset = jnp.where(rel > 0, half, 0)
    n = jnp.abs(rel)
    nf = jnp.maximum(n, 1).astype(jnp.float32)
    large = max_exact + (jnp.log(nf / max_exact) / math.log(MAX_DISTANCE / max_exact)
                         * (half - max_exact)).astype(jnp.int32)
    large = jnp.minimum(large, half - 1)
    bucket = offset + jnp.where(n < max_exact, n, large)
    return jnp.transpose(rel_bias.astype(F32)[bucket], (1, 0))


def _bias_table_kernel(bias_ref, out_ref):
    shape = (BLOCK, 3 * BLOCK)
    col = lax.broadcasted_iota(jnp.int32, shape, 1)
    rel = col - BLOCK - lax.broadcasted_iota(jnp.int32, shape, 0)
    in_window = jnp.abs(rel) <= WINDOW
    for h in range(N_Q_HEADS):
        by_rel = jnp.broadcast_to(bias_ref[h:h + 1, :], (BLOCK, 4 * BLOCK))
        band = pltpu.roll(by_rel, 3 * BLOCK + 1, axis=1, stride=1, stride_axis=0)[:, :3 * BLOCK]
        t = jnp.where(in_window, band, MASK_VALUE)
        out_ref[0, h] = t
        out_ref[1, h] = jnp.where(col < BLOCK, MASK_VALUE, t)
        out_ref[2, h] = jnp.where(col >= 2 * BLOCK, MASK_VALUE, t)


def _bias_table(rel_bias):
    return pl.pallas_call(
        _bias_table_kernel,
        out_shape=jax.ShapeDtypeStruct((3, N_Q_HEADS, BLOCK, 3 * BLOCK), F32),
        in_specs=[pl.BlockSpec(memory_space=pltpu.VMEM)],
        out_specs=pl.BlockSpec(memory_space=pltpu.VMEM),
        name="bias_table",
    )(_bias_by_rel(rel_bias))


def _stage_weight(src_hbm, dst, stage, sem):
    chunk = stage.shape[1]
    n_chunks = src_hbm.shape[0] // chunk
    assert n_chunks * chunk == src_hbm.shape[0] and stage.shape[2] == src_hbm.shape[1]

    def copy(k):
        return pltpu.make_async_copy(src_hbm.at[pl.ds(k * chunk, chunk)], stage.at[k % 2], sem.at[k % 2])

    copy(0).start()
    for k in range(n_chunks):
        if k + 1 < n_chunks:
            copy(k + 1).start()
        copy(k).wait()
        dst[k * chunk:(k + 1) * chunk] = stage[k % 2].astype(dst.dtype)


def _mixer_kernel(xp_ref, xc_ref, xn_ref, lng_ref, lnb_ref, win_hbm, bg_ref, sink_ref, bias_ref,
                  cw_ref, cbias_ref, wa_hbm, wc_hbm, wo_hbm, g2_ref, b2_ref, out_ref,
                  hext, hres, q_scr, kvar, vvar, att_scr, u_scr, conv_scr, g_scr, merged_scr,
                  win_ref, wa_ref, wc_ref, wo_ref, stage_in, stage_sq, dma_sem):
    i = pl.program_id(1)
    last_i = pl.num_programs(1) - 1

    @pl.when((pl.program_id(0) == 0) & (i == 0))
    def _():
        _stage_weight(win_hbm, win_ref, stage_in, dma_sem)
        _stage_weight(wa_hbm, wa_ref, stage_sq, dma_sem)
        _stage_weight(wc_hbm, wc_ref, stage_sq, dma_sem)
        _stage_weight(wo_hbm, wo_ref, stage_sq, dma_sem)

    lng = lng_ref[...]
    lnb = lnb_ref[...]
    ext = TILE + 2 * BLOCK

    def norm_rows(lo, hi):
        for r in range(lo, hi, LN_ROWS):
            if r < BLOCK:
                x = xp_ref[0, r:r + LN_ROWS]
            elif r < BLOCK + TILE:
                x = xc_ref[0, r - BLOCK:r - BLOCK + LN_ROWS]
            else:
                x = xn_ref[0, r - BLOCK - TILE:r - BLOCK - TILE + LN_ROWS]
            y = _layer_norm(x, lng, lnb)
            if BLOCK <= r < BLOCK + TILE:
                hres[r - BLOCK:r - BLOCK + LN_ROWS] = y
            hext[r:r + LN_ROWS] = y.astype(BF16)

    def project_qkv(lo, hi):
        rows = slice(lo, hi)
        kv = _dot(hext[rows], win_ref[:, K_OFF:K_OFF + 2 * KV_WIDTH])
        low = lax.broadcasted_iota(jnp.int32, (hi - lo, KV_WIDTH), 1) < HEAD_DIM
        for src, dst in ((kv[:, :KV_WIDTH], kvar), (kv[:, KV_WIDTH:], vvar)):
            rolled = pltpu.roll(src, HEAD_DIM, axis=1)
            dst[0, rows] = jnp.where(low, src, 0.0).astype(BF16)
            dst[1, rows] = jnp.where(low, 0.0, rolled).astype(BF16)
            dst[2, rows] = jnp.where(low, rolled, 0.0).astype(BF16)
            dst[3, rows] = jnp.where(low, 0.0, src).astype(BF16)
        q_lo, q_hi = max(lo, BLOCK), min(hi, BLOCK + TILE)
        q_scr[q_lo - BLOCK:q_hi - BLOCK] = (
            _dot(hext[q_lo:q_hi], win_ref[:, Q_OFF:Q_OFF + ATT_WIDTH]) * (HEAD_DIM ** -0.5)).astype(BF16)

    norm_rows(0, ext // 2)
    project_qkv(0, ext // 2)
    norm_rows(ext // 2, ext)
    project_qkv(ext // 2, ext)
    hcb = hext[BLOCK:BLOCK + TILE]

    lane_low = lax.broadcasted_iota(jnp.int32, (BLOCK, 2 * HEAD_DIM), 1) < HEAD_DIM

    def att_scores(qb, pair):
        rows = slice(qb * BLOCK, qb * BLOCK + 3 * BLOCK)
        kvh = pair // (GROUP // 2)
        q2 = q_scr[qb * BLOCK:(qb + 1) * BLOCK, pair * 2 * HEAD_DIM:(pair + 1) * 2 * HEAD_DIM]
        return _dot_nt(q2, jnp.concatenate([kvar[2 * kvh, rows], kvar[2 * kvh + 1, rows]], axis=0))

    def att_softmax(qb, pair, scores):
        if qb == 0:
            edge = jnp.where(i == 0, 1, 0)
        elif qb == QB_PER_TILE - 1:
            edge = jnp.where(i == last_i, 2, 0)
        else:
            edge = 0
        probs, inv = [], []
        for sub in range(2):
            h = 2 * pair + sub
            sink = sink_ref[h]
            p_blocks, inv_blocks = [], []
            for r in range(0, BLOCK, SM_ROWS):
                logits = (scores[r:r + SM_ROWS, sub * 3 * BLOCK:(sub + 1) * 3 * BLOCK]
                          + bias_ref[edge, h, r:r + SM_ROWS, :])
                m = jnp.maximum(jnp.max(logits, axis=-1, keepdims=True), sink)
                p = jnp.exp(logits - m)
                denom = jnp.sum(p, axis=-1, keepdims=True) + jnp.exp(sink - m)
                p_blocks.append(p.astype(BF16))
                inv_blocks.append(1.0 / denom)
            probs.append(jnp.concatenate(p_blocks, axis=0))
            inv.append(jnp.concatenate(inv_blocks, axis=0))
        return jnp.concatenate(probs, axis=1), jnp.where(lane_low, inv[0], inv[1])

    def att_values(qb, pair, probs, scale):
        rows = slice(qb * BLOCK, qb * BLOCK + 3 * BLOCK)
        kvh = pair // (GROUP // 2)
        o = _dot(probs, jnp.concatenate([vvar[2 * kvh, rows], vvar[2 * kvh + 1, rows]], axis=0))
        att_scr[qb * BLOCK:(qb + 1) * BLOCK, pair * 2 * HEAD_DIM:(pair + 1) * 2 * HEAD_DIM] = (
            o * scale).astype(BF16)

    lo = BLOCK - CONV_HALO
    hi = BLOCK + TILE + CONV_HALO
    slabs = PROJ_COLS // LANES

    def conv_input(j):
        cc = _dot(hext[lo:hi], win_ref[:, CC_OFF + j * PROJ_COLS:CC_OFF + (j + 1) * PROJ_COLS])
        cx = _dot(hext[lo:hi], win_ref[:, CX_OFF + j * PROJ_COLS:CX_OFF + (j + 1) * PROJ_COLS])
        row = lax.broadcasted_iota(jnp.int32, (TILE + 2 * CONV_HALO, 1), 0)
        inside = ((row >= CONV_HALO) | (i > 0)) & ((row < CONV_HALO + TILE) | (i < last_i))
        u = jnp.where(inside, cc * cx, 0.0)
        for s in range(slabs):
            u_scr[j * slabs + s, pl.ds(0, TILE + 2 * CONV_HALO, stride=ROW_PITCH), :] = (
                u[:, s * LANES:(s + 1) * LANES])

    def conv_branch(j):
        cb = _dot(hcb, win_ref[:, CB_OFF + j * PROJ_COLS:CB_OFF + (j + 1) * PROJ_COLS])
        taps = []
        for s in range(slabs):
            lanes = slice(j * PROJ_COLS + s * LANES, j * PROJ_COLS + (s + 1) * LANES)
            w = cw_ref[:, lanes]
            taps.append([jnp.broadcast_to(w[k:k + 1], (MIX_ROWS, LANES)) for k in range(3)]
                        + [jnp.broadcast_to(cbias_ref[:, lanes], (MIX_ROWS, LANES))])
        for rb in range(TILE // MIX_ROWS):
            r0 = CONV_HALO + rb * MIX_ROWS
            parts = []
            for s in range(slabs):
                ld = lambda r: u_scr[j * slabs + s, pl.ds(ROW_PITCH * r, MIX_ROWS, stride=ROW_PITCH), :]
                t = taps[s]
                dw = ld(r0 - 1) * t[0] + ld(r0) * t[1] + ld(r0 + 1) * t[2] + t[3]
                parts.append(cb[rb * MIX_ROWS:(rb + 1) * MIX_ROWS, s * LANES:(s + 1) * LANES] * dw)
            conv_scr[j, rb * MIX_ROWS:(rb + 1) * MIX_ROWS] = jnp.concatenate(parts, axis=1).astype(BF16)

    def gate(n):
        cols = slice(n * PROJ_COLS, (n + 1) * PROJ_COLS)
        pre = _dot(hcb, win_ref[:, GATE_OFF + n * PROJ_COLS:GATE_OFF + (n + 1) * PROJ_COLS])
        bias = bg_ref[:, cols]
        for rb in range(TILE // BLOCK):
            rows = slice(rb * BLOCK, (rb + 1) * BLOCK)
            g_scr[n, rows] = _sigmoid(pre[rows] + bias)

    n_conv = CONV_WIDTH // PROJ_COLS
    n_gate = 2 * D_MODEL // PROJ_COLS
    proj_units = ([functools.partial(conv_input, j) for j in range(n_conv)]
                  + [functools.partial(conv_branch, j) for j in range(n_conv)]
                  + [functools.partial(gate, n) for n in range(n_gate)])
    n_out = D_MODEL // PROJ_COLS
    half_rows = TILE // 2

    def merge(half, n):
        rows = slice(half * half_rows, (half + 1) * half_rows)
        cols = slice(n * PROJ_COLS, (n + 1) * PROJ_COLS)
        conv = jnp.concatenate([conv_scr[j, rows] for j in range(n_conv)], axis=1)
        ya = _dot(att_scr[rows], wa_ref[:, cols])
        yc = _dot(conv, wc_ref[:, cols])
        merged_scr[n, rows] = (g_scr[n, rows] * ya + g_scr[n_out + n, rows] * yc).astype(BF16)

    def out_proj(half):
        rows = slice(half * half_rows, (half + 1) * half_rows)
        return _dot(jnp.concatenate([merged_scr[n, rows] for n in range(n_out)], axis=1), wo_ref[...])

    def finish(half, mix, blocks):
        for blk in blocks:
            r = half * half_rows + blk * LN_ROWS
            out_ref[0, r:r + LN_ROWS] = _layer_norm(
                DEEPNORM_ALPHA * hres[r:r + LN_ROWS] + mix[blk * LN_ROWS:(blk + 1) * LN_ROWS],
                g2_ref[...], b2_ref[...])

    att_units = [(qb, pair) for qb in range(QB_PER_TILE) for pair in range(N_Q_HEADS // 2)]
    filler = proj_units + [functools.partial(merge, 0, n) for n in range(n_out)]
    assert len(filler) == len(att_units) and len(proj_units) >= len(att_units) // 2
    scores = att_scores(*att_units[0])
    for k, unit in enumerate(att_units):
        ahead = att_scores(*att_units[k + 1]) if k + 1 < len(att_units) else None
        probs, scale = att_softmax(*unit, scores)
        filler[k]()
        att_values(*unit, probs, scale)
        scores = ahead
    ln_blocks = half_rows // LN_ROWS
    mix = out_proj(0)
    for n in range(n_out):
        finish(0, mix, range(n * ln_blocks // n_out, (n + 1) * ln_blocks // n_out))
        merge(1, n)
    finish(1, out_proj(1), range(ln_blocks))


def _const_spec(shape):
    return pl.BlockSpec(shape, lambda *_: (0,) * len(shape), pipeline_mode=pl.Buffered(1))


def _mixer(x, ln_g, ln_b, w_in, b_gates, sink, bias_tbl, conv_w, conv_b, w_a, w_c, w_o, g2, b2):
    bsz, seq, _ = x.shape
    n_tiles = seq // TILE
    blocks_per_tile = TILE // BLOCK
    n_blocks = seq // BLOCK
    in_specs = [
        pl.BlockSpec((1, BLOCK, D_MODEL), lambda b, i: (b, jnp.maximum(i * blocks_per_tile - 1, 0), 0)),
        pl.BlockSpec((1, TILE, D_MODEL), lambda b, i: (b, i, 0)),
        pl.BlockSpec((1, BLOCK, D_MODEL),
                     lambda b, i: (b, jnp.minimum((i + 1) * blocks_per_tile, n_blocks - 1), 0)),
        _const_spec((1, D_MODEL)), _const_spec((1, D_MODEL)),
        pl.BlockSpec(memory_space=pl.ANY),
        _const_spec((1, 2 * D_MODEL)),
        pl.BlockSpec(memory_space=pltpu.SMEM),
        _const_spec((3, N_Q_HEADS, BLOCK, 3 * BLOCK)),
        _const_spec((3, CONV_WIDTH)), _const_spec((1, CONV_WIDTH)),
        pl.BlockSpec(memory_space=pl.ANY), pl.BlockSpec(memory_space=pl.ANY),
        pl.BlockSpec(memory_space=pl.ANY),
        _const_spec((1, D_MODEL)), _const_spec((1, D_MODEL)),
    ]
    ext = TILE + 2 * BLOCK
    return pl.pallas_call(
        _mixer_kernel,
        out_shape=jax.ShapeDtypeStruct((bsz, seq, D_MODEL), F32),
        grid=(bsz, n_tiles),
        in_specs=in_specs,
        out_specs=pl.BlockSpec((1, TILE, D_MODEL), lambda b, i: (b, i, 0)),
        scratch_shapes=[
            pltpu.VMEM((ext, D_MODEL), BF16),
            pltpu.VMEM((TILE, D_MODEL), F32),
            pltpu.VMEM((TILE, ATT_WIDTH), BF16),
            pltpu.VMEM((4, ext, KV_WIDTH), BF16),
            pltpu.VMEM((4, ext, KV_WIDTH), BF16),
            pltpu.VMEM((TILE, ATT_WIDTH), BF16),
            pltpu.VMEM((CONV_WIDTH // LANES, ROW_PITCH * (TILE + 2 * CONV_HALO), LANES), F32),
            pltpu.VMEM((CONV_WIDTH // PROJ_COLS, TILE, PROJ_COLS), BF16),
            pltpu.VMEM((2 * D_MODEL // PROJ_COLS, TILE, PROJ_COLS), F32),
            pltpu.VMEM((D_MODEL // PROJ_COLS, TILE, PROJ_COLS), BF16),
            pltpu.VMEM((D_MODEL, IN_PROJ_WIDTH), BF16),
            pltpu.VMEM((ATT_WIDTH, D_MODEL), BF16),
            pltpu.VMEM((CONV_WIDTH, D_MODEL), BF16),
            pltpu.VMEM((D_MODEL, D_MODEL), BF16),
            pltpu.VMEM((2, STAGE_ROWS_WIDE, IN_PROJ_WIDTH), F32),
            pltpu.VMEM((2, STAGE_ROWS, D_MODEL), F32),
            pltpu.SemaphoreType.DMA((2,)),
        ],
        compiler_params=pltpu.CompilerParams(
            dimension_semantics=("arbitrary", "arbitrary"), vmem_limit_bytes=VMEM_LIMIT_BYTES),
        name="mixer",
    )(x, x, x, ln_g, ln_b, w_in, b_gates, sink, bias_tbl, conv_w, conv_b, w_a, w_c, w_o, g2, b2)


def _ffn_kernel(hp_ref, hc_ref, hn_ref, wup_hbm, cw_ref, cb_ref, wdn_hbm, g_ref, b_ref, out_ref,
                hext, a_scr, u_scr, act_scr, acc, pre_norm, wup_ref, wdn_ref, stage_up, stage_dn, dma_sem,
                *, tiles_per_seq):
    s = pl.program_id(0)
    n_tiles = pl.num_programs(0) - 1
    i = jnp.minimum(s, n_tiles - 1) % tiles_per_seq
    last_i = tiles_per_seq - 1

    @pl.when(s == 0)
    def _():
        _stage_weight(wup_hbm, wup_ref, stage_up, dma_sem)
        _stage_weight(wdn_hbm, wdn_ref, stage_dn, dma_sem)
        pre_norm[...] = jnp.zeros(pre_norm.shape, F32)

    def make_finisher():
        return _StagedLayerNorm(pre_norm, out_ref, g_ref, b_ref, TILE // LN_ROWS)

    @pl.when(s == n_tiles)
    def _():
        finisher = make_finisher()
        while not finisher.done:
            finisher.step()

    @pl.when(s < n_tiles)
    def _():
        _ffn_tile(hp_ref, hc_ref, hn_ref, cw_ref, cb_ref, hext, a_scr, u_scr, act_scr, acc, pre_norm,
                  wup_ref, wdn_ref, i, last_i, make_finisher())


class _StagedLayerNorm:
    def __init__(self, src, dst, g_ref, b_ref, n_blocks):
        self.src, self.dst, self.g_ref, self.b_ref, self.n_blocks = src, dst, g_ref, b_ref, n_blocks
        self.j = 0
        self.mean, self.scale = {}, {}

    @property
    def done(self):
        return self.j >= self.n_blocks + 2

    def _rows(self, blk):
        return slice(blk * LN_ROWS, (blk + 1) * LN_ROWS)

    def step(self):
        j = self.j
        self.j += 1
        stored = None
        if 0 <= j - 2 < self.n_blocks:
            rows = self._rows(j - 2)
            xc = self.src[rows] - self.mean.pop(j - 2)
            y = xc * self.scale.pop(j - 2) * self.g_ref[...] + self.b_ref[...]
            self.dst[0, rows] = y
            stored = y[:, :LANES]
        self._emit_stats(j)
        return stored

    def _emit_stats(self, j):
        if 0 <= j - 1 < self.n_blocks:
            xc = self.src[self._rows(j - 1)] - self.mean[j - 1]
            self.scale[j - 1] = lax.rsqrt(jnp.mean(xc * xc, axis=-1, keepdims=True) + LN_EPS)
        if j < self.n_blocks:
            self.mean[j] = jnp.mean(self.src[self._rows(j)], axis=-1, keepdims=True)


def _ffn_tile(hp_ref, hc_ref, hn_ref, cw_ref, cb_ref, hext, a_scr, u_scr, act_scr, acc, pre_norm,
              wup_ref, wdn_ref, i, last_i, finisher):
    hext[0:CONV_HALO] = jnp.where(i > 0, hp_ref[0], 0.0).astype(BF16)
    hext[CONV_HALO:CONV_HALO + TILE] = hc_ref[0].astype(BF16)
    hext[CONV_HALO + TILE:] = jnp.where(i < last_i, hn_ref[0], 0.0).astype(BF16)

    tile_half = TILE // 2
    ext_split = (0, tile_half + 2 * CONV_HALO, TILE + 2 * CONV_HALO)

    def up_proj(c, half):
        col = c * FF_CHUNK
        start, stop = ext_split[half], ext_split[half + 1]
        dst = pl.ds(ROW_PITCH * start, stop - start, stride=ROW_PITCH)
        for scr, off in ((a_scr, col), (u_scr, D_FF + col)):
            res = _dot(hext[start:stop], wup_ref[:, off:off + FF_CHUNK])
            for slab in range(FF_CHUNK // LANES):
                scr[c % 2, slab, dst, :] = res[:, slab * LANES:(slab + 1) * LANES]

    def down_proj(c, half):
        rows = slice(half * tile_half, (half + 1) * tile_half)
        part = _dot(act_scr[c % 2, rows], wdn_ref[c * FF_CHUNK:(c + 1) * FF_CHUNK, :])
        if c == 0:
            acc[rows] = DEEPNORM_ALPHA * hc_ref[0, rows] + part
        elif c < N_FF_CHUNKS - 1:
            acc[rows] += part
        else:
            pre_norm[rows] = acc[rows] + part

    def conv_taps(col):
        out = []
        for slab in range(FF_CHUNK // LANES):
            lanes = slice(col + slab * LANES, col + (slab + 1) * LANES)
            w = cw_ref[:, lanes]
            out.append([jnp.broadcast_to(w[k:k + 1], (FF_ROWS, LANES)) for k in range(3)]
                       + [jnp.broadcast_to(cb_ref[:, lanes], (FF_ROWS, LANES))])
        return out

    def conv3(scr, slot, slab, r0, taps):
        rows = lambda r: scr[slot, slab, pl.ds(ROW_PITCH * r, FF_ROWS, stride=ROW_PITCH), :]
        return rows(r0 - 1) * taps[0] + rows(r0) * taps[1] + rows(r0 + 1) * taps[2] + taps[3]

    def gate_rows(c, rb, a_taps, u_taps, after=None):
        r0 = CONV_HALO + rb * FF_ROWS
        parts = []
        for slab in range(FF_CHUNK // LANES):
            a = conv3(a_scr, c % 2, slab, r0, a_taps[slab])
            u = conv3(u_scr, c % 2, slab, r0, u_taps[slab])
            parts.append(a * _sigmoid(a) * u)
        if after is not None:
            parts[0] = parts[0] + _zero_after(after)
        act_scr[c % 2, rb * FF_ROWS:(rb + 1) * FF_ROWS] = jnp.concatenate(parts, axis=1).astype(BF16)

    units = [(c, half) for c in range(N_FF_CHUNKS) for half in range(2)]
    blocks_per_half = tile_half // FF_ROWS
    up_proj(*units[0])
    taps = {}
    normed = None
    for k, (c, half) in enumerate(units):
        if half == 0:
            taps = {"a": conv_taps(c * FF_CHUNK), "u": conv_taps(D_FF + c * FF_CHUNK)}
        mxu_units = []
        if k + 1 < len(units):
            mxu_units.append(functools.partial(up_proj, *units[k + 1]))
        if k >= 1:
            mxu_units.append(functools.partial(down_proj, *units[k - 1]))
        per_unit = blocks_per_half // len(mxu_units)
        for m, unit in enumerate(mxu_units):
            unit()
            for rb in range(m * per_unit, (m + 1) * per_unit):
                gate_rows(c, half * blocks_per_half + rb, taps["a"], taps["u"], after=normed)
                normed = None
            if m == 0 and not finisher.done:
                normed = finisher.step()
    down_proj(*units[-1])
    assert finisher.done and normed is None


def _ffn(h, w_up, conv_w, conv_b, w_down, g, b):
    bsz, seq, _ = h.shape
    tiles_per_seq = seq // TILE
    n_tiles = bsz * tiles_per_seq
    halo_per_tile = TILE // CONV_HALO
    n_halo_blocks = seq // CONV_HALO

    def in_tile(s):
        t = jnp.minimum(s, n_tiles - 1)
        return t // tiles_per_seq, t % tiles_per_seq

    def center(s):
        b, i = in_tile(s)
        return b, i, 0

    def halo_before(s):
        b, i = in_tile(s)
        return b, jnp.maximum(i * halo_per_tile - 1, 0), 0

    def halo_after(s):
        b, i = in_tile(s)
        return b, jnp.minimum((i + 1) * halo_per_tile, n_halo_blocks - 1), 0

    def out_tile(s):
        t = jnp.maximum(s - 1, 0)
        return t // tiles_per_seq, t % tiles_per_seq, 0

    in_specs = [
        pl.BlockSpec((1, CONV_HALO, D_MODEL), halo_before),
        pl.BlockSpec((1, TILE, D_MODEL), center),
        pl.BlockSpec((1, CONV_HALO, D_MODEL), halo_after),
        pl.BlockSpec(memory_space=pl.ANY),
        _const_spec((3, 2 * D_FF)), _const_spec((1, 2 * D_FF)),
        pl.BlockSpec(memory_space=pl.ANY),
        _const_spec((1, D_MODEL)), _const_spec((1, D_MODEL)),
    ]
    ext = TILE + 2 * CONV_HALO
    return pl.pallas_call(
        functools.partial(_ffn_kernel, tiles_per_seq=tiles_per_seq),
        out_shape=jax.ShapeDtypeStruct((bsz, seq, D_MODEL), F32),
        grid=(n_tiles + 1,),
        in_specs=in_specs,
        out_specs=pl.BlockSpec((1, TILE, D_MODEL), out_tile),
        scratch_shapes=[
            pltpu.VMEM((ext, D_MODEL), BF16),
            pltpu.VMEM((2, FF_CHUNK // LANES, ROW_PITCH * ext, LANES), F32),
            pltpu.VMEM((2, FF_CHUNK // LANES, ROW_PITCH * ext, LANES), F32),
            pltpu.VMEM((2, TILE, FF_CHUNK), BF16),
            pltpu.VMEM((TILE, D_MODEL), F32),
            pltpu.VMEM((TILE, D_MODEL), F32),
            pltpu.VMEM((D_MODEL, 2 * D_FF), BF16),
            pltpu.VMEM((D_FF, D_MODEL), BF16),
            pltpu.VMEM((2, STAGE_ROWS_WIDE // 2, 2 * D_FF), F32),
            pltpu.VMEM((2, STAGE_ROWS, D_MODEL), F32),
            pltpu.SemaphoreType.DMA((2,)),
        ],
        compiler_params=pltpu.CompilerParams(
            dimension_semantics=("arbitrary",), vmem_limit_bytes=VMEM_LIMIT_BYTES),
        name="ffn",
    )(h, h, h, w_up, conv_w, conv_b, w_down, g, b)


def kernel(x, ln_in_g, ln_in_b, w_in, b_gates, attn_sink, rel_bias, conv_w, conv_b, w_att_branch,
           w_conv_branch, w_o, ln_mix_g, ln_mix_b, w_ffn_up, ffn_conv_w, ffn_conv_b, w_ffn_down,
           ln_ffn_g, ln_ffn_b):
    assert w_in.shape[0] == DEPTH == 1
    row = lambda v: v.reshape(1, -1)
    bias_tbl = _bias_table(rel_bias)
    h = _mixer(x, row(ln_in_g), row(ln_in_b), w_in[0], row(b_gates[0]), attn_sink[0],
               bias_tbl, conv_w[0], row(conv_b[0]), w_att_branch[0],
               w_conv_branch[0], w_o[0], row(ln_mix_g[0]), row(ln_mix_b[0]))
    return _ffn(h, w_ffn_up[0], ffn_conv_w[0], row(ffn_conv_b[0]),
                w_ffn_down[0], row(ln_ffn_g[0]), row(ln_ffn_b[0]))
```

```python
import functools
import math

import jax
import jax.numpy as jnp
import numpy as np
from jax import lax
from jax.experimental import pallas as pl
from jax.experimental.pallas import tpu as pltpu

D_MODEL = 1024
HEAD_DIM = 64
N_Q_HEADS = 8
N_KV_HEADS = 2
GROUP = N_Q_HEADS // N_KV_HEADS
ATT_WIDTH = N_Q_HEADS * HEAD_DIM
KV_WIDTH = N_KV_HEADS * HEAD_DIM
WINDOW = 128
BLOCK = 128
CONV_WIDTH = D_MODEL // 2
D_FF = 2816
N_BUCKETS = 32
MAX_DISTANCE = 128
LN_EPS = 1e-5
DEPTH = 1
DEEPNORM_ALPHA = (2 * DEPTH) ** 0.25
MASK_VALUE = -1e30

Q_OFF = 0
K_OFF = ATT_WIDTH
V_OFF = K_OFF + KV_WIDTH
CB_OFF = V_OFF + KV_WIDTH
CC_OFF = CB_OFF + CONV_WIDTH
CX_OFF = CC_OFF + CONV_WIDTH
GATE_OFF = CX_OFF + CONV_WIDTH
IN_PROJ_WIDTH = GATE_OFF + 2 * D_MODEL

TILE = 512
QB_PER_TILE = TILE // BLOCK
CONV_HALO = 16
FF_CHUNK = 256
N_FF_CHUNKS = D_FF // FF_CHUNK
FF_ROWS = 32
PROJ_COLS = 256
MIX_ROWS = 32
STAGE_ROWS = 256
STAGE_ROWS_WIDE = 128
LN_ROWS = 32
SM_ROWS = 32
LANES = 128
BF16_SUBLANES = 16
ROW_PITCH = 2
VMEM_LIMIT_BYTES = 60 * 1024 * 1024

F32 = jnp.float32
BF16 = jnp.bfloat16


def _layer_norm(x, g, b):
    mu = jnp.mean(x, axis=-1, keepdims=True)
    xc = x - mu
    var = jnp.mean(xc * xc, axis=-1, keepdims=True)
    return xc * lax.rsqrt(var + LN_EPS) * g + b


def _sigmoid(x):
    return 0.5 * jnp.tanh(0.5 * x) + 0.5


def _dot(a, b):
    return jnp.dot(a, b, preferred_element_type=F32)


def _dot_nt(a, b):
    return lax.dot_general(a, b, (((1,), (1,)), ((), ())), preferred_element_type=F32)


def _bias_by_rel(rel_bias):
    rel = jnp.arange(4 * BLOCK) - (2 * BLOCK - 1)
    half = N_BUCKETS // 2
    max_exact = half // 2
    offset = jnp.where(rel > 0, half, 0)
    n = jnp.abs(rel)
    nf = jnp.maximum(n, 1).astype(jnp.float32)
    large = max_exact + (jnp.log(nf / max_exact) / math.log(MAX_DISTANCE / max_exact)
                         * (half - max_exact)).astype(jnp.int32)
    large = jnp.minimum(large, half - 1)
    bucket = offset + jnp.where(n < max_exact, n, large)
    return jnp.transpose(rel_bias.astype(F32)[bucket], (1, 0))


def _bias_table_kernel(bias_ref, out_ref):
    shape = (BLOCK, 3 * BLOCK)
    col = lax.broadcasted_iota(jnp.int32, shape, 1)
    rel = col - BLOCK - lax.broadcasted_iota(jnp.int32, shape, 0)
    in_window = jnp.abs(rel) <= WINDOW
    for h in range(N_Q_HEADS):
        by_rel = jnp.broadcast_to(bias_ref[h:h + 1, :], (BLOCK, 4 * BLOCK))
        band = pltpu.roll(by_rel, 3 * BLOCK + 1, axis=1, stride=1, stride_axis=0)[:, :3 * BLOCK]
        t = jnp.where(in_window, band, MASK_VALUE)
        out_ref[0, h] = t
        out_ref[1, h] = jnp.where(col < BLOCK, MASK_VALUE, t)
        out_ref[2, h] = jnp.where(col >= 2 * BLOCK, MASK_VALUE, t)


def _bias_table(rel_bias):
    return pl.pallas_call(
        _bias_table_kernel,
        out_shape=jax.ShapeDtypeStruct((3, N_Q_HEADS, BLOCK, 3 * BLOCK), F32),
        in_specs=[pl.BlockSpec(memory_space=pltpu.VMEM)],
        out_specs=pl.BlockSpec(memory_space=pltpu.VMEM),
        name="bias_table",
    )(_bias_by_rel(rel_bias))


def _stage_weight(src_hbm, dst, stage, sem):
    chunk = stage.shape[1]
    n_chunks = src_hbm.shape[0] // chunk
    assert n_chunks * chunk == src_hbm.shape[0] and stage.shape[2] == src_hbm.shape[1]

    def copy(k):
        return pltpu.make_async_copy(src_hbm.at[pl.ds(k * chunk, chunk)], stage.at[k % 2], sem.at[k % 2])

    copy(0).start()
    for k in range(n_chunks):
        if k + 1 < n_chunks:
            copy(k + 1).start()
        copy(k).wait()
        dst[k * chunk:(k + 1) * chunk] = stage[k % 2].astype(dst.dtype)


def _mixer_kernel(xp_ref, xc_ref, xn_ref, lng_ref, lnb_ref, win_hbm, bg_ref, sink_ref, bias_ref,
                  cw_ref, cbias_ref, wa_hbm, wc_hbm, wo_hbm, g2_ref, b2_ref, wup_in, wdn_in,
                  out_ref, wup_out, wdn_out,
                  hext, hres, q_scr, kvar, vvar, att_scr, u_scr, conv_scr, g_scr, merged_scr,
                  win_ref, wa_ref, wc_ref, wo_ref, stage_in, stage_sq, dma_sem):
    i = pl.program_id(1)
    last_i = pl.num_programs(1) - 1

    @pl.when((pl.program_id(0) == 0) & (i == 0))
    def _():
        _stage_weight(win_hbm, win_ref, stage_in, dma_sem)
        _stage_weight(wa_hbm, wa_ref, stage_sq, dma_sem)
        _stage_weight(wc_hbm, wc_ref, stage_sq, dma_sem)
        _stage_weight(wo_hbm, wo_ref, stage_sq, dma_sem)

    wup_out[...] = wup_in[...].astype(BF16)
    wdn_out[...] = wdn_in[...].astype(BF16)

    lng = lng_ref[...]
    lnb = lnb_ref[...]
    ext = TILE + 2 * BLOCK

    def norm_rows(lo, hi):
        for r in range(lo, hi, LN_ROWS):
            if r < BLOCK:
                x = xp_ref[0, r:r + LN_ROWS]
            elif r < BLOCK + TILE:
                x = xc_ref[0, r - BLOCK:r - BLOCK + LN_ROWS]
            else:
                x = xn_ref[0, r - BLOCK - TILE:r - BLOCK - TILE + LN_ROWS]
            y = _layer_norm(x, lng, lnb)
            if BLOCK <= r < BLOCK + TILE:
                hres[r - BLOCK:r - BLOCK + LN_ROWS] = y
            hext[r:r + LN_ROWS] = y.astype(BF16)

    def project_qkv(lo, hi):
        rows = slice(lo, hi)
        kv = _dot(hext[rows], win_ref[:, K_OFF:K_OFF + 2 * KV_WIDTH])
        low = lax.broadcasted_iota(jnp.int32, (hi - lo, KV_WIDTH), 1) < HEAD_DIM
        for src, dst in ((kv[:, :KV_WIDTH], kvar), (kv[:, KV_WIDTH:], vvar)):
            rolled = pltpu.roll(src, HEAD_DIM, axis=1)
            dst[0, rows] = jnp.where(low, src, 0.0).astype(BF16)
            dst[1, rows] = jnp.where(low, 0.0, rolled).astype(BF16)
            dst[2, rows] = jnp.where(low, rolled, 0.0).astype(BF16)
            dst[3, rows] = jnp.where(low, 0.0, src).astype(BF16)
        q_lo, q_hi = max(lo, BLOCK), min(hi, BLOCK + TILE)
        q_scr[q_lo - BLOCK:q_hi - BLOCK] = (
            _dot(hext[q_lo:q_hi], win_ref[:, Q_OFF:Q_OFF + ATT_WIDTH]) * (HEAD_DIM ** -0.5)).astype(BF16)

    norm_rows(0, ext // 2)
    project_qkv(0, ext // 2)
    norm_rows(ext // 2, ext)
    project_qkv(ext // 2, ext)
    hcb = hext[BLOCK:BLOCK + TILE]

    lane_low = lax.broadcasted_iota(jnp.int32, (BLOCK, 2 * HEAD_DIM), 1) < HEAD_DIM

    def att_scores(qb, pair):
        rows = slice(qb * BLOCK, qb * BLOCK + 3 * BLOCK)
        kvh = pair // (GROUP // 2)
        q2 = q_scr[qb * BLOCK:(qb + 1) * BLOCK, pair * 2 * HEAD_DIM:(pair + 1) * 2 * HEAD_DIM]
        return _dot_nt(q2, jnp.concatenate([kvar[2 * kvh, rows], kvar[2 * kvh + 1, rows]], axis=0))

    def att_softmax(qb, pair, scores):
        if qb == 0:
            edge = jnp.where(i == 0, 1, 0)
        elif qb == QB_PER_TILE - 1:
            edge = jnp.where(i == last_i, 2, 0)
        else:
            edge = 0
        probs, inv = [], []
        for sub in range(2):
            h = 2 * pair + sub
            sink = sink_ref[h]
            p_blocks, inv_blocks = [], []
            for r in range(0, BLOCK, SM_ROWS):
                logits = (scores[r:r + SM_ROWS, sub * 3 * BLOCK:(sub + 1) * 3 * BLOCK]
                          + bias_ref[edge, h, r:r + SM_ROWS, :])
                m = jnp.maximum(jnp.max(logits, axis=-1, keepdims=True), sink)
                p = jnp.exp(logits - m)
                denom = jnp.sum(p, axis=-1, keepdims=True) + jnp.exp(sink - m)
                p_blocks.append(p.astype(BF16))
                inv_blocks.append(1.0 / denom)
            probs.append(jnp.concatenate(p_blocks, axis=0))
            inv.append(jnp.concatenate(inv_blocks, axis=0))
        return jnp.concatenate(probs, axis=1), jnp.where(lane_low, inv[0], inv[1])

    def att_values(qb, pair, probs, scale):
        rows = slice(qb * BLOCK, qb * BLOCK + 3 * BLOCK)
        kvh = pair // (GROUP // 2)
        o = _dot(probs, jnp.concatenate([vvar[2 * kvh, rows], vvar[2 * kvh + 1, rows]], axis=0))
        att_scr[qb * BLOCK:(qb + 1) * BLOCK, pair * 2 * HEAD_DIM:(pair + 1) * 2 * HEAD_DIM] = (
            o * scale).astype(BF16)

    lo = BLOCK - CONV_HALO
    hi = BLOCK + TILE + CONV_HALO
    slabs = PROJ_COLS // LANES

    def conv_input(j):
        cc = _dot(hext[lo:hi], win_ref[:, CC_OFF + j * PROJ_COLS:CC_OFF + (j + 1) * PROJ_COLS])
        cx = _dot(hext[lo:hi], win_ref[:, CX_OFF + j * PROJ_COLS:CX_OFF + (j + 1) * PROJ_COLS])
        row = lax.broadcasted_iota(jnp.int32, (TILE + 2 * CONV_HALO, 1), 0)
        inside = ((row >= CONV_HALO) | (i > 0)) & ((row < CONV_HALO + TILE) | (i < last_i))
        u = jnp.where(inside, cc * cx, 0.0)
        for s in range(slabs):
            u_scr[j * slabs + s, pl.ds(0, TILE + 2 * CONV_HALO, stride=ROW_PITCH), :] = (
                u[:, s * LANES:(s + 1) * LANES])

    def conv_branch(j):
        cb = _dot(hcb, win_ref[:, CB_OFF + j * PROJ_COLS:CB_OFF + (j + 1) * PROJ_COLS])
        taps = []
        for s in range(slabs):
            lanes = slice(j * PROJ_COLS + s * LANES, j * PROJ_COLS + (s + 1) * LANES)
            w = cw_ref[:, lanes]
            taps.append([jnp.broadcast_to(w[k:k + 1], (MIX_ROWS, LANES)) for k in range(3)]
                        + [jnp.broadcast_to(cbias_ref[:, lanes], (MIX_ROWS, LANES))])
        for rb in range(TILE // MIX_ROWS):
            r0 = CONV_HALO + rb * MIX_ROWS
            parts = []
            for s in range(slabs):
                ld = lambda r: u_scr[j * slabs + s, pl.ds(ROW_PITCH * r, MIX_ROWS, stride=ROW_PITCH), :]
                t = taps[s]
                dw = ld(r0 - 1) * t[0] + ld(r0) * t[1] + ld(r0 + 1) * t[2] + t[3]
                parts.append(cb[rb * MIX_ROWS:(rb + 1) * MIX_ROWS, s * LANES:(s + 1) * LANES] * dw)
            conv_scr[j, rb * MIX_ROWS:(rb + 1) * MIX_ROWS] = jnp.concatenate(parts, axis=1).astype(BF16)

    def gate(n):
        cols = slice(n * PROJ_COLS, (n + 1) * PROJ_COLS)
        pre = _dot(hcb, win_ref[:, GATE_OFF + n * PROJ_COLS:GATE_OFF + (n + 1) * PROJ_COLS])
        bias = bg_ref[:, cols]
        for rb in range(TILE // BLOCK):
            rows = slice(rb * BLOCK, (rb + 1) * BLOCK)
            g_scr[n, rows] = _sigmoid(pre[rows] + bias)

    n_conv = CONV_WIDTH // PROJ_COLS
    n_gate = 2 * D_MODEL // PROJ_COLS
    proj_units = ([functools.partial(conv_input, j) for j in range(n_conv)]
                  + [functools.partial(conv_branch, j) for j in range(n_conv)]
                  + [functools.partial(gate, n) for n in range(n_gate)])
    n_out = D_MODEL // PROJ_COLS
    half_rows = TILE // 2

    def merge(half, n):
        rows = slice(half * half_rows, (half + 1) * half_rows)
        cols = slice(n * PROJ_COLS, (n + 1) * PROJ_COLS)
        conv = jnp.concatenate([conv_scr[j, rows] for j in range(n_conv)], axis=1)
        ya = _dot(att_scr[rows], wa_ref[:, cols])
        yc = _dot(conv, wc_ref[:, cols])
        merged_scr[n, rows] = (g_scr[n, rows] * ya + g_scr[n_out + n, rows] * yc).astype(BF16)

    def out_proj(lo, hi):
        rows = slice(lo, hi)
        return _dot(jnp.concatenate([merged_scr[n, rows] for n in range(n_out)], axis=1), wo_ref[...])

    def finish(lo, mix, blocks):
        for blk in blocks:
            r = lo + blk * LN_ROWS
            out_ref[0, r:r + LN_ROWS] = _layer_norm(
                DEEPNORM_ALPHA * hres[r:r + LN_ROWS] + mix[blk * LN_ROWS:(blk + 1) * LN_ROWS],
                g2_ref[...], b2_ref[...])

    att_units = [(qb, pair) for qb in range(QB_PER_TILE) for pair in range(N_Q_HEADS // 2)]
    filler = proj_units + [functools.partial(merge, 0, n) for n in range(n_out)]
    assert len(filler) == len(att_units) and len(proj_units) >= len(att_units) // 2
    scores = att_scores(*att_units[0])
    for k, unit in enumerate(att_units):
        ahead = att_scores(*att_units[k + 1]) if k + 1 < len(att_units) else None
        probs, scale = att_softmax(*unit, scores)
        filler[k]()
        att_values(*unit, probs, scale)
        scores = ahead
    ln_blocks = half_rows // LN_ROWS
    mix = out_proj(0, half_rows)
    for n in range(n_out):
        finish(0, mix, range(n * ln_blocks // n_out, (n + 1) * ln_blocks // n_out))
        merge(1, n)
    quarter = half_rows // 2
    mix_a = out_proj(half_rows, half_rows + quarter)
    mix_b = out_proj(half_rows + quarter, TILE)
    finish(half_rows, mix_a, range(quarter // LN_ROWS))
    finish(half_rows + quarter, mix_b, range(quarter // LN_ROWS))


def _const_spec(shape):
    return pl.BlockSpec(shape, lambda b, i: (0,) * len(shape), pipeline_mode=pl.Buffered(1))


def _mixer(x, ln_g, ln_b, w_in, b_gates, sink, bias_tbl, conv_w, conv_b, w_a, w_c, w_o, g2, b2,
           w_ffn_up, w_ffn_down):
    bsz, seq, _ = x.shape
    n_tiles = seq // TILE
    blocks_per_tile = TILE // BLOCK
    n_blocks = seq // BLOCK
    in_specs = [
        pl.BlockSpec((1, BLOCK, D_MODEL), lambda b, i: (b, jnp.maximum(i * blocks_per_tile - 1, 0), 0)),
        pl.BlockSpec((1, TILE, D_MODEL), lambda b, i: (b, i, 0)),
        pl.BlockSpec((1, BLOCK, D_MODEL),
                     lambda b, i: (b, jnp.minimum((i + 1) * blocks_per_tile, n_blocks - 1), 0)),
        _const_spec((1, D_MODEL)), _const_spec((1, D_MODEL)),
        pl.BlockSpec(memory_space=pl.ANY),
        _const_spec((1, 2 * D_MODEL)),
        pl.BlockSpec(memory_space=pltpu.SMEM),
        _const_spec((3, N_Q_HEADS, BLOCK, 3 * BLOCK)),
        _const_spec((3, CONV_WIDTH)), _const_spec((1, CONV_WIDTH)),
        pl.BlockSpec(memory_space=pl.ANY), pl.BlockSpec(memory_space=pl.ANY),
        pl.BlockSpec(memory_space=pl.ANY),
        _const_spec((1, D_MODEL)), _const_spec((1, D_MODEL)),
    ]
    n_steps = bsz * n_tiles
    cast_specs, cast_shapes = [], []
    for w in (w_ffn_up, w_ffn_down):
        rows = next(r for r in range(BF16_SUBLANES, w.shape[0] + 1, BF16_SUBLANES)
                    if w.shape[0] % r == 0 and w.shape[0] // r <= n_steps)
        last = w.shape[0] // rows - 1
        cast_specs.append(pl.BlockSpec(
            (rows, w.shape[1]), lambda b, i, last=last: (jnp.minimum(b * n_tiles + i, last), 0)))
        cast_shapes.append(jax.ShapeDtypeStruct(w.shape, BF16))
    ext = TILE + 2 * BLOCK
    return pl.pallas_call(
        _mixer_kernel,
        out_shape=[jax.ShapeDtypeStruct((bsz, seq, D_MODEL), F32)] + cast_shapes,
        grid=(bsz, n_tiles),
        in_specs=in_specs + cast_specs,
        out_specs=[pl.BlockSpec((1, TILE, D_MODEL), lambda b, i: (b, i, 0))] + cast_specs,
        scratch_shapes=[
            pltpu.VMEM((ext, D_MODEL), BF16),
            pltpu.VMEM((TILE, D_MODEL), F32),
            pltpu.VMEM((TILE, ATT_WIDTH), BF16),
            pltpu.VMEM((4, ext, KV_WIDTH), BF16),
            pltpu.VMEM((4, ext, KV_WIDTH), BF16),
            pltpu.VMEM((TILE, ATT_WIDTH), BF16),
            pltpu.VMEM((CONV_WIDTH // LANES, ROW_PITCH * (TILE + 2 * CONV_HALO), LANES), F32),
            pltpu.VMEM((CONV_WIDTH // PROJ_COLS, TILE, PROJ_COLS), BF16),
            pltpu.VMEM((2 * D_MODEL // PROJ_COLS, TILE, PROJ_COLS), F32),
            pltpu.VMEM((D_MODEL // PROJ_COLS, TILE, PROJ_COLS), BF16),
            pltpu.VMEM((D_MODEL, IN_PROJ_WIDTH), BF16),
            pltpu.VMEM((ATT_WIDTH, D_MODEL), BF16),
            pltpu.VMEM((CONV_WIDTH, D_MODEL), BF16),
            pltpu.VMEM((D_MODEL, D_MODEL), BF16),
            pltpu.VMEM((2, STAGE_ROWS_WIDE, IN_PROJ_WIDTH), F32),
            pltpu.VMEM((2, STAGE_ROWS, D_MODEL), F32),
            pltpu.SemaphoreType.DMA((2,)),
        ],
        compiler_params=pltpu.CompilerParams(
            dimension_semantics=("arbitrary", "arbitrary"), vmem_limit_bytes=VMEM_LIMIT_BYTES),
        name="mixer",
    )(x, x, x, ln_g, ln_b, w_in, b_gates, sink, bias_tbl, conv_w, conv_b, w_a, w_c, w_o, g2, b2,
      w_ffn_up, w_ffn_down)


def _ffn_kernel(hp_ref, hc_ref, hn_ref, wup_ref, cw_ref, cb_ref, wdn_ref, g_ref, b_ref, out_ref,
                hext, a_scr, u_scr, act_scr, acc):
    i = pl.program_id(1)
    last_i = pl.num_programs(1) - 1

    hext[0:CONV_HALO] = jnp.where(i > 0, hp_ref[0], 0.0).astype(BF16)
    hext[CONV_HALO:CONV_HALO + TILE] = hc_ref[0].astype(BF16)
    hext[CONV_HALO + TILE:] = jnp.where(i < last_i, hn_ref[0], 0.0).astype(BF16)

    tile_half = TILE // 2
    ext_split = (0, tile_half + 2 * CONV_HALO, TILE + 2 * CONV_HALO)

    def up_proj(c, half):
        col = c * FF_CHUNK
        start, stop = ext_split[half], ext_split[half + 1]
        dst = pl.ds(ROW_PITCH * start, stop - start, stride=ROW_PITCH)
        for scr, off in ((a_scr, col), (u_scr, D_FF + col)):
            res = _dot(hext[start:stop], wup_ref[:, off:off + FF_CHUNK])
            for slab in range(FF_CHUNK // LANES):
                scr[c % 2, slab, dst, :] = res[:, slab * LANES:(slab + 1) * LANES]

    def down_proj(c, half):
        rows = slice(half * tile_half, (half + 1) * tile_half)
        part = _dot(act_scr[c % 2, rows], wdn_ref[c * FF_CHUNK:(c + 1) * FF_CHUNK, :])
        if c == 0:
            acc[rows] = part
        else:
            acc[rows] += part

    def finish(half):
        rows = slice(half * tile_half, (half + 1) * tile_half)
        out_ref[0, rows] = _layer_norm(DEEPNORM_ALPHA * hc_ref[0, rows] + acc[rows], g_ref[...], b_ref[...])

    def conv_taps(col):
        out = []
        for slab in range(FF_CHUNK // LANES):
            lanes = slice(col + slab * LANES, col + (slab + 1) * LANES)
            w = cw_ref[:, lanes]
            out.append([jnp.broadcast_to(w[k:k + 1], (FF_ROWS, LANES)) for k in range(3)]
                       + [jnp.broadcast_to(cb_ref[:, lanes], (FF_ROWS, LANES))])
        return out

    def conv3(scr, slot, slab, r0, taps):
        rows = lambda r: scr[slot, slab, pl.ds(ROW_PITCH * r, FF_ROWS, stride=ROW_PITCH), :]
        return rows(r0 - 1) * taps[0] + rows(r0) * taps[1] + rows(r0 + 1) * taps[2] + taps[3]

    def gate_rows(c, rb, a_taps, u_taps):
        r0 = CONV_HALO + rb * FF_ROWS
        parts = []
        for slab in range(FF_CHUNK // LANES):
            a = conv3(a_scr, c % 2, slab, r0, a_taps[slab])
            u = conv3(u_scr, c % 2, slab, r0, u_taps[slab])
            parts.append(a * _sigmoid(a) * u)
        act_scr[c % 2, rb * FF_ROWS:(rb + 1) * FF_ROWS] = jnp.concatenate(parts, axis=1).astype(BF16)

    units = [(c, half) for c in range(N_FF_CHUNKS) for half in range(2)]
    blocks_per_half = tile_half // FF_ROWS
    up_proj(*units[0])
    taps = {}
    for k, (c, half) in enumerate(units):
        if half == 0:
            taps = {"a": conv_taps(c * FF_CHUNK), "u": conv_taps(D_FF + c * FF_CHUNK)}
        mxu_units = []
        if k + 1 < len(units):
            mxu_units.append(functools.partial(up_proj, *units[k + 1]))
        if k >= 1:
            mxu_units.append(functools.partial(down_proj, *units[k - 1]))
        per_unit = blocks_per_half // len(mxu_units)
        for m, unit in enumerate(mxu_units):
            unit()
            for rb in range(m * per_unit, (m + 1) * per_unit):
                gate_rows(c, half * blocks_per_half + rb, taps["a"], taps["u"])
    finish(0)
    down_proj(*units[-1])
    finish(1)


def _ffn(h, w_up, conv_w, conv_b, w_down, g, b):
    bsz, seq, _ = h.shape
    n_tiles = seq // TILE
    halo_per_tile = TILE // CONV_HALO
    n_halo_blocks = seq // CONV_HALO
    in_specs = [
        pl.BlockSpec((1, CONV_HALO, D_MODEL), lambda b, i: (b, jnp.maximum(i * halo_per_tile - 1, 0), 0)),
        pl.BlockSpec((1, TILE, D_MODEL), lambda b, i: (b, i, 0)),
        pl.BlockSpec((1, CONV_HALO, D_MODEL),
                     lambda b, i: (b, jnp.minimum((i + 1) * halo_per_tile, n_halo_blocks - 1), 0)),
        _const_spec((D_MODEL, 2 * D_FF)),
        _const_spec((3, 2 * D_FF)), _const_spec((1, 2 * D_FF)),
        _const_spec((D_FF, D_MODEL)),
        _const_spec((1, D_MODEL)), _const_spec((1, D_MODEL)),
    ]
    ext = TILE + 2 * CONV_HALO
    return pl.pallas_call(
        _ffn_kernel,
        out_shape=jax.ShapeDtypeStruct((bsz, seq, D_MODEL), F32),
        grid=(bsz, n_tiles),
        in_specs=in_specs,
        out_specs=pl.BlockSpec((1, TILE, D_MODEL), lambda b, i: (b, i, 0)),
        scratch_shapes=[
            pltpu.VMEM((ext, D_MODEL), BF16),
            pltpu.VMEM((2, FF_CHUNK // LANES, ROW_PITCH * ext, LANES), F32),
            pltpu.VMEM((2, FF_CHUNK // LANES, ROW_PITCH * ext, LANES), F32),
            pltpu.VMEM((2, TILE, FF_CHUNK), BF16),
            pltpu.VMEM((TILE, D_MODEL), F32),
        ],
        compiler_params=pltpu.CompilerParams(
            dimension_semantics=("arbitrary", "arbitrary"), vmem_limit_bytes=VMEM_LIMIT_BYTES),
        name="ffn",
    )(h, h, h, w_up, conv_w, conv_b, w_down, g, b)


def kernel(x, ln_in_g, ln_in_b, w_in, b_gates, attn_sink, rel_bias, conv_w, conv_b, w_att_branch,
           w_conv_branch, w_o, ln_mix_g, ln_mix_b, w_ffn_up, ffn_conv_w, ffn_conv_b, w_ffn_down,
           ln_ffn_g, ln_ffn_b):
    assert w_in.shape[0] == DEPTH == 1
    row = lambda v: v.reshape(1, -1)
    bias_tbl = _bias_table(rel_bias)
    h, w_up, w_down = _mixer(x, row(ln_in_g), row(ln_in_b), w_in[0], row(b_gates[0]), attn_sink[0],
                             bias_tbl, conv_w[0], row(conv_b[0]), w_att_branch[0],
                             w_conv_branch[0], w_o[0], row(ln_mix_g[0]), row(ln_mix_b[0]),
                             w_ffn_up[0], w_ffn_down[0])
    return _ffn(h, w_up, ffn_conv_w[0], row(ffn_conv_b[0]), w_down, row(ln_ffn_g[0]), row(ln_ffn_b[0]))
```

```python
import functools
import math

import jax
import jax.numpy as jnp
import numpy as np
from jax import lax
from jax.experimental import pallas as pl
from jax.experimental.pallas import tpu as pltpu

D_MODEL = 1024
HEAD_DIM = 64
N_Q_HEADS = 8
N_KV_HEADS = 2
GROUP = N_Q_HEADS // N_KV_HEADS
ATT_WIDTH = N_Q_HEADS * HEAD_DIM
KV_WIDTH = N_KV_HEADS * HEAD_DIM
WINDOW = 128
BLOCK = 128
CONV_WIDTH = D_MODEL // 2
D_FF = 2816
N_BUCKETS = 32
MAX_DISTANCE = 128
LN_EPS = 1e-5
DEPTH = 1
DEEPNORM_ALPHA = (2 * DEPTH) ** 0.25
MASK_VALUE = -1e30

Q_OFF = 0
K_OFF = ATT_WIDTH
V_OFF = K_OFF + KV_WIDTH
CB_OFF = V_OFF + KV_WIDTH
CC_OFF = CB_OFF + CONV_WIDTH
CX_OFF = CC_OFF + CONV_WIDTH
GATE_OFF = CX_OFF + CONV_WIDTH
IN_PROJ_WIDTH = GATE_OFF + 2 * D_MODEL

TILE = 512
QB_PER_TILE = TILE // BLOCK
CONV_HALO = 16
FF_CHUNK = 256
N_FF_CHUNKS = D_FF // FF_CHUNK
FF_ROWS = 32
PROJ_COLS = 256
MIX_ROWS = 32
STAGE_ROWS = 256
STAGE_ROWS_WIDE = 128
LN_ROWS = 32
SM_ROWS = 32
LANES = 128
BF16_SUBLANES = 16
ROW_PITCH = 2
VMEM_LIMIT_BYTES = 60 * 1024 * 1024

F32 = jnp.float32
BF16 = jnp.bfloat16


def _layer_norm(x, g, b):
    mu = jnp.mean(x, axis=-1, keepdims=True)
    xc = x - mu
    var = jnp.mean(xc * xc, axis=-1, keepdims=True)
    return xc * lax.rsqrt(var + LN_EPS) * g + b


def _sigmoid(x):
    return 0.5 * jnp.tanh(0.5 * x) + 0.5


def _dot(a, b):
    return jnp.dot(a, b, preferred_element_type=F32)


def _dot_nt(a, b):
    return lax.dot_general(a, b, (((1,), (1,)), ((), ())), preferred_element_type=F32)


def _bias_by_rel(rel_bias):
    rel = jnp.arange(4 * BLOCK) - (2 * BLOCK - 1)
    half = N_BUCKETS // 2
    max_exact = half // 2
    offset = jnp.where(rel > 0, half, 0)
    n = jnp.abs(rel)
    nf = jnp.maximum(n, 1).astype(jnp.float32)
    large = max_exact + (jnp.log(nf / max_exact) / math.log(MAX_DISTANCE / max_exact)
                         * (half - max_exact)).astype(jnp.int32)
    large = jnp.minimum(large, half - 1)
    bucket = offset + jnp.where(n < max_exact, n, large)
    return jnp.transpose(rel_bias.astype(F32)[bucket], (1, 0))


def _bias_table_kernel(bias_ref, out_ref):
    shape = (BLOCK, 3 * BLOCK)
    col = lax.broadcasted_iota(jnp.int32, shape, 1)
    rel = col - BLOCK - lax.broadcasted_iota(jnp.int32, shape, 0)
    in_window = jnp.abs(rel) <= WINDOW
    for h in range(N_Q_HEADS):
        by_rel = jnp.broadcast_to(bias_ref[h:h + 1, :], (BLOCK, 4 * BLOCK))
        band = pltpu.roll(by_rel, 3 * BLOCK + 1, axis=1, stride=1, stride_axis=0)[:, :3 * BLOCK]
        t = jnp.where(in_window, band, MASK_VALUE)
        out_ref[0, h] = t
        out_ref[1, h] = jnp.where(col < BLOCK, MASK_VALUE, t)
        out_ref[2, h] = jnp.where(col >= 2 * BLOCK, MASK_VALUE, t)


def _bias_table(rel_bias):
    return pl.pallas_call(
        _bias_table_kernel,
        out_shape=jax.ShapeDtypeStruct((3, N_Q_HEADS, BLOCK, 3 * BLOCK), F32),
        in_specs=[pl.BlockSpec(memory_space=pltpu.VMEM)],
        out_specs=pl.BlockSpec(memory_space=pltpu.VMEM),
        name="bias_table",
    )(_bias_by_rel(rel_bias))


def _stage_weight(src_hbm, dst, stage, sem):
    chunk = stage.shape[1]
    n_chunks = src_hbm.shape[0] // chunk
    assert n_chunks * chunk == src_hbm.shape[0] and stage.shape[2] == src_hbm.shape[1]

    def copy(k):
        return pltpu.make_async_copy(src_hbm.at[pl.ds(k * chunk, chunk)], stage.at[k % 2], sem.at[k % 2])

    copy(0).start()
    for k in range(n_chunks):
        if k + 1 < n_chunks:
            copy(k + 1).start()
        copy(k).wait()
        dst[k * chunk:(k + 1) * chunk] = stage[k % 2].astype(dst.dtype)


def _mixer_kernel(xc_ref, xn_ref, lng_ref, lnb_ref, win_hbm, bg_ref, sink_ref, bias_ref,
                  cw_ref, cbias_ref, wa_hbm, wc_hbm, wo_hbm, g2_ref, b2_ref, wup_in, wdn_in,
                  out_ref, wup_out, wdn_out,
                  hext, hres, q_scr, kvar, vvar, att_scr, u_scr, conv_scr, g_scr, merged_scr,
                  win_ref, wa_ref, wc_ref, wo_ref, stage_in, stage_sq, dma_sem):
    i = pl.program_id(1)
    last_i = pl.num_programs(1) - 1

    @pl.when((pl.program_id(0) == 0) & (i == 0))
    def _():
        _stage_weight(win_hbm, win_ref, stage_in, dma_sem)
        _stage_weight(wa_hbm, wa_ref, stage_sq, dma_sem)
        _stage_weight(wc_hbm, wc_ref, stage_sq, dma_sem)
        _stage_weight(wo_hbm, wo_ref, stage_sq, dma_sem)
        kvar[:, TILE:TILE + BLOCK] = jnp.zeros((4, BLOCK, KV_WIDTH), BF16)
        vvar[:, TILE:TILE + BLOCK] = jnp.zeros((4, BLOCK, KV_WIDTH), BF16)
        hext[TILE + BLOCK - CONV_HALO:TILE + BLOCK] = jnp.zeros((CONV_HALO, D_MODEL), BF16)

    wup_out[...] = wup_in[...].astype(BF16)
    wdn_out[...] = wdn_in[...].astype(BF16)

    kvar[:, 0:BLOCK] = kvar[:, TILE:TILE + BLOCK]
    vvar[:, 0:BLOCK] = vvar[:, TILE:TILE + BLOCK]
    hext[BLOCK - CONV_HALO:BLOCK] = hext[TILE + BLOCK - CONV_HALO:TILE + BLOCK]

    lng = lng_ref[...]
    lnb = lnb_ref[...]
    ext = TILE + 2 * BLOCK

    def norm_rows(lo, hi):
        assert lo >= BLOCK
        for r in range(lo, hi, LN_ROWS):
            if r < BLOCK + TILE:
                x = xc_ref[0, r - BLOCK:r - BLOCK + LN_ROWS]
            else:
                x = xn_ref[0, r - BLOCK - TILE:r - BLOCK - TILE + LN_ROWS]
            y = _layer_norm(x, lng, lnb)
            if BLOCK <= r < BLOCK + TILE:
                hres[r - BLOCK:r - BLOCK + LN_ROWS] = y
            hext[r:r + LN_ROWS] = y.astype(BF16)

    def project_qkv(lo, hi):
        rows = slice(lo, hi)
        kv = _dot(hext[rows], win_ref[:, K_OFF:K_OFF + 2 * KV_WIDTH])
        low = lax.broadcasted_iota(jnp.int32, (hi - lo, KV_WIDTH), 1) < HEAD_DIM
        for src, dst in ((kv[:, :KV_WIDTH], kvar), (kv[:, KV_WIDTH:], vvar)):
            rolled = pltpu.roll(src, HEAD_DIM, axis=1)
            dst[0, rows] = jnp.where(low, src, 0.0).astype(BF16)
            dst[1, rows] = jnp.where(low, 0.0, rolled).astype(BF16)
            dst[2, rows] = jnp.where(low, rolled, 0.0).astype(BF16)
            dst[3, rows] = jnp.where(low, 0.0, src).astype(BF16)
        q_lo, q_hi = max(lo, BLOCK), min(hi, BLOCK + TILE)
        if q_hi > q_lo:
            q_scr[q_lo - BLOCK:q_hi - BLOCK] = (
                _dot(hext[q_lo:q_hi], win_ref[:, Q_OFF:Q_OFF + ATT_WIDTH]) * (HEAD_DIM ** -0.5)
            ).astype(BF16)

    for lo_rows, hi_rows in ((BLOCK, ext // 2), (ext // 2, BLOCK + TILE), (BLOCK + TILE, ext)):
        norm_rows(lo_rows, hi_rows)
        project_qkv(lo_rows, hi_rows)
    hcb = hext[BLOCK:BLOCK + TILE]

    lane_low = lax.broadcasted_iota(jnp.int32, (BLOCK, 2 * HEAD_DIM), 1) < HEAD_DIM

    def att_scores(qb, pair):
        rows = slice(qb * BLOCK, qb * BLOCK + 3 * BLOCK)
        kvh = pair // (GROUP // 2)
        q2 = q_scr[qb * BLOCK:(qb + 1) * BLOCK, pair * 2 * HEAD_DIM:(pair + 1) * 2 * HEAD_DIM]
        return _dot_nt(q2, jnp.concatenate([kvar[2 * kvh, rows], kvar[2 * kvh + 1, rows]], axis=0))

    def att_softmax(qb, pair, scores):
        if qb == 0:
            edge = jnp.where(i == 0, 1, 0)
        elif qb == QB_PER_TILE - 1:
            edge = jnp.where(i == last_i, 2, 0)
        else:
            edge = 0
        probs, inv = [], []
        for sub in range(2):
            h = 2 * pair + sub
            sink = sink_ref[h]
            p_blocks, inv_blocks = [], []
            for r in range(0, BLOCK, SM_ROWS):
                logits = (scores[r:r + SM_ROWS, sub * 3 * BLOCK:(sub + 1) * 3 * BLOCK]
                          + bias_ref[edge, h, r:r + SM_ROWS, :])
                m = jnp.maximum(jnp.max(logits, axis=-1, keepdims=True), sink)
                p = jnp.exp(logits - m)
                denom = jnp.sum(p, axis=-1, keepdims=True) + jnp.exp(sink - m)
                p_blocks.append(p.astype(BF16))
                inv_blocks.append(1.0 / denom)
            probs.append(jnp.concatenate(p_blocks, axis=0))
            inv.append(jnp.concatenate(inv_blocks, axis=0))
        return jnp.concatenate(probs, axis=1), jnp.where(lane_low, inv[0], inv[1])

    def att_values(qb, pair, probs, scale):
        rows = slice(qb * BLOCK, qb * BLOCK + 3 * BLOCK)
        kvh = pair // (GROUP // 2)
        o = _dot(probs, jnp.concatenate([vvar[2 * kvh, rows], vvar[2 * kvh + 1, rows]], axis=0))
        att_scr[qb * BLOCK:(qb + 1) * BLOCK, pair * 2 * HEAD_DIM:(pair + 1) * 2 * HEAD_DIM] = (
            o * scale).astype(BF16)

    lo = BLOCK - CONV_HALO
    hi = BLOCK + TILE + CONV_HALO
    slabs = PROJ_COLS // LANES

    def conv_input(j):
        cc = _dot(hext[lo:hi], win_ref[:, CC_OFF + j * PROJ_COLS:CC_OFF + (j + 1) * PROJ_COLS])
        cx = _dot(hext[lo:hi], win_ref[:, CX_OFF + j * PROJ_COLS:CX_OFF + (j + 1) * PROJ_COLS])
        row = lax.broadcasted_iota(jnp.int32, (TILE + 2 * CONV_HALO, 1), 0)
        inside = ((row >= CONV_HALO) | (i > 0)) & ((row < CONV_HALO + TILE) | (i < last_i))
        u = jnp.where(inside, cc * cx, 0.0)
        for s in range(slabs):
            u_scr[j * slabs + s, pl.ds(0, TILE + 2 * CONV_HALO, stride=ROW_PITCH), :] = (
                u[:, s * LANES:(s + 1) * LANES])

    def conv_branch(j):
        cb = _dot(hcb, win_ref[:, CB_OFF + j * PROJ_COLS:CB_OFF + (j + 1) * PROJ_COLS])
        taps = []
        for s in range(slabs):
            lanes = slice(j * PROJ_COLS + s * LANES, j * PROJ_COLS + (s + 1) * LANES)
            w = cw_ref[:, lanes]
            taps.append([jnp.broadcast_to(w[k:k + 1], (MIX_ROWS, LANES)) for k in range(3)]
                        + [jnp.broadcast_to(cbias_ref[:, lanes], (MIX_ROWS, LANES))])
        for rb in range(TILE // MIX_ROWS):
            r0 = CONV_HALO + rb * MIX_ROWS
            parts = []
            for s in range(slabs):
                ld = lambda r: u_scr[j * slabs + s, pl.ds(ROW_PITCH * r, MIX_ROWS, stride=ROW_PITCH), :]
                t = taps[s]
                dw = ld(r0 - 1) * t[0] + ld(r0) * t[1] + ld(r0 + 1) * t[2] + t[3]
                parts.append(cb[rb * MIX_ROWS:(rb + 1) * MIX_ROWS, s * LANES:(s + 1) * LANES] * dw)
            conv_scr[j, rb * MIX_ROWS:(rb + 1) * MIX_ROWS] = jnp.concatenate(parts, axis=1).astype(BF16)

    def gate(n):
        cols = slice(n * PROJ_COLS, (n + 1) * PROJ_COLS)
        pre = _dot(hcb, win_ref[:, GATE_OFF + n * PROJ_COLS:GATE_OFF + (n + 1) * PROJ_COLS])
        bias = bg_ref[:, cols]
        for rb in range(TILE // BLOCK):
            rows = slice(rb * BLOCK, (rb + 1) * BLOCK)
            g_scr[n, rows] = _sigmoid(pre[rows] + bias)

    n_conv = CONV_WIDTH // PROJ_COLS
    n_gate = 2 * D_MODEL // PROJ_COLS
    proj_units = ([functools.partial(conv_input, j) for j in range(n_conv)]
                  + [functools.partial(conv_branch, j) for j in range(n_conv)]
                  + [functools.partial(gate, n) for n in range(n_gate)])
    n_out = D_MODEL // PROJ_COLS
    half_rows = TILE // 2

    def merge(half, n):
        rows = slice(half * half_rows, (half + 1) * half_rows)
        cols = slice(n * PROJ_COLS, (n + 1) * PROJ_COLS)
        conv = jnp.concatenate([conv_scr[j, rows] for j in range(n_conv)], axis=1)
        ya = _dot(att_scr[rows], wa_ref[:, cols])
        yc = _dot(conv, wc_ref[:, cols])
        merged_scr[n, rows] = (g_scr[n, rows] * ya + g_scr[n_out + n, rows] * yc).astype(BF16)

    def out_proj(lo, hi):
        rows = slice(lo, hi)
        return _dot(jnp.concatenate([merged_scr[n, rows] for n in range(n_out)], axis=1), wo_ref[...])

    def finish(lo, mix, blocks):
        for blk in blocks:
            r = lo + blk * LN_ROWS
            out_ref[0, r:r + LN_ROWS] = _layer_norm(
                DEEPNORM_ALPHA * hres[r:r + LN_ROWS] + mix[blk * LN_ROWS:(blk + 1) * LN_ROWS],
                g2_ref[...], b2_ref[...])

    att_units = [(qb, pair) for qb in range(QB_PER_TILE) for pair in range(N_Q_HEADS // 2)]
    filler = proj_units + [functools.partial(merge, 0, n) for n in range(n_out)]
    assert len(filler) == len(att_units) and len(proj_units) >= len(att_units) // 2
    scores = att_scores(*att_units[0])
    for k, unit in enumerate(att_units):
        ahead = att_scores(*att_units[k + 1]) if k + 1 < len(att_units) else None
        probs, scale = att_softmax(*unit, scores)
        filler[k]()
        att_values(*unit, probs, scale)
        scores = ahead
    ln_blocks = half_rows // LN_ROWS
    mix = out_proj(0, half_rows)
    for n in range(n_out):
        finish(0, mix, range(n * ln_blocks // n_out, (n + 1) * ln_blocks // n_out))
        merge(1, n)
    quarter = half_rows // 2
    mix_a = out_proj(half_rows, half_rows + quarter)
    mix_b = out_proj(half_rows + quarter, TILE)
    finish(half_rows, mix_a, range(quarter // LN_ROWS))
    finish(half_rows + quarter, mix_b, range(quarter // LN_ROWS))


def _const_spec(shape):
    return pl.BlockSpec(shape, lambda b, i: (0,) * len(shape), pipeline_mode=pl.Buffered(1))


def _mixer(x, ln_g, ln_b, w_in, b_gates, sink, bias_tbl, conv_w, conv_b, w_a, w_c, w_o, g2, b2,
           w_ffn_up, w_ffn_down):
    bsz, seq, _ = x.shape
    n_tiles = seq // TILE
    blocks_per_tile = TILE // BLOCK
    n_blocks = seq // BLOCK
    in_specs = [
        pl.BlockSpec((1, TILE, D_MODEL), lambda b, i: (b, i, 0)),
        pl.BlockSpec((1, BLOCK, D_MODEL),
                     lambda b, i: (b, jnp.minimum((i + 1) * blocks_per_tile, n_blocks - 1), 0)),
        _const_spec((1, D_MODEL)), _const_spec((1, D_MODEL)),
        pl.BlockSpec(memory_space=pl.ANY),
        _const_spec((1, 2 * D_MODEL)),
        pl.BlockSpec(memory_space=pltpu.SMEM),
        _const_spec((3, N_Q_HEADS, BLOCK, 3 * BLOCK)),
        _const_spec((3, CONV_WIDTH)), _const_spec((1, CONV_WIDTH)),
        pl.BlockSpec(memory_space=pl.ANY), pl.BlockSpec(memory_space=pl.ANY),
        pl.BlockSpec(memory_space=pl.ANY),
        _const_spec((1, D_MODEL)), _const_spec((1, D_MODEL)),
    ]
    n_steps = bsz * n_tiles
    cast_specs, cast_shapes = [], []
    for w in (w_ffn_up, w_ffn_down):
        rows = next(r for r in range(BF16_SUBLANES, w.shape[0] + 1, BF16_SUBLANES)
                    if w.shape[0] % r == 0 and w.shape[0] // r <= n_steps)
        last = w.shape[0] // rows - 1
        cast_specs.append(pl.BlockSpec(
            (rows, w.shape[1]), lambda b, i, last=last: (jnp.minimum(b * n_tiles + i, last), 0)))
        cast_shapes.append(jax.ShapeDtypeStruct(w.shape, BF16))
    ext = TILE + 2 * BLOCK
    return pl.pallas_call(
        _mixer_kernel,
        out_shape=[jax.ShapeDtypeStruct((bsz, seq, D_MODEL), F32)] + cast_shapes,
        grid=(bsz, n_tiles),
        in_specs=in_specs + cast_specs,
        out_specs=[pl.BlockSpec((1, TILE, D_MODEL), lambda b, i: (b, i, 0))] + cast_specs,
        scratch_shapes=[
            pltpu.VMEM((ext, D_MODEL), BF16),
            pltpu.VMEM((TILE, D_MODEL), F32),
            pltpu.VMEM((TILE, ATT_WIDTH), BF16),
            pltpu.VMEM((4, ext, KV_WIDTH), BF16),
            pltpu.VMEM((4, ext, KV_WIDTH), BF16),
            pltpu.VMEM((TILE, ATT_WIDTH), BF16),
            pltpu.VMEM((CONV_WIDTH // LANES, ROW_PITCH * (TILE + 2 * CONV_HALO), LANES), F32),
            pltpu.VMEM((CONV_WIDTH // PROJ_COLS, TILE, PROJ_COLS), BF16),
            pltpu.VMEM((2 * D_MODEL // PROJ_COLS, TILE, PROJ_COLS), F32),
            pltpu.VMEM((D_MODEL // PROJ_COLS, TILE, PROJ_COLS), BF16),
            pltpu.VMEM((D_MODEL, IN_PROJ_WIDTH), BF16),
            pltpu.VMEM((ATT_WIDTH, D_MODEL), BF16),
            pltpu.VMEM((CONV_WIDTH, D_MODEL), BF16),
            pltpu.VMEM((D_MODEL, D_MODEL), BF16),
            pltpu.VMEM((2, STAGE_ROWS_WIDE, IN_PROJ_WIDTH), F32),
            pltpu.VMEM((2, STAGE_ROWS, D_MODEL), F32),
            pltpu.SemaphoreType.DMA((2,)),
        ],
        compiler_params=pltpu.CompilerParams(
            dimension_semantics=("arbitrary", "arbitrary"), vmem_limit_bytes=VMEM_LIMIT_BYTES),
        name="mixer",
    )(x, x, ln_g, ln_b, w_in, b_gates, sink, bias_tbl, conv_w, conv_b, w_a, w_c, w_o, g2, b2,
      w_ffn_up, w_ffn_down)


def _ffn_kernel(hp_ref, hc_ref, hn_ref, wup_ref, cw_ref, cb_ref, wdn_ref, g_ref, b_ref, out_ref,
                hext, a_scr, u_scr, act_scr, acc):
    i = pl.program_id(1)
    last_i = pl.num_programs(1) - 1

    hext[0:CONV_HALO] = jnp.where(i > 0, hp_ref[0], 0.0).astype(BF16)
    hext[CONV_HALO:CONV_HALO + TILE] = hc_ref[0].astype(BF16)
    hext[CONV_HALO + TILE:] = jnp.where(i < last_i, hn_ref[0], 0.0).astype(BF16)

    tile_half = TILE // 2
    ext_split = (0, tile_half + 2 * CONV_HALO, TILE + 2 * CONV_HALO)

    def up_proj(c, half):
        col = c * FF_CHUNK
        start, stop = ext_split[half], ext_split[half + 1]
        dst = pl.ds(ROW_PITCH * start, stop - start, stride=ROW_PITCH)
        for scr, off in ((a_scr, col), (u_scr, D_FF + col)):
            res = _dot(hext[start:stop], wup_ref[:, off:off + FF_CHUNK])
            for slab in range(FF_CHUNK // LANES):
                scr[c % 2, slab, dst, :] = res[:, slab * LANES:(slab + 1) * LANES]

    def down_proj(c, half):
        rows = slice(half * tile_half, (half + 1) * tile_half)
        part = _dot(act_scr[c % 2, rows], wdn_ref[c * FF_CHUNK:(c + 1) * FF_CHUNK, :])
        if c == 0:
            acc[rows] = part
        else:
            acc[rows] += part

    def finish(half):
        rows = slice(half * tile_half, (half + 1) * tile_half)
        out_ref[0, rows] = _layer_norm(DEEPNORM_ALPHA * hc_ref[0, rows] + acc[rows], g_ref[...], b_ref[...])

    def conv_taps(col):
        out = []
        for slab in range(FF_CHUNK // LANES):
            lanes = slice(col + slab * LANES, col + (slab + 1) * LANES)
            w = cw_ref[:, lanes]
            out.append([jnp.broadcast_to(w[k:k + 1], (FF_ROWS, LANES)) for k in range(3)]
                       + [jnp.broadcast_to(cb_ref[:, lanes], (FF_ROWS, LANES))])
        return out

    def conv3(scr, slot, slab, r0, taps):
        rows = lambda r: scr[slot, slab, pl.ds(ROW_PITCH * r, FF_ROWS, stride=ROW_PITCH), :]
        return rows(r0 - 1) * taps[0] + rows(r0) * taps[1] + rows(r0 + 1) * taps[2] + taps[3]

    def gate_rows(c, rb, a_taps, u_taps):
        r0 = CONV_HALO + rb * FF_ROWS
        parts = []
        for slab in range(FF_CHUNK // LANES):
            a = conv3(a_scr, c % 2, slab, r0, a_taps[slab])
            u = conv3(u_scr, c % 2, slab, r0, u_taps[slab])
            parts.append(a * _sigmoid(a) * u)
        act_scr[c % 2, rb * FF_ROWS:(rb + 1) * FF_ROWS] = jnp.concatenate(parts, axis=1).astype(BF16)

    units = [(c, half) for c in range(N_FF_CHUNKS) for half in range(2)]
    blocks_per_half = tile_half // FF_ROWS
    up_proj(*units[0])
    taps = {}
    for k, (c, half) in enumerate(units):
        if half == 0:
            taps = {"a": conv_taps(c * FF_CHUNK), "u": conv_taps(D_FF + c * FF_CHUNK)}
        mxu_units = []
        if k + 1 < len(units):
            mxu_units.append(functools.partial(up_proj, *units[k + 1]))
        if k >= 1:
            mxu_units.append(functools.partial(down_proj, *units[k - 1]))
        per_unit = blocks_per_half // len(mxu_units)
        for m, unit in enumerate(mxu_units):
            unit()
            for rb in range(m * per_unit, (m + 1) * per_unit):
                gate_rows(c, half * blocks_per_half + rb, taps["a"], taps["u"])
    finish(0)
    down_proj(*units[-1])
    finish(1)


def _ffn(h, w_up, conv_w, conv_b, w_down, g, b):
    bsz, seq, _ = h.shape
    n_tiles = seq // TILE
    halo_per_tile = TILE // CONV_HALO
    n_halo_blocks = seq // CONV_HALO
    in_specs = [
        pl.BlockSpec((1, CONV_HALO, D_MODEL), lambda b, i: (b, jnp.maximum(i * halo_per_tile - 1, 0), 0)),
        pl.BlockSpec((1, TILE, D_MODEL), lambda b, i: (b, i, 0)),
        pl.BlockSpec((1, CONV_HALO, D_MODEL),
                     lambda b, i: (b, jnp.minimum((i + 1) * halo_per_tile, n_halo_blocks - 1), 0)),
        _const_spec((D_MODEL, 2 * D_FF)),
        _const_spec((3, 2 * D_FF)), _const_spec((1, 2 * D_FF)),
        _const_spec((D_FF, D_MODEL)),
        _const_spec((1, D_MODEL)), _const_spec((1, D_MODEL)),
    ]
    ext = TILE + 2 * CONV_HALO
    return pl.pallas_call(
        _ffn_kernel,
        out_shape=jax.ShapeDtypeStruct((bsz, seq, D_MODEL), F32),
        grid=(bsz, n_tiles),
        in_specs=in_specs,
        out_specs=pl.BlockSpec((1, TILE, D_MODEL), lambda b, i: (b, i, 0)),
        scratch_shapes=[
            pltpu.VMEM((ext, D_MODEL), BF16),
            pltpu.VMEM((2, FF_CHUNK // LANES, ROW_PITCH * ext, LANES), F32),
            pltpu.VMEM((2, FF_CHUNK // LANES, ROW_PITCH * ext, LANES), F32),
            pltpu.VMEM((2, TILE, FF_CHUNK), BF16),
            pltpu.VMEM((TILE, D_MODEL), F32),
        ],
        compiler_params=pltpu.CompilerParams(
            dimension_semantics=("arbitrary", "arbitrary"), vmem_limit_bytes=VMEM_LIMIT_BYTES),
        name="ffn",
    )(h, h, h, w_up, conv_w, conv_b, w_down, g, b)


def kernel(x, ln_in_g, ln_in_b, w_in, b_gates, attn_sink, rel_bias, conv_w, conv_b, w_att_branch,
           w_conv_branch, w_o, ln_mix_g, ln_mix_b, w_ffn_up, ffn_conv_w, ffn_conv_b, w_ffn_down,
           ln_ffn_g, ln_ffn_b):
    assert w_in.shape[0] == DEPTH == 1
    row = lambda v: v.reshape(1, -1)
    bias_tbl = _bias_table(rel_bias)
    h, w_up, w_down = _mixer(x, row(ln_in_g), row(ln_in_b), w_in[0], row(b_gates[0]), attn_sink[0],
                             bias_tbl, conv_w[0], row(conv_b[0]), w_att_branch[0],
                             w_conv_branch[0], w_o[0], row(ln_mix_g[0]), row(ln_mix_b[0]),
                             w_ffn_up[0], w_ffn_down[0])
    return _ffn(h, w_up, ffn_conv_w[0], row(ffn_conv_b[0]), w_down, row(ln_ffn_g[0]), row(ln_ffn_b[0]))
```

```python
import functools
import math

import jax
import jax.numpy as jnp
import numpy as np
from jax import lax
from jax.experimental import pallas as pl
from jax.experimental.pallas import tpu as pltpu

D_MODEL = 1024
HEAD_DIM = 64
N_Q_HEADS = 8
N_KV_HEADS = 2
GROUP = N_Q_HEADS // N_KV_HEADS
ATT_WIDTH = N_Q_HEADS * HEAD_DIM
KV_WIDTH = N_KV_HEADS * HEAD_DIM
WINDOW = 128
BLOCK = 128
CONV_WIDTH = D_MODEL // 2
D_FF = 2816
N_BUCKETS = 32
MAX_DISTANCE = 128
LN_EPS = 1e-5
DEPTH = 1
DEEPNORM_ALPHA = (2 * DEPTH) ** 0.25
MASK_VALUE = -1e30

Q_OFF = 0
K_OFF = ATT_WIDTH
V_OFF = K_OFF + KV_WIDTH
CB_OFF = V_OFF + KV_WIDTH
CC_OFF = CB_OFF + CONV_WIDTH
CX_OFF = CC_OFF + CONV_WIDTH
GATE_OFF = CX_OFF + CONV_WIDTH
IN_PROJ_WIDTH = GATE_OFF + 2 * D_MODEL

TILE = 512
FFN_TILE = 1024
QB_PER_TILE = TILE // BLOCK
CONV_HALO = 16
FF_CHUNK = 256
N_FF_CHUNKS = D_FF // FF_CHUNK
FF_ROWS = 32
PROJ_COLS = 256
MIX_ROWS = 32
STAGE_ROWS = 256
STAGE_ROWS_WIDE = 128
LN_ROWS = 32
SM_ROWS = 32
LANES = 128
BF16_SUBLANES = 16
ROW_PITCH = 2
VMEM_LIMIT_BYTES = 60 * 1024 * 1024

F32 = jnp.float32
BF16 = jnp.bfloat16


def _layer_norm(x, g, b):
    mu = jnp.mean(x, axis=-1, keepdims=True)
    xc = x - mu
    var = jnp.mean(xc * xc, axis=-1, keepdims=True)
    return xc * lax.rsqrt(var + LN_EPS) * g + b


def _sigmoid(x):
    return 0.5 * jnp.tanh(0.5 * x) + 0.5


def _dot(a, b):
    return jnp.dot(a, b, preferred_element_type=F32)


def _dot_nt(a, b):
    return lax.dot_general(a, b, (((1,), (1,)), ((), ())), preferred_element_type=F32)


def _bias_by_rel(rel_bias):
    rel = jnp.arange(4 * BLOCK) - (2 * BLOCK - 1)
    half = N_BUCKETS // 2
    max_exact = half // 2
    offset = jnp.where(rel > 0, half, 0)
    n = jnp.abs(rel)
    nf = jnp.maximum(n, 1).astype(jnp.float32)
    large = max_exact + (jnp.log(nf / max_exact) / math.log(MAX_DISTANCE / max_exact)
                         * (half - max_exact)).astype(jnp.int32)
    large = jnp.minimum(large, half - 1)
    bucket = offset + jnp.where(n < max_exact, n, large)
    return jnp.transpose(rel_bias.astype(F32)[bucket], (1, 0))


def _bias_table_kernel(bias_ref, out_ref):
    shape = (BLOCK, 3 * BLOCK)
    col = lax.broadcasted_iota(jnp.int32, shape, 1)
    rel = col - BLOCK - lax.broadcasted_iota(jnp.int32, shape, 0)
    in_window = jnp.abs(rel) <= WINDOW
    for h in range(N_Q_HEADS):
        by_rel = jnp.broadcast_to(bias_ref[h:h + 1, :], (BLOCK, 4 * BLOCK))
        band = pltpu.roll(by_rel, 3 * BLOCK + 1, axis=1, stride=1, stride_axis=0)[:, :3 * BLOCK]
        t = jnp.where(in_window, band, MASK_VALUE)
        out_ref[0, h] = t
        out_ref[1, h] = jnp.where(col < BLOCK, MASK_VALUE, t)
        out_ref[2, h] = jnp.where(col >= 2 * BLOCK, MASK_VALUE, t)


def _bias_table(rel_bias):
    return pl.pallas_call(
        _bias_table_kernel,
        out_shape=jax.ShapeDtypeStruct((3, N_Q_HEADS, BLOCK, 3 * BLOCK), F32),
        in_specs=[pl.BlockSpec(memory_space=pltpu.VMEM)],
        out_specs=pl.BlockSpec(memory_space=pltpu.VMEM),
        name="bias_table",
    )(_bias_by_rel(rel_bias))


def _stage_weight(src_hbm, dst, stage, sem):
    chunk = stage.shape[1]
    n_chunks = src_hbm.shape[0] // chunk
    assert n_chunks * chunk == src_hbm.shape[0] and stage.shape[2] == src_hbm.shape[1]

    def copy(k):
        return pltpu.make_async_copy(src_hbm.at[pl.ds(k * chunk, chunk)], stage.at[k % 2], sem.at[k % 2])

    copy(0).start()
    for k in range(n_chunks):
        if k + 1 < n_chunks:
            copy(k + 1).start()
        copy(k).wait()
        dst[k * chunk:(k + 1) * chunk] = stage[k % 2].astype(dst.dtype)


def _mixer_kernel(xc_ref, xn_ref, lng_ref, lnb_ref, win_hbm, bg_ref, sink_ref, bias_ref,
                  cw_ref, cbias_ref, wa_hbm, wc_hbm, wo_hbm, g2_ref, b2_ref, wup_in, wdn_in,
                  out_ref, wup_out, wdn_out,
                  hext, hres, q_scr, kvar, vvar, att_scr, u_scr, conv_scr, g_scr, merged_scr,
                  win_ref, wa_ref, wc_ref, wo_ref, stage_in, stage_sq, dma_sem):
    i = pl.program_id(1)
    last_i = pl.num_programs(1) - 1

    @pl.when((pl.program_id(0) == 0) & (i == 0))
    def _():
        _stage_weight(win_hbm, win_ref, stage_in, dma_sem)
        _stage_weight(wa_hbm, wa_ref, stage_sq, dma_sem)
        _stage_weight(wc_hbm, wc_ref, stage_sq, dma_sem)
        _stage_weight(wo_hbm, wo_ref, stage_sq, dma_sem)
        kvar[:, TILE:TILE + BLOCK] = jnp.zeros((4, BLOCK, KV_WIDTH), BF16)
        vvar[:, TILE:TILE + BLOCK] = jnp.zeros((4, BLOCK, KV_WIDTH), BF16)
        hext[TILE + BLOCK - CONV_HALO:TILE + BLOCK] = jnp.zeros((CONV_HALO, D_MODEL), BF16)

    wup_out[...] = wup_in[...].astype(BF16)
    wdn_out[...] = wdn_in[...].astype(BF16)

    kvar[:, 0:BLOCK] = kvar[:, TILE:TILE + BLOCK]
    vvar[:, 0:BLOCK] = vvar[:, TILE:TILE + BLOCK]
    hext[BLOCK - CONV_HALO:BLOCK] = hext[TILE + BLOCK - CONV_HALO:TILE + BLOCK]

    lng = lng_ref[...]
    lnb = lnb_ref[...]
    ext = TILE + 2 * BLOCK

    def norm_rows(lo, hi):
        assert lo >= BLOCK
        for r in range(lo, hi, LN_ROWS):
            if r < BLOCK + TILE:
                x = xc_ref[0, r - BLOCK:r - BLOCK + LN_ROWS]
            else:
                x = xn_ref[0, r - BLOCK - TILE:r - BLOCK - TILE + LN_ROWS]
            y = _layer_norm(x, lng, lnb)
            if BLOCK <= r < BLOCK + TILE:
                hres[r - BLOCK:r - BLOCK + LN_ROWS] = y
            hext[r:r + LN_ROWS] = y.astype(BF16)

    def project_qkv(lo, hi):
        rows = slice(lo, hi)
        kv = _dot(hext[rows], win_ref[:, K_OFF:K_OFF + 2 * KV_WIDTH])
        low = lax.broadcasted_iota(jnp.int32, (hi - lo, KV_WIDTH), 1) < HEAD_DIM
        for src, dst in ((kv[:, :KV_WIDTH], kvar), (kv[:, KV_WIDTH:], vvar)):
            rolled = pltpu.roll(src, HEAD_DIM, axis=1)
            dst[0, rows] = jnp.where(low, src, 0.0).astype(BF16)
            dst[1, rows] = jnp.where(low, 0.0, rolled).astype(BF16)
            dst[2, rows] = jnp.where(low, rolled, 0.0).astype(BF16)
            dst[3, rows] = jnp.where(low, 0.0, src).astype(BF16)
        q_lo, q_hi = max(lo, BLOCK), min(hi, BLOCK + TILE)
        if q_hi > q_lo:
            q_scr[q_lo - BLOCK:q_hi - BLOCK] = (
                _dot(hext[q_lo:q_hi], win_ref[:, Q_OFF:Q_OFF + ATT_WIDTH]) * (HEAD_DIM ** -0.5)
            ).astype(BF16)

    for lo_rows, hi_rows in ((BLOCK, ext // 2), (ext // 2, BLOCK + TILE), (BLOCK + TILE, ext)):
        norm_rows(lo_rows, hi_rows)
        project_qkv(lo_rows, hi_rows)
    hcb = hext[BLOCK:BLOCK + TILE]

    lane_low = lax.broadcasted_iota(jnp.int32, (BLOCK, 2 * HEAD_DIM), 1) < HEAD_DIM

    def att_scores(qb, pair):
        rows = slice(qb * BLOCK, qb * BLOCK + 3 * BLOCK)
        kvh = pair // (GROUP // 2)
        q2 = q_scr[qb * BLOCK:(qb + 1) * BLOCK, pair * 2 * HEAD_DIM:(pair + 1) * 2 * HEAD_DIM]
        return _dot_nt(q2, jnp.concatenate([kvar[2 * kvh, rows], kvar[2 * kvh + 1, rows]], axis=0))

    def att_softmax(qb, pair, scores):
        if qb == 0:
            edge = jnp.where(i == 0, 1, 0)
        elif qb == QB_PER_TILE - 1:
            edge = jnp.where(i == last_i, 2, 0)
        else:
            edge = 0
        probs, inv = [], []
        for sub in range(2):
            h = 2 * pair + sub
            sink = sink_ref[h]
            p_blocks, inv_blocks = [], []
            for r in range(0, BLOCK, SM_ROWS):
                logits = (scores[r:r + SM_ROWS, sub * 3 * BLOCK:(sub + 1) * 3 * BLOCK]
                          + bias_ref[edge, h, r:r + SM_ROWS, :])
                m = jnp.maximum(jnp.max(logits, axis=-1, keepdims=True), sink)
                p = jnp.exp(logits - m)
                denom = jnp.sum(p, axis=-1, keepdims=True) + jnp.exp(sink - m)
                p_blocks.append(p.astype(BF16))
                inv_blocks.append(1.0 / denom)
            probs.append(jnp.concatenate(p_blocks, axis=0))
            inv.append(jnp.concatenate(inv_blocks, axis=0))
        return jnp.concatenate(probs, axis=1), jnp.where(lane_low, inv[0], inv[1])

    def att_values(qb, pair, probs, scale):
        rows = slice(qb * BLOCK, qb * BLOCK + 3 * BLOCK)
        kvh = pair // (GROUP // 2)
        o = _dot(probs, jnp.concatenate([vvar[2 * kvh, rows], vvar[2 * kvh + 1, rows]], axis=0))
        att_scr[qb * BLOCK:(qb + 1) * BLOCK, pair * 2 * HEAD_DIM:(pair + 1) * 2 * HEAD_DIM] = (
            o * scale).astype(BF16)

    lo = BLOCK - CONV_HALO
    hi = BLOCK + TILE + CONV_HALO
    slabs = PROJ_COLS // LANES

    def conv_input(j):
        cc = _dot(hext[lo:hi], win_ref[:, CC_OFF + j * PROJ_COLS:CC_OFF + (j + 1) * PROJ_COLS])
        cx = _dot(hext[lo:hi], win_ref[:, CX_OFF + j * PROJ_COLS:CX_OFF + (j + 1) * PROJ_COLS])
        row = lax.broadcasted_iota(jnp.int32, (TILE + 2 * CONV_HALO, 1), 0)
        inside = ((row >= CONV_HALO) | (i > 0)) & ((row < CONV_HALO + TILE) | (i < last_i))
        u = jnp.where(inside, cc * cx, 0.0)
        for s in range(slabs):
            u_scr[j * slabs + s, pl.ds(0, TILE + 2 * CONV_HALO, stride=ROW_PITCH), :] = (
                u[:, s * LANES:(s + 1) * LANES])

    def conv_branch(j):
        cb = _dot(hcb, win_ref[:, CB_OFF + j * PROJ_COLS:CB_OFF + (j + 1) * PROJ_COLS])
        taps = []
        for s in range(slabs):
            lanes = slice(j * PROJ_COLS + s * LANES, j * PROJ_COLS + (s + 1) * LANES)
            w = cw_ref[:, lanes]
            taps.append([jnp.broadcast_to(w[k:k + 1], (MIX_ROWS, LANES)) for k in range(3)]
                        + [jnp.broadcast_to(cbias_ref[:, lanes], (MIX_ROWS, LANES))])
        for rb in range(TILE // MIX_ROWS):
            r0 = CONV_HALO + rb * MIX_ROWS
            parts = []
            for s in range(slabs):
                ld = lambda r: u_scr[j * slabs + s, pl.ds(ROW_PITCH * r, MIX_ROWS, stride=ROW_PITCH), :]
                t = taps[s]
                dw = ld(r0 - 1) * t[0] + ld(r0) * t[1] + ld(r0 + 1) * t[2] + t[3]
                parts.append(cb[rb * MIX_ROWS:(rb + 1) * MIX_ROWS, s * LANES:(s + 1) * LANES] * dw)
            conv_scr[j, rb * MIX_ROWS:(rb + 1) * MIX_ROWS] = jnp.concatenate(parts, axis=1).astype(BF16)

    def gate(n):
        cols = slice(n * PROJ_COLS, (n + 1) * PROJ_COLS)
        pre = _dot(hcb, win_ref[:, GATE_OFF + n * PROJ_COLS:GATE_OFF + (n + 1) * PROJ_COLS])
        bias = bg_ref[:, cols]
        for rb in range(TILE // BLOCK):
            rows = slice(rb * BLOCK, (rb + 1) * BLOCK)
            g_scr[n, rows] = _sigmoid(pre[rows] + bias)

    n_conv = CONV_WIDTH // PROJ_COLS
    n_gate = 2 * D_MODEL // PROJ_COLS
    proj_units = ([functools.partial(conv_input, j) for j in range(n_conv)]
                  + [functools.partial(conv_branch, j) for j in range(n_conv)]
                  + [functools.partial(gate, n) for n in range(n_gate)])
    n_out = D_MODEL // PROJ_COLS
    half_rows = TILE // 2

    def merge(half, n):
        rows = slice(half * half_rows, (half + 1) * half_rows)
        cols = slice(n * PROJ_COLS, (n + 1) * PROJ_COLS)
        conv = jnp.concatenate([conv_scr[j, rows] for j in range(n_conv)], axis=1)
        ya = _dot(att_scr[rows], wa_ref[:, cols])
        yc = _dot(conv, wc_ref[:, cols])
        merged_scr[n, rows] = (g_scr[n, rows] * ya + g_scr[n_out + n, rows] * yc).astype(BF16)

    def out_proj(lo, hi):
        rows = slice(lo, hi)
        return _dot(jnp.concatenate([merged_scr[n, rows] for n in range(n_out)], axis=1), wo_ref[...])

    def finish(lo, mix, blocks):
        for blk in blocks:
            r = lo + blk * LN_ROWS
            out_ref[0, r:r + LN_ROWS] = _layer_norm(
                DEEPNORM_ALPHA * hres[r:r + LN_ROWS] + mix[blk * LN_ROWS:(blk + 1) * LN_ROWS],
                g2_ref[...], b2_ref[...])

    att_units = [(qb, pair) for qb in range(QB_PER_TILE) for pair in range(N_Q_HEADS // 2)]
    filler = proj_units + [functools.partial(merge, 0, n) for n in range(n_out)]
    assert len(filler) == len(att_units) and len(proj_units) >= len(att_units) // 2
    scores = att_scores(*att_units[0])
    for k, unit in enumerate(att_units):
        ahead = att_scores(*att_units[k + 1]) if k + 1 < len(att_units) else None
        probs, scale = att_softmax(*unit, scores)
        filler[k]()
        att_values(*unit, probs, scale)
        scores = ahead
    ln_blocks = half_rows // LN_ROWS
    mix = out_proj(0, half_rows)
    for n in range(n_out):
        finish(0, mix, range(n * ln_blocks // n_out, (n + 1) * ln_blocks // n_out))
        merge(1, n)
    quarter = half_rows // 2
    mix_a = out_proj(half_rows, half_rows + quarter)
    mix_b = out_proj(half_rows + quarter, TILE)
    finish(half_rows, mix_a, range(quarter // LN_ROWS))
    finish(half_rows + quarter, mix_b, range(quarter // LN_ROWS))


def _const_spec(shape):
    return pl.BlockSpec(shape, lambda b, i: (0,) * len(shape), pipeline_mode=pl.Buffered(1))


def _mixer(x, ln_g, ln_b, w_in, b_gates, sink, bias_tbl, conv_w, conv_b, w_a, w_c, w_o, g2, b2,
           w_ffn_up, w_ffn_down):
    bsz, seq, _ = x.shape
    n_tiles = seq // TILE
    blocks_per_tile = TILE // BLOCK
    n_blocks = seq // BLOCK
    in_specs = [
        pl.BlockSpec((1, TILE, D_MODEL), lambda b, i: (b, i, 0)),
        pl.BlockSpec((1, BLOCK, D_MODEL),
                     lambda b, i: (b, jnp.minimum((i + 1) * blocks_per_tile, n_blocks - 1), 0)),
        _const_spec((1, D_MODEL)), _const_spec((1, D_MODEL)),
        pl.BlockSpec(memory_space=pl.ANY),
        _const_spec((1, 2 * D_MODEL)),
        pl.BlockSpec(memory_space=pltpu.SMEM),
        _const_spec((3, N_Q_HEADS, BLOCK, 3 * BLOCK)),
        _const_spec((3, CONV_WIDTH)), _const_spec((1, CONV_WIDTH)),
        pl.BlockSpec(memory_space=pl.ANY), pl.BlockSpec(memory_space=pl.ANY),
        pl.BlockSpec(memory_space=pl.ANY),
        _const_spec((1, D_MODEL)), _const_spec((1, D_MODEL)),
    ]
    n_steps = bsz * n_tiles
    cast_specs, cast_shapes = [], []
    for w in (w_ffn_up, w_ffn_down):
        rows = next(r for r in range(BF16_SUBLANES, w.shape[0] + 1, BF16_SUBLANES)
                    if w.shape[0] % r == 0 and w.shape[0] // r <= n_steps)
        last = w.shape[0] // rows - 1
        cast_specs.append(pl.BlockSpec(
            (rows, w.shape[1]), lambda b, i, last=last: (jnp.minimum(b * n_tiles + i, last), 0)))
        cast_shapes.append(jax.ShapeDtypeStruct(w.shape, BF16))
    ext = TILE + 2 * BLOCK
    return pl.pallas_call(
        _mixer_kernel,
        out_shape=[jax.ShapeDtypeStruct((bsz, seq, D_MODEL), F32)] + cast_shapes,
        grid=(bsz, n_tiles),
        in_specs=in_specs + cast_specs,
        out_specs=[pl.BlockSpec((1, TILE, D_MODEL), lambda b, i: (b, i, 0))] + cast_specs,
        scratch_shapes=[
            pltpu.VMEM((ext, D_MODEL), BF16),
            pltpu.VMEM((TILE, D_MODEL), F32),
            pltpu.VMEM((TILE, ATT_WIDTH), BF16),
            pltpu.VMEM((4, ext, KV_WIDTH), BF16),
            pltpu.VMEM((4, ext, KV_WIDTH), BF16),
            pltpu.VMEM((TILE, ATT_WIDTH), BF16),
            pltpu.VMEM((CONV_WIDTH // LANES, ROW_PITCH * (TILE + 2 * CONV_HALO), LANES), F32),
            pltpu.VMEM((CONV_WIDTH // PROJ_COLS, TILE, PROJ_COLS), BF16),
            pltpu.VMEM((2 * D_MODEL // PROJ_COLS, TILE, PROJ_COLS), F32),
            pltpu.VMEM((D_MODEL // PROJ_COLS, TILE, PROJ_COLS), BF16),
            pltpu.VMEM((D_MODEL, IN_PROJ_WIDTH), BF16),
            pltpu.VMEM((ATT_WIDTH, D_MODEL), BF16),
            pltpu.VMEM((CONV_WIDTH, D_MODEL), BF16),
            pltpu.VMEM((D_MODEL, D_MODEL), BF16),
            pltpu.VMEM((2, STAGE_ROWS_WIDE, IN_PROJ_WIDTH), F32),
            pltpu.VMEM((2, STAGE_ROWS, D_MODEL), F32),
            pltpu.SemaphoreType.DMA((2,)),
        ],
        compiler_params=pltpu.CompilerParams(
            dimension_semantics=("arbitrary", "arbitrary"), vmem_limit_bytes=VMEM_LIMIT_BYTES),
        name="mixer",
    )(x, x, ln_g, ln_b, w_in, b_gates, sink, bias_tbl, conv_w, conv_b, w_a, w_c, w_o, g2, b2,
      w_ffn_up, w_ffn_down)


def _ffn_kernel(hp_ref, hc_ref, hn_ref, wup_ref, cw_ref, cb_ref, wdn_ref, g_ref, b_ref, out_ref,
                hext, a_scr, u_scr, act_scr, acc):
    i = pl.program_id(1)
    last_i = pl.num_programs(1) - 1

    hext[0:CONV_HALO] = jnp.where(i > 0, hp_ref[0], 0.0).astype(BF16)
    hext[CONV_HALO:CONV_HALO + FFN_TILE] = hc_ref[0].astype(BF16)
    hext[CONV_HALO + FFN_TILE:] = jnp.where(i < last_i, hn_ref[0], 0.0).astype(BF16)

    tile_half = FFN_TILE // 2
    ext_split = (0, tile_half + 2 * CONV_HALO, FFN_TILE + 2 * CONV_HALO)

    def up_proj(c, half):
        col = c * FF_CHUNK
        start, stop = ext_split[half], ext_split[half + 1]
        dst = pl.ds(ROW_PITCH * start, stop - start, stride=ROW_PITCH)
        for scr, off in ((a_scr, col), (u_scr, D_FF + col)):
            res = _dot(hext[start:stop], wup_ref[:, off:off + FF_CHUNK])
            for slab in range(FF_CHUNK // LANES):
                scr[c % 2, slab, dst, :] = res[:, slab * LANES:(slab + 1) * LANES]

    def down_proj(c, half):
        rows = slice(half * tile_half, (half + 1) * tile_half)
        part = _dot(act_scr[c % 2, rows], wdn_ref[c * FF_CHUNK:(c + 1) * FF_CHUNK, :])
        if c == 0:
            acc[rows] = part
        else:
            acc[rows] += part

    def finish(half):
        rows = slice(half * tile_half, (half + 1) * tile_half)
        out_ref[0, rows] = _layer_norm(DEEPNORM_ALPHA * hc_ref[0, rows] + acc[rows], g_ref[...], b_ref[...])

    def conv_taps(col):
        out = []
        for slab in range(FF_CHUNK // LANES):
            lanes = slice(col + slab * LANES, col + (slab + 1) * LANES)
            w = cw_ref[:, lanes]
            out.append([jnp.broadcast_to(w[k:k + 1], (FF_ROWS, LANES)) for k in range(3)]
                       + [jnp.broadcast_to(cb_ref[:, lanes], (FF_ROWS, LANES))])
        return out

    def conv3(scr, slot, slab, r0, taps):
        rows = lambda r: scr[slot, slab, pl.ds(ROW_PITCH * r, FF_ROWS, stride=ROW_PITCH), :]
        return rows(r0 - 1) * taps[0] + rows(r0) * taps[1] + rows(r0 + 1) * taps[2] + taps[3]

    def gate_rows(c, rb, a_taps, u_taps):
        r0 = CONV_HALO + rb * FF_ROWS
        parts = []
        for slab in range(FF_CHUNK // LANES):
            a = conv3(a_scr, c % 2, slab, r0, a_taps[slab])
            u = conv3(u_scr, c % 2, slab, r0, u_taps[slab])
            parts.append(a * _sigmoid(a) * u)
        act_scr[c % 2, rb * FF_ROWS:(rb + 1) * FF_ROWS] = jnp.concatenate(parts, axis=1).astype(BF16)

    units = [(c, half) for c in range(N_FF_CHUNKS) for half in range(2)]
    blocks_per_half = tile_half // FF_ROWS
    up_proj(*units[0])
    taps = {}
    for k, (c, half) in enumerate(units):
        if half == 0:
            taps = {"a": conv_taps(c * FF_CHUNK), "u": conv_taps(D_FF + c * FF_CHUNK)}
        mxu_units = []
        if k + 1 < len(units):
            mxu_units.append(functools.partial(up_proj, *units[k + 1]))
        if k >= 1:
            mxu_units.append(functools.partial(down_proj, *units[k - 1]))
        per_unit = blocks_per_half // len(mxu_units)
        for m, unit in enumerate(mxu_units):
            unit()
            for rb in range(m * per_unit, (m + 1) * per_unit):
                gate_rows(c, half * blocks_per_half + rb, taps["a"], taps["u"])
    finish(0)
    down_proj(*units[-1])
    finish(1)


def _ffn(h, w_up, conv_w, conv_b, w_down, g, b):
    bsz, seq, _ = h.shape
    n_tiles = seq // FFN_TILE
    halo_per_tile = FFN_TILE // CONV_HALO
    n_halo_blocks = seq // CONV_HALO
    in_specs = [
        pl.BlockSpec((1, CONV_HALO, D_MODEL), lambda b, i: (b, jnp.maximum(i * halo_per_tile - 1, 0), 0)),
        pl.BlockSpec((1, FFN_TILE, D_MODEL), lambda b, i: (b, i, 0)),
        pl.BlockSpec((1, CONV_HALO, D_MODEL),
                     lambda b, i: (b, jnp.minimum((i + 1) * halo_per_tile, n_halo_blocks - 1), 0)),
        _const_spec((D_MODEL, 2 * D_FF)),
        _const_spec((3, 2 * D_FF)), _const_spec((1, 2 * D_FF)),
        _const_spec((D_FF, D_MODEL)),
        _const_spec((1, D_MODEL)), _const_spec((1, D_MODEL)),
    ]
    ext = FFN_TILE + 2 * CONV_HALO
    return pl.pallas_call(
        _ffn_kernel,
        out_shape=jax.ShapeDtypeStruct((bsz, seq, D_MODEL), F32),
        grid=(bsz, n_tiles),
        in_specs=in_specs,
        out_specs=pl.BlockSpec((1, FFN_TILE, D_MODEL), lambda b, i: (b, i, 0)),
        scratch_shapes=[
            pltpu.VMEM((ext, D_MODEL), BF16),
            pltpu.VMEM((2, FF_CHUNK // LANES, ROW_PITCH * ext, LANES), F32),
            pltpu.VMEM((2, FF_CHUNK // LANES, ROW_PITCH * ext, LANES), F32),
            pltpu.VMEM((2, FFN_TILE, FF_CHUNK), BF16),
            pltpu.VMEM((FFN_TILE, D_MODEL), F32),
        ],
        compiler_params=pltpu.CompilerParams(
            dimension_semantics=("arbitrary", "arbitrary"), vmem_limit_bytes=VMEM_LIMIT_BYTES),
        name="ffn",
    )(h, h, h, w_up, conv_w, conv_b, w_down, g, b)


def kernel(x, ln_in_g, ln_in_b, w_in, b_gates, attn_sink, rel_bias, conv_w, conv_b, w_att_branch,
           w_conv_branch, w_o, ln_mix_g, ln_mix_b, w_ffn_up, ffn_conv_w, ffn_conv_b, w_ffn_down,
           ln_ffn_g, ln_ffn_b):
    assert w_in.shape[0] == DEPTH == 1
    row = lambda v: v.reshape(1, -1)
    bias_tbl = _bias_table(rel_bias)
    h, w_up, w_down = _mixer(x, row(ln_in_g), row(ln_in_b), w_in[0], row(b_gates[0]), attn_sink[0],
                             bias_tbl, conv_w[0], row(conv_b[0]), w_att_branch[0],
                             w_conv_branch[0], w_o[0], row(ln_mix_g[0]), row(ln_mix_b[0]),
                             w_ffn_up[0], w_ffn_down[0])
    return _ffn(h, w_up, ffn_conv_w[0], row(ffn_conv_b[0]), w_down, row(ln_ffn_g[0]), row(ln_ffn_b[0]))
```

```python
import functools
import math

import jax
import jax.numpy as jnp
import numpy as np
from jax import lax
from jax.experimental import pallas as pl
from jax.experimental.pallas import tpu as pltpu

D_MODEL = 1024
HEAD_DIM = 64
N_Q_HEADS = 8
N_KV_HEADS = 2
GROUP = N_Q_HEADS // N_KV_HEADS
ATT_WIDTH = N_Q_HEADS * HEAD_DIM
KV_WIDTH = N_KV_HEADS * HEAD_DIM
WINDOW = 128
BLOCK = 128
CONV_WIDTH = D_MODEL // 2
D_FF = 2816
N_BUCKETS = 32
MAX_DISTANCE = 128
LN_EPS = 1e-5
DEPTH = 1
DEEPNORM_ALPHA = (2 * DEPTH) ** 0.25
MASK_VALUE = -1e30

Q_OFF = 0
K_OFF = ATT_WIDTH
V_OFF = K_OFF + KV_WIDTH
CB_OFF = V_OFF + KV_WIDTH
CC_OFF = CB_OFF + CONV_WIDTH
CX_OFF = CC_OFF + CONV_WIDTH
GATE_OFF = CX_OFF + CONV_WIDTH
IN_PROJ_WIDTH = GATE_OFF + 2 * D_MODEL

TILE = 512
FFN_TILE = 512
QB_PER_TILE = TILE // BLOCK
CONV_HALO = 16
FF_CHUNK = 256
N_FF_CHUNKS = D_FF // FF_CHUNK
FF_ROWS = 32
PROJ_COLS = 256
MIX_ROWS = 32
STAGE_ROWS = 256
STAGE_ROWS_WIDE = 128
LN_ROWS = 32
SM_ROWS = 32
LANES = 128
BF16_SUBLANES = 16
ROW_PITCH = 2
VMEM_LIMIT_BYTES = 60 * 1024 * 1024

F32 = jnp.float32
BF16 = jnp.bfloat16


def _layer_norm(x, g, b):
    mu = jnp.mean(x, axis=-1, keepdims=True)
    xc = x - mu
    var = jnp.mean(xc * xc, axis=-1, keepdims=True)
    return xc * lax.rsqrt(var + LN_EPS) * g + b


def _sigmoid(x):
    return 0.5 * jnp.tanh(0.5 * x) + 0.5


def _dot(a, b):
    return jnp.dot(a, b, preferred_element_type=F32)


def _dot_nt(a, b):
    return lax.dot_general(a, b, (((1,), (1,)), ((), ())), preferred_element_type=F32)


def _bias_by_rel(rel_bias):
    rel = jnp.arange(4 * BLOCK) - (2 * BLOCK - 1)
    half = N_BUCKETS // 2
    max_exact = half // 2
    offset = jnp.where(rel > 0, half, 0)
    n = jnp.abs(rel)
    nf = jnp.maximum(n, 1).astype(jnp.float32)
    large = max_exact + (jnp.log(nf / max_exact) / math.log(MAX_DISTANCE / max_exact)
                         * (half - max_exact)).astype(jnp.int32)
    large = jnp.minimum(large, half - 1)
    bucket = offset + jnp.where(n < max_exact, n, large)
    return jnp.transpose(rel_bias.astype(F32)[bucket], (1, 0))


def _bias_table_kernel(bias_ref, out_ref):
    shape = (BLOCK, 3 * BLOCK)
    col = lax.broadcasted_iota(jnp.int32, shape, 1)
    rel = col - BLOCK - lax.broadcasted_iota(jnp.int32, shape, 0)
    in_window = jnp.abs(rel) <= WINDOW
    for h in range(N_Q_HEADS):
        by_rel = jnp.broadcast_to(bias_ref[h:h + 1, :], (BLOCK, 4 * BLOCK))
        band = pltpu.roll(by_rel, 3 * BLOCK + 1, axis=1, stride=1, stride_axis=0)[:, :3 * BLOCK]
        t = jnp.where(in_window, band, MASK_VALUE)
        out_ref[0, h] = t
        out_ref[1, h] = jnp.where(col < BLOCK, MASK_VALUE, t)
        out_ref[2, h] = jnp.where(col >= 2 * BLOCK, MASK_VALUE, t)


def _bias_table(rel_bias):
    return pl.pallas_call(
        _bias_table_kernel,
        out_shape=jax.ShapeDtypeStruct((3, N_Q_HEADS, BLOCK, 3 * BLOCK), F32),
        in_specs=[pl.BlockSpec(memory_space=pltpu.VMEM)],
        out_specs=pl.BlockSpec(memory_space=pltpu.VMEM),
        name="bias_table",
    )(_bias_by_rel(rel_bias))


def _stage_weight(src_hbm, dst, stage, sem):
    chunk = stage.shape[1]
    n_chunks = src_hbm.shape[0] // chunk
    assert n_chunks * chunk == src_hbm.shape[0] and stage.shape[2] == src_hbm.shape[1]

    def copy(k):
        return pltpu.make_async_copy(src_hbm.at[pl.ds(k * chunk, chunk)], stage.at[k % 2], sem.at[k % 2])

    copy(0).start()
    for k in range(n_chunks):
        if k + 1 < n_chunks:
            copy(k + 1).start()
        copy(k).wait()
        dst[k * chunk:(k + 1) * chunk] = stage[k % 2].astype(dst.dtype)


def _mixer_kernel(xc_ref, xn_ref, lng_ref, lnb_ref, win_hbm, bg_ref, sink_ref, bias_ref,
                  cw_ref, cbias_ref, wa_hbm, wc_hbm, wo_hbm, g2_ref, b2_ref, wup_in, wdn_in,
                  out_ref, wup_out, wdn_out,
                  hext, hres, q_scr, kvar, vvar, att_scr, u_scr, conv_scr, g_scr, merged_scr,
                  win_ref, wa_ref, wc_ref, wo_ref, stage_in, stage_sq, dma_sem):
    i = pl.program_id(1)
    last_i = pl.num_programs(1) - 1

    @pl.when((pl.program_id(0) == 0) & (i == 0))
    def _():
        _stage_weight(win_hbm, win_ref, stage_in, dma_sem)
        _stage_weight(wa_hbm, wa_ref, stage_sq, dma_sem)
        _stage_weight(wc_hbm, wc_ref, stage_sq, dma_sem)
        _stage_weight(wo_hbm, wo_ref, stage_sq, dma_sem)
        kvar[:, TILE:TILE + BLOCK] = jnp.zeros((4, BLOCK, KV_WIDTH), BF16)
        vvar[:, TILE:TILE + BLOCK] = jnp.zeros((4, BLOCK, KV_WIDTH), BF16)
        hext[TILE + BLOCK - CONV_HALO:TILE + BLOCK] = jnp.zeros((CONV_HALO, D_MODEL), BF16)

    wup_out[...] = wup_in[...].astype(BF16)
    wdn_out[...] = wdn_in[...].astype(BF16)

    kvar[:, 0:BLOCK] = kvar[:, TILE:TILE + BLOCK]
    vvar[:, 0:BLOCK] = vvar[:, TILE:TILE + BLOCK]
    hext[BLOCK - CONV_HALO:BLOCK] = hext[TILE + BLOCK - CONV_HALO:TILE + BLOCK]

    lng = lng_ref[...]
    lnb = lnb_ref[...]
    ext = TILE + 2 * BLOCK

    def norm_rows(lo, hi):
        assert lo >= BLOCK
        for r in range(lo, hi, LN_ROWS):
            if r < BLOCK + TILE:
                x = xc_ref[0, r - BLOCK:r - BLOCK + LN_ROWS]
            else:
                x = xn_ref[0, r - BLOCK - TILE:r - BLOCK - TILE + LN_ROWS]
            y = _layer_norm(x, lng, lnb)
            if BLOCK <= r < BLOCK + TILE:
                hres[r - BLOCK:r - BLOCK + LN_ROWS] = y
            hext[r:r + LN_ROWS] = y.astype(BF16)

    def project_qkv(lo, hi):
        rows = slice(lo, hi)
        kv = _dot(hext[rows], win_ref[:, K_OFF:K_OFF + 2 * KV_WIDTH])
        low = lax.broadcasted_iota(jnp.int32, (hi - lo, KV_WIDTH), 1) < HEAD_DIM
        for src, dst in ((kv[:, :KV_WIDTH], kvar), (kv[:, KV_WIDTH:], vvar)):
            rolled = pltpu.roll(src, HEAD_DIM, axis=1)
            dst[0, rows] = jnp.where(low, src, 0.0).astype(BF16)
            dst[1, rows] = jnp.where(low, 0.0, rolled).astype(BF16)
            dst[2, rows] = jnp.where(low, rolled, 0.0).astype(BF16)
            dst[3, rows] = jnp.where(low, 0.0, src).astype(BF16)
        q_lo, q_hi = max(lo, BLOCK), min(hi, BLOCK + TILE)
        if q_hi > q_lo:
            q_scr[q_lo - BLOCK:q_hi - BLOCK] = (
                _dot(hext[q_lo:q_hi], win_ref[:, Q_OFF:Q_OFF + ATT_WIDTH]) * (HEAD_DIM ** -0.5)
            ).astype(BF16)

    for lo_rows, hi_rows in ((BLOCK, ext // 2), (ext // 2, BLOCK + TILE), (BLOCK + TILE, ext)):
        norm_rows(lo_rows, hi_rows)
        project_qkv(lo_rows, hi_rows)
    hcb = hext[BLOCK:BLOCK + TILE]

    lane_low = lax.broadcasted_iota(jnp.int32, (BLOCK, 2 * HEAD_DIM), 1) < HEAD_DIM

    def att_scores(qb, pair):
        rows = slice(qb * BLOCK, qb * BLOCK + 3 * BLOCK)
        kvh = pair // (GROUP // 2)
        q2 = q_scr[qb * BLOCK:(qb + 1) * BLOCK, pair * 2 * HEAD_DIM:(pair + 1) * 2 * HEAD_DIM]
        return _dot_nt(q2, jnp.concatenate([kvar[2 * kvh, rows], kvar[2 * kvh + 1, rows]], axis=0))

    def att_softmax(qb, pair, scores):
        if qb == 0:
            edge = jnp.where(i == 0, 1, 0)
        elif qb == QB_PER_TILE - 1:
            edge = jnp.where(i == last_i, 2, 0)
        else:
            edge = 0
        probs, inv = [], []
        for sub in range(2):
            h = 2 * pair + sub
            sink = sink_ref[h]
            p_blocks, inv_blocks = [], []
            for r in range(0, BLOCK, SM_ROWS):
                logits = (scores[r:r + SM_ROWS, sub * 3 * BLOCK:(sub + 1) * 3 * BLOCK]
                          + bias_ref[edge, h, r:r + SM_ROWS, :])
                m = jnp.maximum(jnp.max(logits, axis=-1, keepdims=True), sink)
                p = jnp.exp(logits - m)
                denom = jnp.sum(p, axis=-1, keepdims=True) + jnp.exp(sink - m)
                p_blocks.append(p.astype(BF16))
                inv_blocks.append(1.0 / denom)
            probs.append(jnp.concatenate(p_blocks, axis=0))
            inv.append(jnp.concatenate(inv_blocks, axis=0))
        return jnp.concatenate(probs, axis=1), jnp.where(lane_low, inv[0], inv[1])

    def att_values(qb, pair, probs, scale):
        rows = slice(qb * BLOCK, qb * BLOCK + 3 * BLOCK)
        kvh = pair // (GROUP // 2)
        o = _dot(probs, jnp.concatenate([vvar[2 * kvh, rows], vvar[2 * kvh + 1, rows]], axis=0))
        att_scr[qb * BLOCK:(qb + 1) * BLOCK, pair * 2 * HEAD_DIM:(pair + 1) * 2 * HEAD_DIM] = (
            o * scale).astype(BF16)

    lo = BLOCK - CONV_HALO
    hi = BLOCK + TILE + CONV_HALO
    slabs = PROJ_COLS // LANES

    def conv_input(j):
        cc = _dot(hext[lo:hi], win_ref[:, CC_OFF + j * PROJ_COLS:CC_OFF + (j + 1) * PROJ_COLS])
        cx = _dot(hext[lo:hi], win_ref[:, CX_OFF + j * PROJ_COLS:CX_OFF + (j + 1) * PROJ_COLS])
        row = lax.broadcasted_iota(jnp.int32, (TILE + 2 * CONV_HALO, 1), 0)
        inside = ((row >= CONV_HALO) | (i > 0)) & ((row < CONV_HALO + TILE) | (i < last_i))
        u = jnp.where(inside, cc * cx, 0.0)
        for s in range(slabs):
            u_scr[j * slabs + s, pl.ds(0, TILE + 2 * CONV_HALO, stride=ROW_PITCH), :] = (
                u[:, s * LANES:(s + 1) * LANES])

    def conv_branch(j):
        cb = _dot(hcb, win_ref[:, CB_OFF + j * PROJ_COLS:CB_OFF + (j + 1) * PROJ_COLS])
        taps = []
        for s in range(slabs):
            lanes = slice(j * PROJ_COLS + s * LANES, j * PROJ_COLS + (s + 1) * LANES)
            w = cw_ref[:, lanes]
            taps.append([jnp.broadcast_to(w[k:k + 1], (MIX_ROWS, LANES)) for k in range(3)]
                        + [jnp.broadcast_to(cbias_ref[:, lanes], (MIX_ROWS, LANES))])
        for rb in range(TILE // MIX_ROWS):
            r0 = CONV_HALO + rb * MIX_ROWS
            parts = []
            for s in range(slabs):
                ld = lambda r: u_scr[j * slabs + s, pl.ds(ROW_PITCH * r, MIX_ROWS, stride=ROW_PITCH), :]
                t = taps[s]
                dw = ld(r0 - 1) * t[0] + ld(r0) * t[1] + ld(r0 + 1) * t[2] + t[3]
                parts.append(cb[rb * MIX_ROWS:(rb + 1) * MIX_ROWS, s * LANES:(s + 1) * LANES] * dw)
            conv_scr[j, rb * MIX_ROWS:(rb + 1) * MIX_ROWS] = jnp.concatenate(parts, axis=1).astype(BF16)

    def gate(n):
        cols = slice(n * PROJ_COLS, (n + 1) * PROJ_COLS)
        pre = _dot(hcb, win_ref[:, GATE_OFF + n * PROJ_COLS:GATE_OFF + (n + 1) * PROJ_COLS])
        bias = bg_ref[:, cols]
        for rb in range(TILE // BLOCK):
            rows = slice(rb * BLOCK, (rb + 1) * BLOCK)
            g_scr[n, rows] = _sigmoid(pre[rows] + bias)

    n_conv = CONV_WIDTH // PROJ_COLS
    n_gate = 2 * D_MODEL // PROJ_COLS
    proj_units = ([functools.partial(conv_input, j) for j in range(n_conv)]
                  + [functools.partial(conv_branch, j) for j in range(n_conv)]
                  + [functools.partial(gate, n) for n in range(n_gate)])
    n_out = D_MODEL // PROJ_COLS
    half_rows = TILE // 2

    def merge(half, n):
        rows = slice(half * half_rows, (half + 1) * half_rows)
        cols = slice(n * PROJ_COLS, (n + 1) * PROJ_COLS)
        conv = jnp.concatenate([conv_scr[j, rows] for j in range(n_conv)], axis=1)
        ya = _dot(att_scr[rows], wa_ref[:, cols])
        yc = _dot(conv, wc_ref[:, cols])
        merged_scr[n, rows] = (g_scr[n, rows] * ya + g_scr[n_out + n, rows] * yc).astype(BF16)

    def out_proj(lo, hi):
        rows = slice(lo, hi)
        return _dot(jnp.concatenate([merged_scr[n, rows] for n in range(n_out)], axis=1), wo_ref[...])

    def finish(lo, mix, blocks):
        for blk in blocks:
            r = lo + blk * LN_ROWS
            out_ref[0, r:r + LN_ROWS] = _layer_norm(
                DEEPNORM_ALPHA * hres[r:r + LN_ROWS] + mix[blk * LN_ROWS:(blk + 1) * LN_ROWS],
                g2_ref[...], b2_ref[...])

    att_units = [(qb, pair) for qb in range(QB_PER_TILE) for pair in range(N_Q_HEADS // 2)]
    filler = proj_units + [functools.partial(merge, 0, n) for n in range(n_out)]
    assert len(filler) == len(att_units) and len(proj_units) >= len(att_units) // 2
    scores = att_scores(*att_units[0])
    for k, unit in enumerate(att_units):
        ahead = att_scores(*att_units[k + 1]) if k + 1 < len(att_units) else None
        probs, scale = att_softmax(*unit, scores)
        filler[k]()
        att_values(*unit, probs, scale)
        scores = ahead
    ln_blocks = half_rows // LN_ROWS
    mix = out_proj(0, half_rows)
    for n in range(n_out):
        finish(0, mix, range(n * ln_blocks // n_out, (n + 1) * ln_blocks // n_out))
        merge(1, n)
    quarter = half_rows // 2
    mix_a = out_proj(half_rows, half_rows + quarter)
    mix_b = out_proj(half_rows + quarter, TILE)
    finish(half_rows, mix_a, range(quarter // LN_ROWS))
    finish(half_rows + quarter, mix_b, range(quarter // LN_ROWS))


def _const_spec(shape):
    return pl.BlockSpec(shape, lambda b, i: (0,) * len(shape), pipeline_mode=pl.Buffered(1))


def _mixer(x, ln_g, ln_b, w_in, b_gates, sink, bias_tbl, conv_w, conv_b, w_a, w_c, w_o, g2, b2,
           w_ffn_up, w_ffn_down):
    bsz, seq, _ = x.shape
    n_tiles = seq // TILE
    blocks_per_tile = TILE // BLOCK
    n_blocks = seq // BLOCK
    in_specs = [
        pl.BlockSpec((1, TILE, D_MODEL), lambda b, i: (b, i, 0)),
        pl.BlockSpec((1, BLOCK, D_MODEL),
                     lambda b, i: (b, jnp.minimum((i + 1) * blocks_per_tile, n_blocks - 1), 0)),
        _const_spec((1, D_MODEL)), _const_spec((1, D_MODEL)),
        pl.BlockSpec(memory_space=pl.ANY),
        _const_spec((1, 2 * D_MODEL)),
        pl.BlockSpec(memory_space=pltpu.SMEM),
        _const_spec((3, N_Q_HEADS, BLOCK, 3 * BLOCK)),
        _const_spec((3, CONV_WIDTH)), _const_spec((1, CONV_WIDTH)),
        pl.BlockSpec(memory_space=pl.ANY), pl.BlockSpec(memory_space=pl.ANY),
        pl.BlockSpec(memory_space=pl.ANY),
        _const_spec((1, D_MODEL)), _const_spec((1, D_MODEL)),
    ]
    n_steps = bsz * n_tiles
    cast_specs, cast_shapes = [], []
    for w in (w_ffn_up, w_ffn_down):
        rows = next(r for r in range(BF16_SUBLANES, w.shape[0] + 1, BF16_SUBLANES)
                    if w.shape[0] % r == 0 and w.shape[0] // r <= n_steps)
        last = w.shape[0] // rows - 1
        cast_specs.append(pl.BlockSpec(
            (rows, w.shape[1]), lambda b, i, last=last: (jnp.minimum(b * n_tiles + i, last), 0)))
        cast_shapes.append(jax.ShapeDtypeStruct(w.shape, BF16))
    ext = TILE + 2 * BLOCK
    return pl.pallas_call(
        _mixer_kernel,
        out_shape=[jax.ShapeDtypeStruct((bsz, seq, D_MODEL), F32)] + cast_shapes,
        grid=(bsz, n_tiles),
        in_specs=in_specs + cast_specs,
        out_specs=[pl.BlockSpec((1, TILE, D_MODEL), lambda b, i: (b, i, 0))] + cast_specs,
        scratch_shapes=[
            pltpu.VMEM((ext, D_MODEL), BF16),
            pltpu.VMEM((TILE, D_MODEL), F32),
            pltpu.VMEM((TILE, ATT_WIDTH), BF16),
            pltpu.VMEM((4, ext, KV_WIDTH), BF16),
            pltpu.VMEM((4, ext, KV_WIDTH), BF16),
            pltpu.VMEM((TILE, ATT_WIDTH), BF16),
            pltpu.VMEM((CONV_WIDTH // LANES, ROW_PITCH * (TILE + 2 * CONV_HALO), LANES), F32),
            pltpu.VMEM((CONV_WIDTH // PROJ_COLS, TILE, PROJ_COLS), BF16),
            pltpu.VMEM((2 * D_MODEL // PROJ_COLS, TILE, PROJ_COLS), F32),
            pltpu.VMEM((D_MODEL // PROJ_COLS, TILE, PROJ_COLS), BF16),
            pltpu.VMEM((D_MODEL, IN_PROJ_WIDTH), BF16),
            pltpu.VMEM((ATT_WIDTH, D_MODEL), BF16),
            pltpu.VMEM((CONV_WIDTH, D_MODEL), BF16),
            pltpu.VMEM((D_MODEL, D_MODEL), BF16),
            pltpu.VMEM((2, STAGE_ROWS_WIDE, IN_PROJ_WIDTH), F32),
            pltpu.VMEM((2, STAGE_ROWS, D_MODEL), F32),
            pltpu.SemaphoreType.DMA((2,)),
        ],
        compiler_params=pltpu.CompilerParams(
            dimension_semantics=("arbitrary", "arbitrary"), vmem_limit_bytes=VMEM_LIMIT_BYTES),
        name="mixer",
    )(x, x, ln_g, ln_b, w_in, b_gates, sink, bias_tbl, conv_w, conv_b, w_a, w_c, w_o, g2, b2,
      w_ffn_up, w_ffn_down)


def _ffn_kernel(hp_ref, hc_ref, hn_ref, wup_ref, cw_ref, cb_ref, wdn_ref, g_ref, b_ref, out_ref,
                hext, a_scr, u_scr, act_scr, acc):
    i = pl.program_id(1)
    last_i = pl.num_programs(1) - 1

    hext[0:CONV_HALO] = jnp.where(i > 0, hp_ref[0], 0.0).astype(BF16)
    hext[CONV_HALO:CONV_HALO + FFN_TILE] = hc_ref[0].astype(BF16)
    hext[CONV_HALO + FFN_TILE:] = jnp.where(i < last_i, hn_ref[0], 0.0).astype(BF16)

    tile_half = FFN_TILE // 2
    ext_split = (0, tile_half + 2 * CONV_HALO, FFN_TILE + 2 * CONV_HALO)

    def up_proj(c, half):
        col = c * FF_CHUNK
        start, stop = ext_split[half], ext_split[half + 1]
        dst = pl.ds(ROW_PITCH * start, stop - start, stride=ROW_PITCH)
        for scr, off in ((a_scr, col), (u_scr, D_FF + col)):
            res = _dot(hext[start:stop], wup_ref[:, off:off + FF_CHUNK])
            for slab in range(FF_CHUNK // LANES):
                scr[c % 2, slab, dst, :] = res[:, slab * LANES:(slab + 1) * LANES]

    def has_down_proj(c):
        return c % 2 == 1 or c == N_FF_CHUNKS - 1

    def down_proj(c, half):
        rows = slice(half * tile_half, (half + 1) * tile_half)
        if c % 2 == 1:
            lhs = jnp.concatenate([act_scr[0, rows], act_scr[1, rows]], axis=1)
            part = _dot(lhs, wdn_ref[(c - 1) * FF_CHUNK:(c + 1) * FF_CHUNK, :])
        else:
            part = _dot(act_scr[c % 2, rows], wdn_ref[c * FF_CHUNK:(c + 1) * FF_CHUNK, :])
        if c == 1:
            acc[rows] = part
        else:
            acc[rows] += part

    def finish(half):
        rows = slice(half * tile_half, (half + 1) * tile_half)
        out_ref[0, rows] = _layer_norm(DEEPNORM_ALPHA * hc_ref[0, rows] + acc[rows], g_ref[...], b_ref[...])

    def conv_taps(col):
        out = []
        for slab in range(FF_CHUNK // LANES):
            lanes = slice(col + slab * LANES, col + (slab + 1) * LANES)
            w = cw_ref[:, lanes]
            out.append([jnp.broadcast_to(w[k:k + 1], (FF_ROWS, LANES)) for k in range(3)]
                       + [jnp.broadcast_to(cb_ref[:, lanes], (FF_ROWS, LANES))])
        return out

    def conv3(scr, slot, slab, r0, taps):
        rows = lambda r: scr[slot, slab, pl.ds(ROW_PITCH * r, FF_ROWS, stride=ROW_PITCH), :]
        return rows(r0 - 1) * taps[0] + rows(r0) * taps[1] + rows(r0 + 1) * taps[2] + taps[3]

    def gate_rows(c, rb, a_taps, u_taps):
        r0 = CONV_HALO + rb * FF_ROWS
        parts = []
        for slab in range(FF_CHUNK // LANES):
            a = conv3(a_scr, c % 2, slab, r0, a_taps[slab])
            u = conv3(u_scr, c % 2, slab, r0, u_taps[slab])
            parts.append(a * _sigmoid(a) * u)
        act_scr[c % 2, rb * FF_ROWS:(rb + 1) * FF_ROWS] = jnp.concatenate(parts, axis=1).astype(BF16)

    units = [(c, half) for c in range(N_FF_CHUNKS) for half in range(2)]
    blocks_per_half = tile_half // FF_ROWS
    up_proj(*units[0])
    taps = {}
    for k, (c, half) in enumerate(units):
        if half == 0:
            taps = {"a": conv_taps(c * FF_CHUNK), "u": conv_taps(D_FF + c * FF_CHUNK)}
        mxu_units = []
        if k + 1 < len(units):
            mxu_units.append(functools.partial(up_proj, *units[k + 1]))
        if k >= 1 and has_down_proj(units[k - 1][0]):
            mxu_units.append(functools.partial(down_proj, *units[k - 1]))
        per_unit = blocks_per_half // len(mxu_units)
        for m, unit in enumerate(mxu_units):
            unit()
            for rb in range(m * per_unit, (m + 1) * per_unit):
                gate_rows(c, half * blocks_per_half + rb, taps["a"], taps["u"])
    finish(0)
    down_proj(*units[-1])
    finish(1)


def _ffn(h, w_up, conv_w, conv_b, w_down, g, b):
    bsz, seq, _ = h.shape
    n_tiles = seq // FFN_TILE
    halo_per_tile = FFN_TILE // CONV_HALO
    n_halo_blocks = seq // CONV_HALO
    in_specs = [
        pl.BlockSpec((1, CONV_HALO, D_MODEL), lambda b, i: (b, jnp.maximum(i * halo_per_tile - 1, 0), 0)),
        pl.BlockSpec((1, FFN_TILE, D_MODEL), lambda b, i: (b, i, 0)),
        pl.BlockSpec((1, CONV_HALO, D_MODEL),
                     lambda b, i: (b, jnp.minimum((i + 1) * halo_per_tile, n_halo_blocks - 1), 0)),
        _const_spec((D_MODEL, 2 * D_FF)),
        _const_spec((3, 2 * D_FF)), _const_spec((1, 2 * D_FF)),
        _const_spec((D_FF, D_MODEL)),
        _const_spec((1, D_MODEL)), _const_spec((1, D_MODEL)),
    ]
    ext = FFN_TILE + 2 * CONV_HALO
    return pl.pallas_call(
        _ffn_kernel,
        out_shape=jax.ShapeDtypeStruct((bsz, seq, D_MODEL), F32),
        grid=(bsz, n_tiles),
        in_specs=in_specs,
        out_specs=pl.BlockSpec((1, FFN_TILE, D_MODEL), lambda b, i: (b, i, 0)),
        scratch_shapes=[
            pltpu.VMEM((ext, D_MODEL), BF16),
            pltpu.VMEM((2, FF_CHUNK // LANES, ROW_PITCH * ext, LANES), F32),
            pltpu.VMEM((2, FF_CHUNK // LANES, ROW_PITCH * ext, LANES), F32),
            pltpu.VMEM((2, FFN_TILE, FF_CHUNK), BF16),
            pltpu.VMEM((FFN_TILE, D_MODEL), F32),
        ],
        compiler_params=pltpu.CompilerParams(
            dimension_semantics=("arbitrary", "arbitrary"), vmem_limit_bytes=VMEM_LIMIT_BYTES),
        name="ffn",
    )(h, h, h, w_up, conv_w, conv_b, w_down, g, b)


def kernel(x, ln_in_g, ln_in_b, w_in, b_gates, attn_sink, rel_bias, conv_w, conv_b, w_att_branch,
           w_conv_branch, w_o, ln_mix_g, ln_mix_b, w_ffn_up, ffn_conv_w, ffn_conv_b, w_ffn_down,
           ln_ffn_g, ln_ffn_b):
    assert w_in.shape[0] == DEPTH == 1
    row = lambda v: v.reshape(1, -1)
    bias_tbl = _bias_table(rel_bias)
    h, w_up, w_down = _mixer(x, row(ln_in_g), row(ln_in_b), w_in[0], row(b_gates[0]), attn_sink[0],
                             bias_tbl, conv_w[0], row(conv_b[0]), w_att_branch[0],
                             w_conv_branch[0], w_o[0], row(ln_mix_g[0]), row(ln_mix_b[0]),
                             w_ffn_up[0], w_ffn_down[0])
    return _ffn(h, w_up, ffn_conv_w[0], row(ffn_conv_b[0]), w_down, row(ln_ffn_g[0]), row(ln_ffn_b[0]))
```

```python
import functools
import math

import jax
import jax.numpy as jnp
import numpy as np
from jax import lax
from jax.experimental import pallas as pl
from jax.experimental.pallas import tpu as pltpu

D_MODEL = 1024
HEAD_DIM = 64
N_Q_HEADS = 8
N_KV_HEADS = 2
GROUP = N_Q_HEADS // N_KV_HEADS
ATT_WIDTH = N_Q_HEADS * HEAD_DIM
KV_WIDTH = N_KV_HEADS * HEAD_DIM
WINDOW = 128
BLOCK = 128
CONV_WIDTH = D_MODEL // 2
D_FF = 2816
N_BUCKETS = 32
MAX_DISTANCE = 128
LN_EPS = 1e-5
DEPTH = 1
DEEPNORM_ALPHA = (2 * DEPTH) ** 0.25
MASK_VALUE = -1e30

Q_OFF = 0
K_OFF = ATT_WIDTH
V_OFF = K_OFF + KV_WIDTH
CB_OFF = V_OFF + KV_WIDTH
CC_OFF = CB_OFF + CONV_WIDTH
CX_OFF = CC_OFF + CONV_WIDTH
GATE_OFF = CX_OFF + CONV_WIDTH
IN_PROJ_WIDTH = GATE_OFF + 2 * D_MODEL

TILE = 512
QB_PER_TILE = TILE // BLOCK
CONV_HALO = 16
FFN_HALO = 8
FF_CHUNK = 256
N_FF_CHUNKS = D_FF // FF_CHUNK
FF_ROWS = 32
PROJ_COLS = 256
MIX_ROWS = 32
STAGE_ROWS = 256
STAGE_ROWS_WIDE = 128
LN_ROWS = 32
SM_ROWS = 32
LANES = 128
BF16_SUBLANES = 16
ROW_PITCH = 2
VMEM_LIMIT_BYTES = 60 * 1024 * 1024

F32 = jnp.float32
BF16 = jnp.bfloat16


def _layer_norm(x, g, b):
    mu = jnp.mean(x, axis=-1, keepdims=True)
    xc = x - mu
    var = jnp.mean(xc * xc, axis=-1, keepdims=True)
    return xc * lax.rsqrt(var + LN_EPS) * g + b


def _sigmoid(x):
    return 0.5 * jnp.tanh(0.5 * x) + 0.5


def _dot(a, b):
    return jnp.dot(a, b, preferred_element_type=F32)


def _dot_nt(a, b):
    return lax.dot_general(a, b, (((1,), (1,)), ((), ())), preferred_element_type=F32)


def _bias_by_rel(rel_bias):
    rel = jnp.arange(4 * BLOCK) - (2 * BLOCK - 1)
    half = N_BUCKETS // 2
    max_exact = half // 2
    offset = jnp.where(rel > 0, half, 0)
    n = jnp.abs(rel)
    nf = jnp.maximum(n, 1).astype(jnp.float32)
    large = max_exact + (jnp.log(nf / max_exact) / math.log(MAX_DISTANCE / max_exact)
                         * (half - max_exact)).astype(jnp.int32)
    large = jnp.minimum(large, half - 1)
    bucket = offset + jnp.where(n < max_exact, n, large)
    return jnp.transpose(rel_bias.astype(F32)[bucket], (1, 0))


def _bias_table_kernel(bias_ref, out_ref):
    shape = (BLOCK, 3 * BLOCK)
    col = lax.broadcasted_iota(jnp.int32, shape, 1)
    rel = col - BLOCK - lax.broadcasted_iota(jnp.int32, shape, 0)
    in_window = jnp.abs(rel) <= WINDOW
    for h in range(N_Q_HEADS):
        by_rel = jnp.broadcast_to(bias_ref[h:h + 1, :], (BLOCK, 4 * BLOCK))
        band = pltpu.roll(by_rel, 3 * BLOCK + 1, axis=1, stride=1, stride_axis=0)[:, :3 * BLOCK]
        t = jnp.where(in_window, band, MASK_VALUE)
        out_ref[0, h] = t
        out_ref[1, h] = jnp.where(col < BLOCK, MASK_VALUE, t)
        out_ref[2, h] = jnp.where(col >= 2 * BLOCK, MASK_VALUE, t)


def _bias_table(rel_bias):
    return pl.pallas_call(
        _bias_table_kernel,
        out_shape=jax.ShapeDtypeStruct((3, N_Q_HEADS, BLOCK, 3 * BLOCK), F32),
        in_specs=[pl.BlockSpec(memory_space=pltpu.VMEM)],
        out_specs=pl.BlockSpec(memory_space=pltpu.VMEM),
        name="bias_table",
    )(_bias_by_rel(rel_bias))


def _stage_weight(src_hbm, dst, stage, sem):
    chunk = stage.shape[1]
    n_chunks = src_hbm.shape[0] // chunk
    assert n_chunks * chunk == src_hbm.shape[0] and stage.shape[2] == src_hbm.shape[1]

    def copy(k):
        return pltpu.make_async_copy(src_hbm.at[pl.ds(k * chunk, chunk)], stage.at[k % 2], sem.at[k % 2])

    copy(0).start()
    for k in range(n_chunks):
        if k + 1 < n_chunks:
            copy(k + 1).start()
        copy(k).wait()
        dst[k * chunk:(k + 1) * chunk] = stage[k % 2].astype(dst.dtype)


def _mixer_kernel(xc_ref, xn_ref, lng_ref, lnb_ref, win_hbm, bg_ref, sink_ref, bias_ref,
                  cw_ref, cbias_ref, wa_hbm, wc_hbm, wo_hbm, g2_ref, b2_ref, wup_in, wdn_in,
                  out_ref, wup_out, wdn_out,
                  hext, hres, q_scr, kvar, vvar, att_scr, u_scr, conv_scr, g_scr, merged_scr,
                  win_ref, wa_ref, wc_ref, wo_ref, stage_in, stage_sq, dma_sem):
    i = pl.program_id(1)
    last_i = pl.num_programs(1) - 1

    @pl.when((pl.program_id(0) == 0) & (i == 0))
    def _():
        _stage_weight(win_hbm, win_ref, stage_in, dma_sem)
        _stage_weight(wa_hbm, wa_ref, stage_sq, dma_sem)
        _stage_weight(wc_hbm, wc_ref, stage_sq, dma_sem)
        _stage_weight(wo_hbm, wo_ref, stage_sq, dma_sem)
        kvar[:, TILE:TILE + BLOCK] = jnp.zeros((4, BLOCK, KV_WIDTH), BF16)
        vvar[:, TILE:TILE + BLOCK] = jnp.zeros((4, BLOCK, KV_WIDTH), BF16)
        hext[TILE + BLOCK - CONV_HALO:TILE + BLOCK] = jnp.zeros((CONV_HALO, D_MODEL), BF16)

    wup_out[...] = wup_in[...].astype(BF16)
    wdn_out[...] = wdn_in[...].astype(BF16)

    kvar[:, 0:BLOCK] = kvar[:, TILE:TILE + BLOCK]
    vvar[:, 0:BLOCK] = vvar[:, TILE:TILE + BLOCK]
    hext[BLOCK - CONV_HALO:BLOCK] = hext[TILE + BLOCK - CONV_HALO:TILE + BLOCK]

    lng = lng_ref[...]
    lnb = lnb_ref[...]
    ext = TILE + 2 * BLOCK

    def norm_rows(lo, hi):
        assert lo >= BLOCK
        for r in range(lo, hi, LN_ROWS):
            if r < BLOCK + TILE:
                x = xc_ref[0, r - BLOCK:r - BLOCK + LN_ROWS]
            else:
                x = xn_ref[0, r - BLOCK - TILE:r - BLOCK - TILE + LN_ROWS]
            y = _layer_norm(x, lng, lnb)
            if BLOCK <= r < BLOCK + TILE:
                hres[r - BLOCK:r - BLOCK + LN_ROWS] = y
            hext[r:r + LN_ROWS] = y.astype(BF16)

    def project_qkv(lo, hi):
        rows = slice(lo, hi)
        kv = _dot(hext[rows], win_ref[:, K_OFF:K_OFF + 2 * KV_WIDTH])
        low = lax.broadcasted_iota(jnp.int32, (hi - lo, KV_WIDTH), 1) < HEAD_DIM
        for src, dst in ((kv[:, :KV_WIDTH], kvar), (kv[:, KV_WIDTH:], vvar)):
            rolled = pltpu.roll(src, HEAD_DIM, axis=1)
            dst[0, rows] = jnp.where(low, src, 0.0).astype(BF16)
            dst[1, rows] = jnp.where(low, 0.0, rolled).astype(BF16)
            dst[2, rows] = jnp.where(low, rolled, 0.0).astype(BF16)
            dst[3, rows] = jnp.where(low, 0.0, src).astype(BF16)
        q_lo, q_hi = max(lo, BLOCK), min(hi, BLOCK + TILE)
        if q_hi > q_lo:
            q_scr[q_lo - BLOCK:q_hi - BLOCK] = (
                _dot(hext[q_lo:q_hi], win_ref[:, Q_OFF:Q_OFF + ATT_WIDTH]) * (HEAD_DIM ** -0.5)
            ).astype(BF16)

    for lo_rows, hi_rows in ((BLOCK, ext // 2), (ext // 2, BLOCK + TILE), (BLOCK + TILE, ext)):
        norm_rows(lo_rows, hi_rows)
        project_qkv(lo_rows, hi_rows)
    hcb = hext[BLOCK:BLOCK + TILE]

    lane_low = lax.broadcasted_iota(jnp.int32, (BLOCK, 2 * HEAD_DIM), 1) < HEAD_DIM

    def att_scores(qb, pair):
        rows = slice(qb * BLOCK, qb * BLOCK + 3 * BLOCK)
        kvh = pair // (GROUP // 2)
        q2 = q_scr[qb * BLOCK:(qb + 1) * BLOCK, pair * 2 * HEAD_DIM:(pair + 1) * 2 * HEAD_DIM]
        return _dot_nt(q2, jnp.concatenate([kvar[2 * kvh, rows], kvar[2 * kvh + 1, rows]], axis=0))

    def att_softmax(qb, pair, scores):
        if qb == 0:
            edge = jnp.where(i == 0, 1, 0)
        elif qb == QB_PER_TILE - 1:
            edge = jnp.where(i == last_i, 2, 0)
        else:
            edge = 0
        probs, inv = [], []
        for sub in range(2):
            h = 2 * pair + sub
            sink = sink_ref[h]
            p_blocks, inv_blocks = [], []
            for r in range(0, BLOCK, SM_ROWS):
                logits = (scores[r:r + SM_ROWS, sub * 3 * BLOCK:(sub + 1) * 3 * BLOCK]
                          + bias_ref[edge, h, r:r + SM_ROWS, :])
                m = jnp.maximum(jnp.max(logits, axis=-1, keepdims=True), sink)
                p = jnp.exp(logits - m)
                denom = jnp.sum(p, axis=-1, keepdims=True) + jnp.exp(sink - m)
                p_blocks.append(p.astype(BF16))
                inv_blocks.append(1.0 / denom)
            probs.append(jnp.concatenate(p_blocks, axis=0))
            inv.append(jnp.concatenate(inv_blocks, axis=0))
        return jnp.concatenate(probs, axis=1), jnp.where(lane_low, inv[0], inv[1])

    def att_values(qb, pair, probs, scale):
        rows = slice(qb * BLOCK, qb * BLOCK + 3 * BLOCK)
        kvh = pair // (GROUP // 2)
        o = _dot(probs, jnp.concatenate([vvar[2 * kvh, rows], vvar[2 * kvh + 1, rows]], axis=0))
        att_scr[qb * BLOCK:(qb + 1) * BLOCK, pair * 2 * HEAD_DIM:(pair + 1) * 2 * HEAD_DIM] = (
            o * scale).astype(BF16)

    lo = BLOCK - CONV_HALO
    hi = BLOCK + TILE + CONV_HALO
    slabs = PROJ_COLS // LANES

    def conv_input(j):
        cc = _dot(hext[lo:hi], win_ref[:, CC_OFF + j * PROJ_COLS:CC_OFF + (j + 1) * PROJ_COLS])
        cx = _dot(hext[lo:hi], win_ref[:, CX_OFF + j * PROJ_COLS:CX_OFF + (j + 1) * PROJ_COLS])
        row = lax.broadcasted_iota(jnp.int32, (TILE + 2 * CONV_HALO, 1), 0)
        inside = ((row >= CONV_HALO) | (i > 0)) & ((row < CONV_HALO + TILE) | (i < last_i))
        u = jnp.where(inside, cc * cx, 0.0)
        for s in range(slabs):
            u_scr[j * slabs + s, pl.ds(0, TILE + 2 * CONV_HALO, stride=ROW_PITCH), :] = (
                u[:, s * LANES:(s + 1) * LANES])

    def conv_branch(j):
        cb = _dot(hcb, win_ref[:, CB_OFF + j * PROJ_COLS:CB_OFF + (j + 1) * PROJ_COLS])
        taps = []
        for s in range(slabs):
            lanes = slice(j * PROJ_COLS + s * LANES, j * PROJ_COLS + (s + 1) * LANES)
            w = cw_ref[:, lanes]
            taps.append([jnp.broadcast_to(w[k:k + 1], (MIX_ROWS, LANES)) for k in range(3)]
                        + [jnp.broadcast_to(cbias_ref[:, lanes], (MIX_ROWS, LANES))])
        for rb in range(TILE // MIX_ROWS):
            r0 = CONV_HALO + rb * MIX_ROWS
            parts = []
            for s in range(slabs):
                ld = lambda r: u_scr[j * slabs + s, pl.ds(ROW_PITCH * r, MIX_ROWS, stride=ROW_PITCH), :]
                t = taps[s]
                dw = ld(r0 - 1) * t[0] + ld(r0) * t[1] + ld(r0 + 1) * t[2] + t[3]
                parts.append(cb[rb * MIX_ROWS:(rb + 1) * MIX_ROWS, s * LANES:(s + 1) * LANES] * dw)
            conv_scr[j, rb * MIX_ROWS:(rb + 1) * MIX_ROWS] = jnp.concatenate(parts, axis=1).astype(BF16)

    def gate(n):
        cols = slice(n * PROJ_COLS, (n + 1) * PROJ_COLS)
        pre = _dot(hcb, win_ref[:, GATE_OFF + n * PROJ_COLS:GATE_OFF + (n + 1) * PROJ_COLS])
        bias = bg_ref[:, cols]
        for rb in range(TILE // BLOCK):
            rows = slice(rb * BLOCK, (rb + 1) * BLOCK)
            g_scr[n, rows] = _sigmoid(pre[rows] + bias)

    n_conv = CONV_WIDTH // PROJ_COLS
    n_gate = 2 * D_MODEL // PROJ_COLS
    proj_units = ([functools.partial(conv_input, j) for j in range(n_conv)]
                  + [functools.partial(conv_branch, j) for j in range(n_conv)]
                  + [functools.partial(gate, n) for n in range(n_gate)])
    n_out = D_MODEL // PROJ_COLS
    half_rows = TILE // 2

    def merge(half, n):
        rows = slice(half * half_rows, (half + 1) * half_rows)
        cols = slice(n * PROJ_COLS, (n + 1) * PROJ_COLS)
        conv = jnp.concatenate([conv_scr[j, rows] for j in range(n_conv)], axis=1)
        ya = _dot(att_scr[rows], wa_ref[:, cols])
        yc = _dot(conv, wc_ref[:, cols])
        merged_scr[n, rows] = (g_scr[n, rows] * ya + g_scr[n_out + n, rows] * yc).astype(BF16)

    def out_proj(lo, hi):
        rows = slice(lo, hi)
        return _dot(jnp.concatenate([merged_scr[n, rows] for n in range(n_out)], axis=1), wo_ref[...])

    def finish(lo, mix, blocks):
        for blk in blocks:
            r = lo + blk * LN_ROWS
            out_ref[0, r:r + LN_ROWS] = _layer_norm(
                DEEPNORM_ALPHA * hres[r:r + LN_ROWS] + mix[blk * LN_ROWS:(blk + 1) * LN_ROWS],
                g2_ref[...], b2_ref[...])

    att_units = [(qb, pair) for qb in range(QB_PER_TILE) for pair in range(N_Q_HEADS // 2)]
    filler = proj_units + [functools.partial(merge, 0, n) for n in range(n_out)]
    assert len(filler) == len(att_units) and len(proj_units) >= len(att_units) // 2
    scores = att_scores(*att_units[0])
    for k, unit in enumerate(att_units):
        ahead = att_scores(*att_units[k + 1]) if k + 1 < len(att_units) else None
        probs, scale = att_softmax(*unit, scores)
        filler[k]()
        att_values(*unit, probs, scale)
        scores = ahead
    ln_blocks = half_rows // LN_ROWS
    mix = out_proj(0, half_rows)
    for n in range(n_out):
        finish(0, mix, range(n * ln_blocks // n_out, (n + 1) * ln_blocks // n_out))
        merge(1, n)
    quarter = half_rows // 2
    mix_a = out_proj(half_rows, half_rows + quarter)
    mix_b = out_proj(half_rows + quarter, TILE)
    finish(half_rows, mix_a, range(quarter // LN_ROWS))
    finish(half_rows + quarter, mix_b, range(quarter // LN_ROWS))


def _const_spec(shape):
    return pl.BlockSpec(shape, lambda b, i: (0,) * len(shape), pipeline_mode=pl.Buffered(1))


def _mixer(x, ln_g, ln_b, w_in, b_gates, sink, bias_tbl, conv_w, conv_b, w_a, w_c, w_o, g2, b2,
           w_ffn_up, w_ffn_down):
    bsz, seq, _ = x.shape
    n_tiles = seq // TILE
    blocks_per_tile = TILE // BLOCK
    n_blocks = seq // BLOCK
    in_specs = [
        pl.BlockSpec((1, TILE, D_MODEL), lambda b, i: (b, i, 0)),
        pl.BlockSpec((1, BLOCK, D_MODEL),
                     lambda b, i: (b, jnp.minimum((i + 1) * blocks_per_tile, n_blocks - 1), 0)),
        _const_spec((1, D_MODEL)), _const_spec((1, D_MODEL)),
        pl.BlockSpec(memory_space=pl.ANY),
        _const_spec((1, 2 * D_MODEL)),
        pl.BlockSpec(memory_space=pltpu.SMEM),
        _const_spec((3, N_Q_HEADS, BLOCK, 3 * BLOCK)),
        _const_spec((3, CONV_WIDTH)), _const_spec((1, CONV_WIDTH)),
        pl.BlockSpec(memory_space=pl.ANY), pl.BlockSpec(memory_space=pl.ANY),
        pl.BlockSpec(memory_space=pl.ANY),
        _const_spec((1, D_MODEL)), _const_spec((1, D_MODEL)),
    ]
    n_steps = bsz * n_tiles
    cast_specs, cast_shapes = [], []
    for w in (w_ffn_up, w_ffn_down):
        rows = next(r for r in range(BF16_SUBLANES, w.shape[0] + 1, BF16_SUBLANES)
                    if w.shape[0] % r == 0 and w.shape[0] // r <= n_steps)
        last = w.shape[0] // rows - 1
        cast_specs.append(pl.BlockSpec(
            (rows, w.shape[1]), lambda b, i, last=last: (jnp.minimum(b * n_tiles + i, last), 0)))
        cast_shapes.append(jax.ShapeDtypeStruct(w.shape, BF16))
    ext = TILE + 2 * BLOCK
    return pl.pallas_call(
        _mixer_kernel,
        out_shape=[jax.ShapeDtypeStruct((bsz, seq, D_MODEL), F32)] + cast_shapes,
        grid=(bsz, n_tiles),
        in_specs=in_specs + cast_specs,
        out_specs=[pl.BlockSpec((1, TILE, D_MODEL), lambda b, i: (b, i, 0))] + cast_specs,
        scratch_shapes=[
            pltpu.VMEM((ext, D_MODEL), BF16),
            pltpu.VMEM((TILE, D_MODEL), F32),
            pltpu.VMEM((TILE, ATT_WIDTH), BF16),
            pltpu.VMEM((4, ext, KV_WIDTH), BF16),
            pltpu.VMEM((4, ext, KV_WIDTH), BF16),
            pltpu.VMEM((TILE, ATT_WIDTH), BF16),
            pltpu.VMEM((CONV_WIDTH // LANES, ROW_PITCH * (TILE + 2 * CONV_HALO), LANES), F32),
            pltpu.VMEM((CONV_WIDTH // PROJ_COLS, TILE, PROJ_COLS), BF16),
            pltpu.VMEM((2 * D_MODEL // PROJ_COLS, TILE, PROJ_COLS), F32),
            pltpu.VMEM((D_MODEL // PROJ_COLS, TILE, PROJ_COLS), BF16),
            pltpu.VMEM((D_MODEL, IN_PROJ_WIDTH), BF16),
            pltpu.VMEM((ATT_WIDTH, D_MODEL), BF16),
            pltpu.VMEM((CONV_WIDTH, D_MODEL), BF16),
            pltpu.VMEM((D_MODEL, D_MODEL), BF16),
            pltpu.VMEM((2, STAGE_ROWS_WIDE, IN_PROJ_WIDTH), F32),
            pltpu.VMEM((2, STAGE_ROWS, D_MODEL), F32),
            pltpu.SemaphoreType.DMA((2,)),
        ],
        compiler_params=pltpu.CompilerParams(
            dimension_semantics=("arbitrary", "arbitrary"), vmem_limit_bytes=VMEM_LIMIT_BYTES),
        name="mixer",
    )(x, x, ln_g, ln_b, w_in, b_gates, sink, bias_tbl, conv_w, conv_b, w_a, w_c, w_o, g2, b2,
      w_ffn_up, w_ffn_down)


def _ffn_kernel(hp_ref, hc_ref, hn_ref, wup_ref, cw_ref, cb_ref, wdn_ref, g_ref, b_ref, out_ref,
                hext, a_scr, u_scr, act_scr, acc):
    i = pl.program_id(1)
    last_i = pl.num_programs(1) - 1

    hext[...] = jnp.concatenate([jnp.where(i > 0, hp_ref[0], 0.0), hc_ref[0],
                                 jnp.where(i < last_i, hn_ref[0], 0.0)], axis=0).astype(BF16)

    tile_half = TILE // 2
    ext_split = (0, tile_half + 2 * FFN_HALO, TILE + 2 * FFN_HALO)
    assert ext_split[1] % BF16_SUBLANES == 0 and ext_split[2] % BF16_SUBLANES == 0

    def up_proj(c, half):
        col = c * FF_CHUNK
        start, stop = ext_split[half], ext_split[half + 1]
        dst = pl.ds(ROW_PITCH * start, stop - start, stride=ROW_PITCH)
        for scr, off in ((a_scr, col), (u_scr, D_FF + col)):
            res = _dot(hext[start:stop], wup_ref[:, off:off + FF_CHUNK])
            for slab in range(FF_CHUNK // LANES):
                scr[c % 2, slab, dst, :] = res[:, slab * LANES:(slab + 1) * LANES]

    def down_proj(c, half):
        rows = slice(half * tile_half, (half + 1) * tile_half)
        part = _dot(act_scr[c % 2, rows], wdn_ref[c * FF_CHUNK:(c + 1) * FF_CHUNK, :])
        if c == 0:
            acc[rows] = part
        else:
            acc[rows] += part

    def finish(half):
        rows = slice(half * tile_half, (half + 1) * tile_half)
        out_ref[0, rows] = _layer_norm(DEEPNORM_ALPHA * hc_ref[0, rows] + acc[rows], g_ref[...], b_ref[...])

    def conv_taps(col):
        out = []
        for slab in range(FF_CHUNK // LANES):
            lanes = slice(col + slab * LANES, col + (slab + 1) * LANES)
            w = cw_ref[:, lanes]
            out.append([jnp.broadcast_to(w[k:k + 1], (FF_ROWS, LANES)) for k in range(3)]
                       + [jnp.broadcast_to(cb_ref[:, lanes], (FF_ROWS, LANES))])
        return out

    def conv3(scr, slot, slab, r0, taps):
        rows = lambda r: scr[slot, slab, pl.ds(ROW_PITCH * r, FF_ROWS, stride=ROW_PITCH), :]
        return rows(r0 - 1) * taps[0] + rows(r0) * taps[1] + rows(r0 + 1) * taps[2] + taps[3]

    def gate_rows(c, rb, a_taps, u_taps):
        r0 = FFN_HALO + rb * FF_ROWS
        parts = []
        for slab in range(FF_CHUNK // LANES):
            a = conv3(a_scr, c % 2, slab, r0, a_taps[slab])
            u = conv3(u_scr, c % 2, slab, r0, u_taps[slab])
            parts.append(a * _sigmoid(a) * u)
        act_scr[c % 2, rb * FF_ROWS:(rb + 1) * FF_ROWS] = jnp.concatenate(parts, axis=1).astype(BF16)

    units = [(c, half) for c in range(N_FF_CHUNKS) for half in range(2)]
    blocks_per_half = tile_half // FF_ROWS
    up_proj(*units[0])
    taps = {}
    for k, (c, half) in enumerate(units):
        if half == 0:
            taps = {"a": conv_taps(c * FF_CHUNK), "u": conv_taps(D_FF + c * FF_CHUNK)}
        mxu_units = []
        if k + 1 < len(units):
            mxu_units.append(functools.partial(up_proj, *units[k + 1]))
        if k >= 1:
            mxu_units.append(functools.partial(down_proj, *units[k - 1]))
        per_unit = blocks_per_half // len(mxu_units)
        for m, unit in enumerate(mxu_units):
            unit()
            for rb in range(m * per_unit, (m + 1) * per_unit):
                gate_rows(c, half * blocks_per_half + rb, taps["a"], taps["u"])
    finish(0)
    down_proj(*units[-1])
    finish(1)


def _ffn(h, w_up, conv_w, conv_b, w_down, g, b):
    bsz, seq, _ = h.shape
    n_tiles = seq // TILE
    halo_per_tile = TILE // FFN_HALO
    n_halo_blocks = seq // FFN_HALO
    in_specs = [
        pl.BlockSpec((1, FFN_HALO, D_MODEL), lambda b, i: (b, jnp.maximum(i * halo_per_tile - 1, 0), 0)),
        pl.BlockSpec((1, TILE, D_MODEL), lambda b, i: (b, i, 0)),
        pl.BlockSpec((1, FFN_HALO, D_MODEL),
                     lambda b, i: (b, jnp.minimum((i + 1) * halo_per_tile, n_halo_blocks - 1), 0)),
        _const_spec((D_MODEL, 2 * D_FF)),
        _const_spec((3, 2 * D_FF)), _const_spec((1, 2 * D_FF)),
        _const_spec((D_FF, D_MODEL)),
        _const_spec((1, D_MODEL)), _const_spec((1, D_MODEL)),
    ]
    ext = TILE + 2 * FFN_HALO
    return pl.pallas_call(
        _ffn_kernel,
        out_shape=jax.ShapeDtypeStruct((bsz, seq, D_MODEL), F32),
        grid=(bsz, n_tiles),
        in_specs=in_specs,
        out_specs=pl.BlockSpec((1, TILE, D_MODEL), lambda b, i: (b, i, 0)),
        scratch_shapes=[
            pltpu.VMEM((ext, D_MODEL), BF16),
            pltpu.VMEM((2, FF_CHUNK // LANES, ROW_PITCH * ext, LANES), F32),
            pltpu.VMEM((2, FF_CHUNK // LANES, ROW_PITCH * ext, LANES), F32),
            pltpu.VMEM((2, TILE, FF_CHUNK), BF16),
            pltpu.VMEM((TILE, D_MODEL), F32),
        ],
        compiler_params=pltpu.CompilerParams(
            dimension_semantics=("arbitrary", "arbitrary"), vmem_limit_bytes=VMEM_LIMIT_BYTES),
        name="ffn",
    )(h, h, h, w_up, conv_w, conv_b, w_down, g, b)


def kernel(x, ln_in_g, ln_in_b, w_in, b_gates, attn_sink, rel_bias, conv_w, conv_b, w_att_branch,
           w_conv_branch, w_o, ln_mix_g, ln_mix_b, w_ffn_up, ffn_conv_w, ffn_conv_b, w_ffn_down,
           ln_ffn_g, ln_ffn_b):
    assert w_in.shape[0] == DEPTH == 1
    row = lambda v: v.reshape(1, -1)
    bias_tbl = _bias_table(rel_bias)
    h, w_up, w_down = _mixer(x, row(ln_in_g), row(ln_in_b), w_in[0], row(b_gates[0]), attn_sink[0],
                             bias_tbl, conv_w[0], row(conv_b[0]), w_att_branch[0],
                             w_conv_branch[0], w_o[0], row(ln_mix_g[0]), row(ln_mix_b[0]),
                             w_ffn_up[0], w_ffn_down[0])
    return _ffn(h, w_up, ffn_conv_w[0], row(ffn_conv_b[0]), w_down, row(ln_ffn_g[0]), row(ln_ffn_b[0]))
```

```python
import functools
import math

import jax
import jax.numpy as jnp
import numpy as np
from jax import lax
from jax.experimental import pallas as pl
from jax.experimental.pallas import tpu as pltpu

D_MODEL = 1024
HEAD_DIM = 64
N_Q_HEADS = 8
N_KV_HEADS = 2
GROUP = N_Q_HEADS // N_KV_HEADS
ATT_WIDTH = N_Q_HEADS * HEAD_DIM
KV_WIDTH = N_KV_HEADS * HEAD_DIM
WINDOW = 128
BLOCK = 128
CONV_WIDTH = D_MODEL // 2
D_FF = 2816
N_BUCKETS = 32
MAX_DISTANCE = 128
LN_EPS = 1e-5
DEPTH = 1
DEEPNORM_ALPHA = (2 * DEPTH) ** 0.25
MASK_VALUE = -1e30

Q_OFF = 0
K_OFF = ATT_WIDTH
V_OFF = K_OFF + KV_WIDTH
CB_OFF = V_OFF + KV_WIDTH
CC_OFF = CB_OFF + CONV_WIDTH
CX_OFF = CC_OFF + CONV_WIDTH
GATE_OFF = CX_OFF + CONV_WIDTH
IN_PROJ_WIDTH = GATE_OFF + 2 * D_MODEL

TILE = 512
QB_PER_TILE = TILE // BLOCK
CONV_HALO = 16
FF_CHUNK = 256
N_FF_CHUNKS = D_FF // FF_CHUNK
FF_ROWS = 32
PROJ_COLS = 256
MIX_ROWS = 32
STAGE_ROWS = 256
STAGE_ROWS_WIDE = 128
LN_ROWS = 32
SM_ROWS = 32
LANES = 128
BF16_SUBLANES = 16
ROW_PITCH = 2
VMEM_LIMIT_BYTES = 60 * 1024 * 1024

F32 = jnp.float32
BF16 = jnp.bfloat16


def _layer_norm(x, g, b):
    mu = jnp.mean(x, axis=-1, keepdims=True)
    xc = x - mu
    var = jnp.mean(xc * xc, axis=-1, keepdims=True)
    return xc * lax.rsqrt(var + LN_EPS) * g + b


def _sigmoid(x):
    return 0.5 * jnp.tanh(0.5 * x) + 0.5


def _dot(a, b):
    return jnp.dot(a, b, preferred_element_type=F32)


def _dot_nt(a, b):
    return lax.dot_general(a, b, (((1,), (1,)), ((), ())), preferred_element_type=F32)


def _bias_by_rel(rel_bias):
    rel = jnp.arange(4 * BLOCK) - (2 * BLOCK - 1)
    half = N_BUCKETS // 2
    max_exact = half // 2
    offset = jnp.where(rel > 0, half, 0)
    n = jnp.abs(rel)
    nf = jnp.maximum(n, 1).astype(jnp.float32)
    large = max_exact + (jnp.log(nf / max_exact) / math.log(MAX_DISTANCE / max_exact)
                         * (half - max_exact)).astype(jnp.int32)
    large = jnp.minimum(large, half - 1)
    bucket = offset + jnp.where(n < max_exact, n, large)
    return jnp.transpose(rel_bias.astype(F32)[bucket], (1, 0))


def _fill_bias_table(bias_ref, out_ref):
    shape = (BLOCK, 3 * BLOCK)
    col = lax.broadcasted_iota(jnp.int32, shape, 1)
    rel = col - BLOCK - lax.broadcasted_iota(jnp.int32, shape, 0)
    in_window = jnp.abs(rel) <= WINDOW
    for h in range(N_Q_HEADS):
        by_rel = jnp.broadcast_to(bias_ref[h:h + 1, :], (BLOCK, 4 * BLOCK))
        band = pltpu.roll(by_rel, 3 * BLOCK + 1, axis=1, stride=1, stride_axis=0)[:, :3 * BLOCK]
        t = jnp.where(in_window, band, MASK_VALUE)
        out_ref[0, h] = t
        out_ref[1, h] = jnp.where(col < BLOCK, MASK_VALUE, t)
        out_ref[2, h] = jnp.where(col >= 2 * BLOCK, MASK_VALUE, t)


def _stage_weight(src_hbm, dst, stage, sem):
    chunk = stage.shape[1]
    n_chunks = src_hbm.shape[0] // chunk
    assert n_chunks * chunk == src_hbm.shape[0] and stage.shape[2] == src_hbm.shape[1]

    def copy(k):
        return pltpu.make_async_copy(src_hbm.at[pl.ds(k * chunk, chunk)], stage.at[k % 2], sem.at[k % 2])

    copy(0).start()
    for k in range(n_chunks):
        if k + 1 < n_chunks:
            copy(k + 1).start()
        copy(k).wait()
        dst[k * chunk:(k + 1) * chunk] = stage[k % 2].astype(dst.dtype)


def _mixer_kernel(xc_ref, xn_ref, lng_ref, lnb_ref, win_hbm, bg_ref, sink_ref, brel_ref,
                  cw_ref, cbias_ref, wa_hbm, wc_hbm, wo_hbm, g2_ref, b2_ref, wup_in, wdn_in,
                  out_ref, wup_out, wdn_out,
                  hext, hres, q_scr, kvar, vvar, att_scr, u_scr, conv_scr, g_scr, merged_scr,
                  win_ref, wa_ref, wc_ref, wo_ref, stage_in, stage_sq, dma_sem, bias_ref):
    i = pl.program_id(1)
    last_i = pl.num_programs(1) - 1

    @pl.when((pl.program_id(0) == 0) & (i == 0))
    def _():
        _stage_weight(win_hbm, win_ref, stage_in, dma_sem)
        _stage_weight(wa_hbm, wa_ref, stage_sq, dma_sem)
        _stage_weight(wc_hbm, wc_ref, stage_sq, dma_sem)
        _stage_weight(wo_hbm, wo_ref, stage_sq, dma_sem)
        _fill_bias_table(brel_ref, bias_ref)
        kvar[:, TILE:TILE + BLOCK] = jnp.zeros((4, BLOCK, KV_WIDTH), BF16)
        vvar[:, TILE:TILE + BLOCK] = jnp.zeros((4, BLOCK, KV_WIDTH), BF16)
        hext[TILE + BLOCK - CONV_HALO:TILE + BLOCK] = jnp.zeros((CONV_HALO, D_MODEL), BF16)

    wup_out[...] = wup_in[...].astype(BF16)
    wdn_out[...] = wdn_in[...].astype(BF16)

    same_seq = i > 0
    for scr in (kvar, vvar):
        carried = scr[:, TILE:TILE + BLOCK]
        scr[:, 0:BLOCK] = jnp.where(same_seq, carried, jnp.zeros_like(carried))
    carried = hext[TILE + BLOCK - CONV_HALO:TILE + BLOCK]
    hext[BLOCK - CONV_HALO:BLOCK] = jnp.where(same_seq, carried, jnp.zeros_like(carried))

    lng = lng_ref[...]
    lnb = lnb_ref[...]
    ext = TILE + 2 * BLOCK

    def norm_rows(lo, hi):
        assert lo >= BLOCK
        for r in range(lo, hi, LN_ROWS):
            if r < BLOCK + TILE:
                x = xc_ref[0, r - BLOCK:r - BLOCK + LN_ROWS]
            else:
                x = xn_ref[0, r - BLOCK - TILE:r - BLOCK - TILE + LN_ROWS]
            y = _layer_norm(x, lng, lnb)
            if BLOCK <= r < BLOCK + TILE:
                hres[r - BLOCK:r - BLOCK + LN_ROWS] = y
            hext[r:r + LN_ROWS] = y.astype(BF16)

    def project_qkv(lo, hi):
        rows = slice(lo, hi)
        kv = _dot(hext[rows], win_ref[:, K_OFF:K_OFF + 2 * KV_WIDTH])
        low = lax.broadcasted_iota(jnp.int32, (hi - lo, KV_WIDTH), 1) < HEAD_DIM
        for src, dst in ((kv[:, :KV_WIDTH], kvar), (kv[:, KV_WIDTH:], vvar)):
            rolled = pltpu.roll(src, HEAD_DIM, axis=1)
            dst[0, rows] = jnp.where(low, src, 0.0).astype(BF16)
            dst[1, rows] = jnp.where(low, 0.0, rolled).astype(BF16)
            dst[2, rows] = jnp.where(low, rolled, 0.0).astype(BF16)
            dst[3, rows] = jnp.where(low, 0.0, src).astype(BF16)
        q_lo, q_hi = max(lo, BLOCK), min(hi, BLOCK + TILE)
        if q_hi > q_lo:
            q_scr[q_lo - BLOCK:q_hi - BLOCK] = (
                _dot(hext[q_lo:q_hi], win_ref[:, Q_OFF:Q_OFF + ATT_WIDTH]) * (HEAD_DIM ** -0.5)
            ).astype(BF16)

    for lo_rows, hi_rows in ((BLOCK, ext // 2), (ext // 2, BLOCK + TILE), (BLOCK + TILE, ext)):
        norm_rows(lo_rows, hi_rows)
        project_qkv(lo_rows, hi_rows)
    hcb = hext[BLOCK:BLOCK + TILE]

    lane_low = lax.broadcasted_iota(jnp.int32, (BLOCK, 2 * HEAD_DIM), 1) < HEAD_DIM

    def att_scores(qb, pair):
        rows = slice(qb * BLOCK, qb * BLOCK + 3 * BLOCK)
        kvh = pair // (GROUP // 2)
        q2 = q_scr[qb * BLOCK:(qb + 1) * BLOCK, pair * 2 * HEAD_DIM:(pair + 1) * 2 * HEAD_DIM]
        return _dot_nt(q2, jnp.concatenate([kvar[2 * kvh, rows], kvar[2 * kvh + 1, rows]], axis=0))

    def att_softmax(qb, pair, scores):
        if qb == 0:
            edge = jnp.where(i == 0, 1, 0)
        elif qb == QB_PER_TILE - 1:
            edge = jnp.where(i == last_i, 2, 0)
        else:
            edge = 0
        probs, inv = [], []
        for sub in range(2):
            h = 2 * pair + sub
            sink = sink_ref[h]
            p_blocks, inv_blocks = [], []
            for r in range(0, BLOCK, SM_ROWS):
                logits = (scores[r:r + SM_ROWS, sub * 3 * BLOCK:(sub + 1) * 3 * BLOCK]
                          + bias_ref[edge, h, r:r + SM_ROWS, :])
                m = jnp.maximum(jnp.max(logits, axis=-1, keepdims=True), sink)
                p = jnp.exp(logits - m)
                denom = jnp.sum(p, axis=-1, keepdims=True) + jnp.exp(sink - m)
                p_blocks.append(p.astype(BF16))
                inv_blocks.append(1.0 / denom)
            probs.append(jnp.concatenate(p_blocks, axis=0))
            inv.append(jnp.concatenate(inv_blocks, axis=0))
        return jnp.concatenate(probs, axis=1), jnp.where(lane_low, inv[0], inv[1])

    def att_values(qb, pair, probs, scale):
        rows = slice(qb * BLOCK, qb * BLOCK + 3 * BLOCK)
        kvh = pair // (GROUP // 2)
        o = _dot(probs, jnp.concatenate([vvar[2 * kvh, rows], vvar[2 * kvh + 1, rows]], axis=0))
        att_scr[qb * BLOCK:(qb + 1) * BLOCK, pair * 2 * HEAD_DIM:(pair + 1) * 2 * HEAD_DIM] = (
            o * scale).astype(BF16)

    lo = BLOCK - CONV_HALO
    hi = BLOCK + TILE + CONV_HALO
    slabs = PROJ_COLS // LANES

    def conv_input(j):
        cc = _dot(hext[lo:hi], win_ref[:, CC_OFF + j * PROJ_COLS:CC_OFF + (j + 1) * PROJ_COLS])
        cx = _dot(hext[lo:hi], win_ref[:, CX_OFF + j * PROJ_COLS:CX_OFF + (j + 1) * PROJ_COLS])
        row = lax.broadcasted_iota(jnp.int32, (TILE + 2 * CONV_HALO, 1), 0)
        inside = ((row >= CONV_HALO) | (i > 0)) & ((row < CONV_HALO + TILE) | (i < last_i))
        u = jnp.where(inside, cc * cx, 0.0)
        for s in range(slabs):
            u_scr[j * slabs + s, pl.ds(0, TILE + 2 * CONV_HALO, stride=ROW_PITCH), :] = (
                u[:, s * LANES:(s + 1) * LANES])

    def conv_branch(j):
        cb = _dot(hcb, win_ref[:, CB_OFF + j * PROJ_COLS:CB_OFF + (j + 1) * PROJ_COLS])
        taps = []
        for s in range(slabs):
            lanes = slice(j * PROJ_COLS + s * LANES, j * PROJ_COLS + (s + 1) * LANES)
            w = cw_ref[0, :, lanes]
            taps.append([jnp.broadcast_to(w[k:k + 1], (MIX_ROWS, LANES)) for k in range(3)]
                        + [jnp.broadcast_to(cbias_ref[:, lanes], (MIX_ROWS, LANES))])
        for rb in range(TILE // MIX_ROWS):
            r0 = CONV_HALO + rb * MIX_ROWS
            parts = []
            for s in range(slabs):
                ld = lambda r: u_scr[j * slabs + s, pl.ds(ROW_PITCH * r, MIX_ROWS, stride=ROW_PITCH), :]
                t = taps[s]
                dw = ld(r0 - 1) * t[0] + ld(r0) * t[1] + ld(r0 + 1) * t[2] + t[3]
                parts.append(cb[rb * MIX_ROWS:(rb + 1) * MIX_ROWS, s * LANES:(s + 1) * LANES] * dw)
            conv_scr[j, rb * MIX_ROWS:(rb + 1) * MIX_ROWS] = jnp.concatenate(parts, axis=1).astype(BF16)

    def gate(n):
        cols = slice(n * PROJ_COLS, (n + 1) * PROJ_COLS)
        pre = _dot(hcb, win_ref[:, GATE_OFF + n * PROJ_COLS:GATE_OFF + (n + 1) * PROJ_COLS])
        bias = bg_ref[:, cols]
        for rb in range(TILE // BLOCK):
            rows = slice(rb * BLOCK, (rb + 1) * BLOCK)
            g_scr[n, rows] = _sigmoid(pre[rows] + bias)

    n_conv = CONV_WIDTH // PROJ_COLS
    n_gate = 2 * D_MODEL // PROJ_COLS
    proj_units = ([functools.partial(conv_input, j) for j in range(n_conv)]
                  + [functools.partial(conv_branch, j) for j in range(n_conv)]
                  + [functools.partial(gate, n) for n in range(n_gate)])
    n_out = D_MODEL // PROJ_COLS
    half_rows = TILE // 2

    def merge(half, n):
        rows = slice(half * half_rows, (half + 1) * half_rows)
        cols = slice(n * PROJ_COLS, (n + 1) * PROJ_COLS)
        conv = jnp.concatenate([conv_scr[j, rows] for j in range(n_conv)], axis=1)
        ya = _dot(att_scr[rows], wa_ref[:, cols])
        yc = _dot(conv, wc_ref[:, cols])
        merged_scr[n, rows] = (g_scr[n, rows] * ya + g_scr[n_out + n, rows] * yc).astype(BF16)

    def out_proj(lo, hi):
        rows = slice(lo, hi)
        return _dot(jnp.concatenate([merged_scr[n, rows] for n in range(n_out)], axis=1), wo_ref[...])

    def finish(lo, mix, blocks):
        for blk in blocks:
            r = lo + blk * LN_ROWS
            out_ref[0, r:r + LN_ROWS] = _layer_norm(
                DEEPNORM_ALPHA * hres[r:r + LN_ROWS] + mix[blk * LN_ROWS:(blk + 1) * LN_ROWS],
                g2_ref[...], b2_ref[...])

    att_units = [(qb, pair) for qb in range(QB_PER_TILE) for pair in range(N_Q_HEADS // 2)]
    filler = proj_units + [functools.partial(merge, 0, n) for n in range(n_out)]
    assert len(filler) == len(att_units) and len(proj_units) >= len(att_units) // 2
    scores = att_scores(*att_units[0])
    for k, unit in enumerate(att_units):
        ahead = att_scores(*att_units[k + 1]) if k + 1 < len(att_units) else None
        probs, scale = att_softmax(*unit, scores)
        filler[k]()
        att_values(*unit, probs, scale)
        scores = ahead
    ln_blocks = half_rows // LN_ROWS
    mix = out_proj(0, half_rows)
    for n in range(n_out):
        finish(0, mix, range(n * ln_blocks // n_out, (n + 1) * ln_blocks // n_out))
        merge(1, n)
    quarter = half_rows // 2
    mix_a = out_proj(half_rows, half_rows + quarter)
    mix_b = out_proj(half_rows + quarter, TILE)
    finish(half_rows, mix_a, range(quarter // LN_ROWS))
    finish(half_rows + quarter, mix_b, range(quarter // LN_ROWS))


def _const_spec(shape):
    return pl.BlockSpec(shape, lambda b, i: (0,) * len(shape), pipeline_mode=pl.Buffered(1))


def _mixer(x, ln_g, ln_b, w_in, b_gates, sink, bias_by_rel, conv_w, conv_b, w_a, w_c, w_o, g2, b2,
           w_ffn_up, w_ffn_down):
    bsz, seq, _ = x.shape
    n_tiles = seq // TILE
    blocks_per_tile = TILE // BLOCK
    n_blocks = seq // BLOCK
    in_specs = [
        pl.BlockSpec((1, TILE, D_MODEL), lambda b, i: (b, i, 0)),
        pl.BlockSpec((1, BLOCK, D_MODEL),
                     lambda b, i: (b, jnp.minimum((i + 1) * blocks_per_tile, n_blocks - 1), 0)),
        _const_spec((1, D_MODEL)), _const_spec((1, D_MODEL)),
        pl.BlockSpec(memory_space=pl.ANY),
        _const_spec((1, 2 * D_MODEL)),
        pl.BlockSpec(memory_space=pltpu.SMEM),
        _const_spec((N_Q_HEADS, 4 * BLOCK)),
        _const_spec((1, 3, CONV_WIDTH)), _const_spec((1, CONV_WIDTH)),
        pl.BlockSpec(memory_space=pl.ANY), pl.BlockSpec(memory_space=pl.ANY),
        pl.BlockSpec(memory_space=pl.ANY),
        _const_spec((1, D_MODEL)), _const_spec((1, D_MODEL)),
    ]
    n_steps = bsz * n_tiles
    cast_specs, cast_shapes = [], []
    for w in (w_ffn_up, w_ffn_down):
        rows = next(r for r in range(BF16_SUBLANES, w.shape[0] + 1, BF16_SUBLANES)
                    if w.shape[0] % r == 0 and w.shape[0] // r <= n_steps)
        last = w.shape[0] // rows - 1
        cast_specs.append(pl.BlockSpec(
            (rows, w.shape[1]), lambda b, i, last=last: (jnp.minimum(b * n_tiles + i, last), 0)))
        cast_shapes.append(jax.ShapeDtypeStruct(w.shape, BF16))
    ext = TILE + 2 * BLOCK
    return pl.pallas_call(
        _mixer_kernel,
        out_shape=[jax.ShapeDtypeStruct((bsz, seq, D_MODEL), F32)] + cast_shapes,
        grid=(bsz, n_tiles),
        in_specs=in_specs + cast_specs,
        out_specs=[pl.BlockSpec((1, TILE, D_MODEL), lambda b, i: (b, i, 0))] + cast_specs,
        scratch_shapes=[
            pltpu.VMEM((ext, D_MODEL), BF16),
            pltpu.VMEM((TILE, D_MODEL), F32),
            pltpu.VMEM((TILE, ATT_WIDTH), BF16),
            pltpu.VMEM((4, ext, KV_WIDTH), BF16),
            pltpu.VMEM((4, ext, KV_WIDTH), BF16),
            pltpu.VMEM((TILE, ATT_WIDTH), BF16),
            pltpu.VMEM((CONV_WIDTH // LANES, ROW_PITCH * (TILE + 2 * CONV_HALO), LANES), F32),
            pltpu.VMEM((CONV_WIDTH // PROJ_COLS, TILE, PROJ_COLS), BF16),
            pltpu.VMEM((2 * D_MODEL // PROJ_COLS, TILE, PROJ_COLS), F32),
            pltpu.VMEM((D_MODEL // PROJ_COLS, TILE, PROJ_COLS), BF16),
            pltpu.VMEM((D_MODEL, IN_PROJ_WIDTH), BF16),
            pltpu.VMEM((ATT_WIDTH, D_MODEL), BF16),
            pltpu.VMEM((CONV_WIDTH, D_MODEL), BF16),
            pltpu.VMEM((D_MODEL, D_MODEL), BF16),
            pltpu.VMEM((2, STAGE_ROWS_WIDE, IN_PROJ_WIDTH), F32),
            pltpu.VMEM((2, STAGE_ROWS, D_MODEL), F32),
            pltpu.SemaphoreType.DMA((2,)),
            pltpu.VMEM((3, N_Q_HEADS, BLOCK, 3 * BLOCK), F32),
        ],
        compiler_params=pltpu.CompilerParams(
            dimension_semantics=("arbitrary", "arbitrary"), vmem_limit_bytes=VMEM_LIMIT_BYTES),
        name="mixer",
    )(x, x, ln_g, ln_b, w_in, b_gates, sink, bias_by_rel, conv_w, conv_b, w_a, w_c, w_o, g2, b2,
      w_ffn_up, w_ffn_down)


def _ffn_kernel(hp_ref, hc_ref, hn_ref, wup_ref, cw_ref, cb_ref, wdn_ref, g_ref, b_ref, out_ref,
                hext, a_scr, u_scr, act_scr, acc):
    i = pl.program_id(1)
    last_i = pl.num_programs(1) - 1

    hext[0:CONV_HALO] = jnp.where(i > 0, hp_ref[0], 0.0).astype(BF16)
    hext[CONV_HALO:CONV_HALO + TILE] = hc_ref[0].astype(BF16)
    hext[CONV_HALO + TILE:] = jnp.where(i < last_i, hn_ref[0], 0.0).astype(BF16)

    tile_half = TILE // 2
    ext_split = (0, tile_half + 2 * CONV_HALO, TILE + 2 * CONV_HALO)

    def up_proj(c, half):
        col = c * FF_CHUNK
        start, stop = ext_split[half], ext_split[half + 1]
        dst = pl.ds(ROW_PITCH * start, stop - start, stride=ROW_PITCH)
        for scr, off in ((a_scr, col), (u_scr, D_FF + col)):
            res = _dot(hext[start:stop], wup_ref[:, off:off + FF_CHUNK])
            for slab in range(FF_CHUNK // LANES):
                scr[c % 2, slab, dst, :] = res[:, slab * LANES:(slab + 1) * LANES]

    def down_proj(c, half):
        rows = slice(half * tile_half, (half + 1) * tile_half)
        part = _dot(act_scr[c % 2, rows], wdn_ref[c * FF_CHUNK:(c + 1) * FF_CHUNK, :])
        if c == 0:
            acc[rows] = part
        else:
            acc[rows] += part

    def finish(half):
        rows = slice(half * tile_half, (half + 1) * tile_half)
        out_ref[0, rows] = _layer_norm(DEEPNORM_ALPHA * hc_ref[0, rows] + acc[rows], g_ref[...], b_ref[...])

    def conv_taps(col):
        out = []
        for slab in range(FF_CHUNK // LANES):
            lanes = slice(col + slab * LANES, col + (slab + 1) * LANES)
            w = cw_ref[0, :, lanes]
            out.append([jnp.broadcast_to(w[k:k + 1], (FF_ROWS, LANES)) for k in range(3)]
                       + [jnp.broadcast_to(cb_ref[:, lanes], (FF_ROWS, LANES))])
        return out

    def conv3(scr, slot, slab, r0, taps):
        rows = lambda r: scr[slot, slab, pl.ds(ROW_PITCH * r, FF_ROWS, stride=ROW_PITCH), :]
        return rows(r0 - 1) * taps[0] + rows(r0) * taps[1] + rows(r0 + 1) * taps[2] + taps[3]

    def gate_rows(c, rb, a_taps, u_taps):
        r0 = CONV_HALO + rb * FF_ROWS
        parts = []
        for slab in range(FF_CHUNK // LANES):
            a = conv3(a_scr, c % 2, slab, r0, a_taps[slab])
            u = conv3(u_scr, c % 2, slab, r0, u_taps[slab])
            parts.append(a * _sigmoid(a) * u)
        act_scr[c % 2, rb * FF_ROWS:(rb + 1) * FF_ROWS] = jnp.concatenate(parts, axis=1).astype(BF16)

    units = [(c, half) for c in range(N_FF_CHUNKS) for half in range(2)]
    blocks_per_half = tile_half // FF_ROWS
    up_proj(*units[0])
    taps = {}
    for k, (c, half) in enumerate(units):
        if half == 0:
            taps = {"a": conv_taps(c * FF_CHUNK), "u": conv_taps(D_FF + c * FF_CHUNK)}
        mxu_units = []
        if k + 1 < len(units):
            mxu_units.append(functools.partial(up_proj, *units[k + 1]))
        if k >= 1:
            mxu_units.append(functools.partial(down_proj, *units[k - 1]))
        per_unit = blocks_per_half // len(mxu_units)
        for m, unit in enumerate(mxu_units):
            unit()
            for rb in range(m * per_unit, (m + 1) * per_unit):
                gate_rows(c, half * blocks_per_half + rb, taps["a"], taps["u"])
    finish(0)
    down_proj(*units[-1])
    finish(1)


def _ffn(h, w_up, conv_w, conv_b, w_down, g, b):
    bsz, seq, _ = h.shape
    n_tiles = seq // TILE
    halo_per_tile = TILE // CONV_HALO
    n_halo_blocks = seq // CONV_HALO
    in_specs = [
        pl.BlockSpec((1, CONV_HALO, D_MODEL), lambda b, i: (b, jnp.maximum(i * halo_per_tile - 1, 0), 0)),
        pl.BlockSpec((1, TILE, D_MODEL), lambda b, i: (b, i, 0)),
        pl.BlockSpec((1, CONV_HALO, D_MODEL),
                     lambda b, i: (b, jnp.minimum((i + 1) * halo_per_tile, n_halo_blocks - 1), 0)),
        _const_spec((D_MODEL, 2 * D_FF)),
        _const_spec((1, 3, 2 * D_FF)), _const_spec((1, 2 * D_FF)),
        _const_spec((D_FF, D_MODEL)),
        _const_spec((1, D_MODEL)), _const_spec((1, D_MODEL)),
    ]
    ext = TILE + 2 * CONV_HALO
    return pl.pallas_call(
        _ffn_kernel,
        out_shape=jax.ShapeDtypeStruct((bsz, seq, D_MODEL), F32),
        grid=(bsz, n_tiles),
        in_specs=in_specs,
        out_specs=pl.BlockSpec((1, TILE, D_MODEL), lambda b, i: (b, i, 0)),
        scratch_shapes=[
            pltpu.VMEM((ext, D_MODEL), BF16),
            pltpu.VMEM((2, FF_CHUNK // LANES, ROW_PITCH * ext, LANES), F32),
            pltpu.VMEM((2, FF_CHUNK // LANES, ROW_PITCH * ext, LANES), F32),
            pltpu.VMEM((2, TILE, FF_CHUNK), BF16),
            pltpu.VMEM((TILE, D_MODEL), F32),
        ],
        compiler_params=pltpu.CompilerParams(
            dimension_semantics=("arbitrary", "arbitrary"), vmem_limit_bytes=VMEM_LIMIT_BYTES),
        name="ffn",
    )(h, h, h, w_up, conv_w, conv_b, w_down, g, b)


def kernel(x, ln_in_g, ln_in_b, w_in, b_gates, attn_sink, rel_bias, conv_w, conv_b, w_att_branch,
           w_conv_branch, w_o, ln_mix_g, ln_mix_b, w_ffn_up, ffn_conv_w, ffn_conv_b, w_ffn_down,
           ln_ffn_g, ln_ffn_b):
    assert w_in.shape[0] == DEPTH == 1
    row = lambda v: v.reshape(1, -1)
    h, w_up, w_down = _mixer(x, row(ln_in_g), row(ln_in_b), w_in[0], row(b_gates[0]), attn_sink[0],
                             _bias_by_rel(rel_bias), conv_w, row(conv_b[0]), w_att_branch[0],
                             w_conv_branch[0], w_o[0], row(ln_mix_g[0]), row(ln_mix_b[0]),
                             w_ffn_up[0], w_ffn_down[0])
    return _ffn(h, w_up, ffn_conv_w, row(ffn_conv_b[0]), w_down, row(ln_ffn_g[0]), row(ln_ffn_b[0]))
```

```python
import functools
import math

import jax
import jax.numpy as jnp
import numpy as np
from jax import lax
from jax.experimental import pallas as pl
from jax.experimental.pallas import tpu as pltpu

D_MODEL = 1024
HEAD_DIM = 64
N_Q_HEADS = 8
N_KV_HEADS = 2
GROUP = N_Q_HEADS // N_KV_HEADS
ATT_WIDTH = N_Q_HEADS * HEAD_DIM
KV_WIDTH = N_KV_HEADS * HEAD_DIM
WINDOW = 128
BLOCK = 128
CONV_WIDTH = D_MODEL // 2
D_FF = 2816
N_BUCKETS = 32
MAX_DISTANCE = 128
LN_EPS = 1e-5
DEPTH = 1
DEEPNORM_ALPHA = (2 * DEPTH) ** 0.25
MASK_VALUE = -1e30

Q_OFF = 0
K_OFF = ATT_WIDTH
V_OFF = K_OFF + KV_WIDTH
CB_OFF = V_OFF + KV_WIDTH
CC_OFF = CB_OFF + CONV_WIDTH
CX_OFF = CC_OFF + CONV_WIDTH
GATE_OFF = CX_OFF + CONV_WIDTH
IN_PROJ_WIDTH = GATE_OFF + 2 * D_MODEL

TILE = 512
QB_PER_TILE = TILE // BLOCK
CONV_HALO = 16
FF_CHUNK = 256
N_FF_CHUNKS = D_FF // FF_CHUNK
FF_ROWS = 32
PROJ_COLS = 256
MIX_ROWS = 32
STAGE_ROWS = 256
STAGE_ROWS_WIDE = 128
LN_ROWS = 32
SM_ROWS = 32
LANES = 128
BF16_SUBLANES = 16
ROW_PITCH = 2
VMEM_LIMIT_BYTES = 60 * 1024 * 1024

F32 = jnp.float32
BF16 = jnp.bfloat16


def _layer_norm(x, g, b):
    mu = jnp.mean(x, axis=-1, keepdims=True)
    xc = x - mu
    var = jnp.mean(xc * xc, axis=-1, keepdims=True)
    return xc * lax.rsqrt(var + LN_EPS) * g + b


def _sigmoid(x):
    return 0.5 * jnp.tanh(0.5 * x) + 0.5


def _dot(a, b):
    return jnp.dot(a, b, preferred_element_type=F32)


def _dot_nt(a, b):
    return lax.dot_general(a, b, (((1,), (1,)), ((), ())), preferred_element_type=F32)


def _bias_by_rel(rel_bias):
    rel = jnp.arange(4 * BLOCK) - (2 * BLOCK - 1)
    half = N_BUCKETS // 2
    max_exact = half // 2
    offset = jnp.where(rel > 0, half, 0)
    n = jnp.abs(rel)
    nf = jnp.maximum(n, 1).astype(jnp.float32)
    large = max_exact + (jnp.log(nf / max_exact) / math.log(MAX_DISTANCE / max_exact)
                         * (half - max_exact)).astype(jnp.int32)
    large = jnp.minimum(large, half - 1)
    bucket = offset + jnp.where(n < max_exact, n, large)
    return jnp.transpose(rel_bias.astype(F32)[bucket], (1, 0))


def _fill_bias_table(bias_ref, out_ref):
    shape = (BLOCK, 3 * BLOCK)
    col = lax.broadcasted_iota(jnp.int32, shape, 1)
    rel = col - BLOCK - lax.broadcasted_iota(jnp.int32, shape, 0)
    in_window = jnp.abs(rel) <= WINDOW
    for h in range(N_Q_HEADS):
        by_rel = jnp.broadcast_to(bias_ref[h:h + 1, :], (BLOCK, 4 * BLOCK))
        band = pltpu.roll(by_rel, 3 * BLOCK + 1, axis=1, stride=1, stride_axis=0)[:, :3 * BLOCK]
        t = jnp.where(in_window, band, MASK_VALUE)
        out_ref[0, h] = t
        out_ref[1, h] = jnp.where(col < BLOCK, MASK_VALUE, t)
        out_ref[2, h] = jnp.where(col >= 2 * BLOCK, MASK_VALUE, t)


def _stage_weight(src_hbm, dst, stage, sem):
    chunk = stage.shape[1]
    n_chunks = src_hbm.shape[0] // chunk
    assert n_chunks * chunk == src_hbm.shape[0] and stage.shape[2] == src_hbm.shape[1]

    def copy(k):
        return pltpu.make_async_copy(src_hbm.at[pl.ds(k * chunk, chunk)], stage.at[k % 2], sem.at[k % 2])

    copy(0).start()
    for k in range(n_chunks):
        if k + 1 < n_chunks:
            copy(k + 1).start()
        copy(k).wait()
        dst[k * chunk:(k + 1) * chunk] = stage[k % 2].astype(dst.dtype)


def _mixer_kernel(xc_ref, xn_ref, lng_ref, lnb_ref, win_hbm, bg_ref, sink_ref, brel_ref,
                  cw_ref, cbias_ref, wa_hbm, wc_hbm, wo_hbm, g2_ref, b2_ref, wup_in, wdn_in,
                  out_ref, wup_out, wdn_out,
                  hext, hres, q_scr, kvar, vvar, att_scr, u_scr, conv_scr, g_scr, merged_scr,
                  win_ref, wa_ref, wc_ref, wo_ref, stage_in, stage_sq, dma_sem, bias_ref):
    i = pl.program_id(1)
    last_i = pl.num_programs(1) - 1

    @pl.when((pl.program_id(0) == 0) & (i == 0))
    def _():
        _stage_weight(win_hbm, win_ref, stage_in, dma_sem)
        _stage_weight(wa_hbm, wa_ref, stage_sq, dma_sem)
        _stage_weight(wc_hbm, wc_ref, stage_sq, dma_sem)
        _stage_weight(wo_hbm, wo_ref, stage_sq, dma_sem)
        _fill_bias_table(brel_ref, bias_ref)
        kvar[:, TILE:TILE + BLOCK] = jnp.zeros((4, BLOCK, KV_WIDTH), BF16)
        vvar[:, TILE:TILE + BLOCK] = jnp.zeros((4, BLOCK, KV_WIDTH), BF16)
        hext[TILE + BLOCK - CONV_HALO:TILE + BLOCK] = jnp.zeros((CONV_HALO, D_MODEL), BF16)

    wup_out[...] = wup_in[...].astype(BF16)
    wdn_out[...] = wdn_in[...].astype(BF16)

    same_seq = i > 0
    for scr in (kvar, vvar):
        carried = scr[:, TILE:TILE + BLOCK]
        scr[:, 0:BLOCK] = jnp.where(same_seq, carried, jnp.zeros_like(carried))
    carried = hext[TILE + BLOCK - CONV_HALO:TILE + BLOCK]
    hext[BLOCK - CONV_HALO:BLOCK] = jnp.where(same_seq, carried, jnp.zeros_like(carried))

    lng = lng_ref[...]
    lnb = lnb_ref[...]
    ext = TILE + 2 * BLOCK

    def norm_rows(lo, hi):
        assert lo >= BLOCK
        for r in range(lo, hi, LN_ROWS):
            if r < BLOCK + TILE:
                x = xc_ref[0, r - BLOCK:r - BLOCK + LN_ROWS]
            else:
                x = xn_ref[0, r - BLOCK - TILE:r - BLOCK - TILE + LN_ROWS]
            y = _layer_norm(x, lng, lnb)
            if BLOCK <= r < BLOCK + TILE:
                hres[r - BLOCK:r - BLOCK + LN_ROWS] = y
            hext[r:r + LN_ROWS] = y.astype(BF16)

    def project_qkv(lo, hi):
        rows = slice(lo, hi)
        kv = _dot(hext[rows], win_ref[:, K_OFF:K_OFF + 2 * KV_WIDTH])
        low = lax.broadcasted_iota(jnp.int32, (hi - lo, KV_WIDTH), 1) < HEAD_DIM
        for src, dst in ((kv[:, :KV_WIDTH], kvar), (kv[:, KV_WIDTH:], vvar)):
            rolled = pltpu.roll(src, HEAD_DIM, axis=1)
            dst[0, rows] = jnp.where(low, src, 0.0).astype(BF16)
            dst[1, rows] = jnp.where(low, 0.0, rolled).astype(BF16)
            dst[2, rows] = jnp.where(low, rolled, 0.0).astype(BF16)
            dst[3, rows] = jnp.where(low, 0.0, src).astype(BF16)
        q_lo, q_hi = max(lo, BLOCK), min(hi, BLOCK + TILE)
        if q_hi > q_lo:
            q_scr[q_lo - BLOCK:q_hi - BLOCK] = (
                _dot(hext[q_lo:q_hi], win_ref[:, Q_OFF:Q_OFF + ATT_WIDTH]) * (HEAD_DIM ** -0.5)
            ).astype(BF16)

    for lo_rows, hi_rows in ((BLOCK, ext // 2), (ext // 2, BLOCK + TILE), (BLOCK + TILE, ext)):
        norm_rows(lo_rows, hi_rows)
        project_qkv(lo_rows, hi_rows)

    lane_low = lax.broadcasted_iota(jnp.int32, (BLOCK, 2 * HEAD_DIM), 1) < HEAD_DIM

    def att_scores(qb, pair):
        rows = slice(qb * BLOCK, qb * BLOCK + 3 * BLOCK)
        kvh = pair // (GROUP // 2)
        q2 = q_scr[qb * BLOCK:(qb + 1) * BLOCK, pair * 2 * HEAD_DIM:(pair + 1) * 2 * HEAD_DIM]
        return _dot_nt(q2, jnp.concatenate([kvar[2 * kvh, rows], kvar[2 * kvh + 1, rows]], axis=0))

    def att_softmax(qb, pair, scores):
        if qb == 0:
            edge = jnp.where(i == 0, 1, 0)
        elif qb == QB_PER_TILE - 1:
            edge = jnp.where(i == last_i, 2, 0)
        else:
            edge = 0
        probs, inv = [], []
        for sub in range(2):
            h = 2 * pair + sub
            sink = sink_ref[h]
            p_blocks, inv_blocks = [], []
            for r in range(0, BLOCK, SM_ROWS):
                logits = (scores[r:r + SM_ROWS, sub * 3 * BLOCK:(sub + 1) * 3 * BLOCK]
                          + bias_ref[edge, h, r:r + SM_ROWS, :])
                m = jnp.maximum(jnp.max(logits, axis=-1, keepdims=True), sink)
                p = jnp.exp(logits - m)
                denom = jnp.sum(p, axis=-1, keepdims=True) + jnp.exp(sink - m)
                p_blocks.append(p.astype(BF16))
                inv_blocks.append(1.0 / denom)
            probs.append(jnp.concatenate(p_blocks, axis=0))
            inv.append(jnp.concatenate(inv_blocks, axis=0))
        return jnp.concatenate(probs, axis=1), jnp.where(lane_low, inv[0], inv[1])

    def att_values(qb, pair, probs, scale):
        rows = slice(qb * BLOCK, qb * BLOCK + 3 * BLOCK)
        kvh = pair // (GROUP // 2)
        o = _dot(probs, jnp.concatenate([vvar[2 * kvh, rows], vvar[2 * kvh + 1, rows]], axis=0))
        att_scr[qb * BLOCK:(qb + 1) * BLOCK, pair * 2 * HEAD_DIM:(pair + 1) * 2 * HEAD_DIM] = (
            o * scale).astype(BF16)

    lo = BLOCK - CONV_HALO
    hi = BLOCK + TILE + CONV_HALO
    slabs = PROJ_COLS // LANES

    def conv_input(j):
        cc = _dot(hext[lo:hi], win_ref[:, CC_OFF + j * PROJ_COLS:CC_OFF + (j + 1) * PROJ_COLS])
        cx = _dot(hext[lo:hi], win_ref[:, CX_OFF + j * PROJ_COLS:CX_OFF + (j + 1) * PROJ_COLS])
        row = lax.broadcasted_iota(jnp.int32, (TILE + 2 * CONV_HALO, 1), 0)
        inside = ((row >= CONV_HALO) | (i > 0)) & ((row < CONV_HALO + TILE) | (i < last_i))
        u = jnp.where(inside, cc * cx, 0.0)
        for s in range(slabs):
            u_scr[j * slabs + s, pl.ds(0, TILE + 2 * CONV_HALO, stride=ROW_PITCH), :] = (
                u[:, s * LANES:(s + 1) * LANES])

    def conv_branch(j):
        cb = _dot(hext[BLOCK:BLOCK + TILE], win_ref[:, CB_OFF + j * PROJ_COLS:CB_OFF + (j + 1) * PROJ_COLS])
        taps = []
        for s in range(slabs):
            lanes = slice(j * PROJ_COLS + s * LANES, j * PROJ_COLS + (s + 1) * LANES)
            w = cw_ref[0, :, lanes]
            taps.append([jnp.broadcast_to(w[k:k + 1], (MIX_ROWS, LANES)) for k in range(3)]
                        + [jnp.broadcast_to(cbias_ref[:, lanes], (MIX_ROWS, LANES))])
        for rb in range(TILE // MIX_ROWS):
            r0 = CONV_HALO + rb * MIX_ROWS
            parts = []
            for s in range(slabs):
                ld = lambda r: u_scr[j * slabs + s, pl.ds(ROW_PITCH * r, MIX_ROWS, stride=ROW_PITCH), :]
                t = taps[s]
                dw = ld(r0 - 1) * t[0] + ld(r0) * t[1] + ld(r0 + 1) * t[2] + t[3]
                parts.append(cb[rb * MIX_ROWS:(rb + 1) * MIX_ROWS, s * LANES:(s + 1) * LANES] * dw)
            conv_scr[j, rb * MIX_ROWS:(rb + 1) * MIX_ROWS] = jnp.concatenate(parts, axis=1).astype(BF16)

    def gate(n):
        cols = slice(n * PROJ_COLS, (n + 1) * PROJ_COLS)
        pre = _dot(hext[BLOCK:BLOCK + TILE],
                   win_ref[:, GATE_OFF + n * PROJ_COLS:GATE_OFF + (n + 1) * PROJ_COLS])
        bias = bg_ref[:, cols]
        for rb in range(TILE // BLOCK):
            rows = slice(rb * BLOCK, (rb + 1) * BLOCK)
            g_scr[n, rows] = _sigmoid(pre[rows] + bias)

    n_conv = CONV_WIDTH // PROJ_COLS
    n_gate = 2 * D_MODEL // PROJ_COLS
    proj_units = ([functools.partial(conv_input, j) for j in range(n_conv)]
                  + [functools.partial(conv_branch, j) for j in range(n_conv)]
                  + [functools.partial(gate, n) for n in range(n_gate)])
    n_out = D_MODEL // PROJ_COLS
    half_rows = TILE // 2

    def merge(half, n):
        rows = slice(half * half_rows, (half + 1) * half_rows)
        cols = slice(n * PROJ_COLS, (n + 1) * PROJ_COLS)
        conv = jnp.concatenate([conv_scr[j, rows] for j in range(n_conv)], axis=1)
        ya = _dot(att_scr[rows], wa_ref[:, cols])
        yc = _dot(conv, wc_ref[:, cols])
        merged_scr[n, rows] = (g_scr[n, rows] * ya + g_scr[n_out + n, rows] * yc).astype(BF16)

    def out_proj(lo, hi):
        rows = slice(lo, hi)
        return _dot(jnp.concatenate([merged_scr[n, rows] for n in range(n_out)], axis=1), wo_ref[...])

    def finish(lo, mix, blocks):
        for blk in blocks:
            r = lo + blk * LN_ROWS
            out_ref[0, r:r + LN_ROWS] = _layer_norm(
                DEEPNORM_ALPHA * hres[r:r + LN_ROWS] + mix[blk * LN_ROWS:(blk + 1) * LN_ROWS],
                g2_ref[...], b2_ref[...])

    att_units = [(qb, pair) for qb in range(QB_PER_TILE) for pair in range(N_Q_HEADS // 2)]
    filler = proj_units + [functools.partial(merge, 0, n) for n in range(n_out)]
    assert len(filler) == len(att_units) and len(proj_units) >= len(att_units) // 2
    scores = att_scores(*att_units[0])
    for k, unit in enumerate(att_units):
        ahead = att_scores(*att_units[k + 1]) if k + 1 < len(att_units) else None
        probs, scale = att_softmax(*unit, scores)
        filler[k]()
        att_values(*unit, probs, scale)
        scores = ahead
    ln_blocks = half_rows // LN_ROWS
    mix = out_proj(0, half_rows)
    for n in range(n_out):
        finish(0, mix, range(n * ln_blocks // n_out, (n + 1) * ln_blocks // n_out))
        merge(1, n)
    quarter = half_rows // 2
    mix_a = out_proj(half_rows, half_rows + quarter)
    mix_b = out_proj(half_rows + quarter, TILE)
    finish(half_rows, mix_a, range(quarter // LN_ROWS))
    finish(half_rows + quarter, mix_b, range(quarter // LN_ROWS))


def _const_spec(shape):
    return pl.BlockSpec(shape, lambda b, i: (0,) * len(shape), pipeline_mode=pl.Buffered(1))


def _mixer(x, ln_g, ln_b, w_in, b_gates, sink, bias_by_rel, conv_w, conv_b, w_a, w_c, w_o, g2, b2,
           w_ffn_up, w_ffn_down):
    bsz, seq, _ = x.shape
    n_tiles = seq // TILE
    blocks_per_tile = TILE // BLOCK
    n_blocks = seq // BLOCK
    in_specs = [
        pl.BlockSpec((1, TILE, D_MODEL), lambda b, i: (b, i, 0)),
        pl.BlockSpec((1, BLOCK, D_MODEL),
                     lambda b, i: (b, jnp.minimum((i + 1) * blocks_per_tile, n_blocks - 1), 0)),
        _const_spec((1, D_MODEL)), _const_spec((1, D_MODEL)),
        pl.BlockSpec(memory_space=pl.ANY),
        _const_spec((1, 2 * D_MODEL)),
        pl.BlockSpec(memory_space=pltpu.SMEM),
        _const_spec((N_Q_HEADS, 4 * BLOCK)),
        _const_spec((1, 3, CONV_WIDTH)), _const_spec((1, CONV_WIDTH)),
        pl.BlockSpec(memory_space=pl.ANY), pl.BlockSpec(memory_space=pl.ANY),
        pl.BlockSpec(memory_space=pl.ANY),
        _const_spec((1, D_MODEL)), _const_spec((1, D_MODEL)),
    ]
    n_steps = bsz * n_tiles
    cast_specs, cast_shapes = [], []
    for w in (w_ffn_up, w_ffn_down):
        rows = next(r for r in range(BF16_SUBLANES, w.shape[0] + 1, BF16_SUBLANES)
                    if w.shape[0] % r == 0 and w.shape[0] // r <= n_steps)
        last = w.shape[0] // rows - 1
        cast_specs.append(pl.BlockSpec(
            (rows, w.shape[1]), lambda b, i, last=last: (jnp.minimum(b * n_tiles + i, last), 0)))
        cast_shapes.append(jax.ShapeDtypeStruct(w.shape, BF16))
    ext = TILE + 2 * BLOCK
    return pl.pallas_call(
        _mixer_kernel,
        out_shape=[jax.ShapeDtypeStruct((bsz, seq, D_MODEL), F32)] + cast_shapes,
        grid=(bsz, n_tiles),
        in_specs=in_specs + cast_specs,
        out_specs=[pl.BlockSpec((1, TILE, D_MODEL), lambda b, i: (b, i, 0))] + cast_specs,
        scratch_shapes=[
            pltpu.VMEM((ext, D_MODEL), BF16),
            pltpu.VMEM((TILE, D_MODEL), F32),
            pltpu.VMEM((TILE, ATT_WIDTH), BF16),
            pltpu.VMEM((4, ext, KV_WIDTH), BF16),
            pltpu.VMEM((4, ext, KV_WIDTH), BF16),
            pltpu.VMEM((TILE, ATT_WIDTH), BF16),
            pltpu.VMEM((CONV_WIDTH // LANES, ROW_PITCH * (TILE + 2 * CONV_HALO), LANES), F32),
            pltpu.VMEM((CONV_WIDTH // PROJ_COLS, TILE, PROJ_COLS), BF16),
            pltpu.VMEM((2 * D_MODEL // PROJ_COLS, TILE, PROJ_COLS), F32),
            pltpu.VMEM((D_MODEL // PROJ_COLS, TILE, PROJ_COLS), BF16),
            pltpu.VMEM((D_MODEL, IN_PROJ_WIDTH), BF16),
            pltpu.VMEM((ATT_WIDTH, D_MODEL), BF16),
            pltpu.VMEM((CONV_WIDTH, D_MODEL), BF16),
            pltpu.VMEM((D_MODEL, D_MODEL), BF16),
            pltpu.VMEM((2, STAGE_ROWS_WIDE, IN_PROJ_WIDTH), F32),
            pltpu.VMEM((2, STAGE_ROWS, D_MODEL), F32),
            pltpu.SemaphoreType.DMA((2,)),
            pltpu.VMEM((3, N_Q_HEADS, BLOCK, 3 * BLOCK), F32),
        ],
        compiler_params=pltpu.CompilerParams(
            dimension_semantics=("arbitrary", "arbitrary"), vmem_limit_bytes=VMEM_LIMIT_BYTES),
        name="mixer",
    )(x, x, ln_g, ln_b, w_in, b_gates, sink, bias_by_rel, conv_w, conv_b, w_a, w_c, w_o, g2, b2,
      w_ffn_up, w_ffn_down)


def _ffn_kernel(hp_ref, hc_ref, hn_ref, wup_ref, cw_ref, cb_ref, wdn_ref, g_ref, b_ref, out_ref,
                hext, a_scr, u_scr, act_scr, acc):
    i = pl.program_id(1)
    last_i = pl.num_programs(1) - 1

    hext[0:CONV_HALO] = jnp.where(i > 0, hp_ref[0], 0.0).astype(BF16)
    hext[CONV_HALO:CONV_HALO + TILE] = hc_ref[0].astype(BF16)
    hext[CONV_HALO + TILE:] = jnp.where(i < last_i, hn_ref[0], 0.0).astype(BF16)

    tile_half = TILE // 2
    ext_split = (0, tile_half + 2 * CONV_HALO, TILE + 2 * CONV_HALO)

    def up_proj(c, half):
        col = c * FF_CHUNK
        start, stop = ext_split[half], ext_split[half + 1]
        dst = pl.ds(ROW_PITCH * start, stop - start, stride=ROW_PITCH)
        for scr, off in ((a_scr, col), (u_scr, D_FF + col)):
            res = _dot(hext[start:stop], wup_ref[:, off:off + FF_CHUNK])
            for slab in range(FF_CHUNK // LANES):
                scr[c % 2, slab, dst, :] = res[:, slab * LANES:(slab + 1) * LANES]

    def down_proj(c, half):
        rows = slice(half * tile_half, (half + 1) * tile_half)
        part = _dot(act_scr[c % 2, rows], wdn_ref[c * FF_CHUNK:(c + 1) * FF_CHUNK, :])
        if c == 0:
            acc[rows] = part
        else:
            acc[rows] += part

    def finish(half):
        rows = slice(half * tile_half, (half + 1) * tile_half)
        out_ref[0, rows] = _layer_norm(DEEPNORM_ALPHA * hc_ref[0, rows] + acc[rows], g_ref[...], b_ref[...])

    def conv_taps(col):
        out = []
        for slab in range(FF_CHUNK // LANES):
            lanes = slice(col + slab * LANES, col + (slab + 1) * LANES)
            w = cw_ref[0, :, lanes]
            out.append([jnp.broadcast_to(w[k:k + 1], (FF_ROWS, LANES)) for k in range(3)]
                       + [jnp.broadcast_to(cb_ref[:, lanes], (FF_ROWS, LANES))])
        return out

    def conv3(scr, slot, slab, r0, taps):
        rows = lambda r: scr[slot, slab, pl.ds(ROW_PITCH * r, FF_ROWS, stride=ROW_PITCH), :]
        return rows(r0 - 1) * taps[0] + rows(r0) * taps[1] + rows(r0 + 1) * taps[2] + taps[3]

    def gate_rows(c, rb, a_taps, u_taps):
        r0 = CONV_HALO + rb * FF_ROWS
        parts = []
        for slab in range(FF_CHUNK // LANES):
            a = conv3(a_scr, c % 2, slab, r0, a_taps[slab])
            u = conv3(u_scr, c % 2, slab, r0, u_taps[slab])
            parts.append(a * _sigmoid(a) * u)
        act_scr[c % 2, rb * FF_ROWS:(rb + 1) * FF_ROWS] = jnp.concatenate(parts, axis=1).astype(BF16)

    units = [(c, half) for c in range(N_FF_CHUNKS) for half in range(2)]
    blocks_per_half = tile_half // FF_ROWS
    up_proj(*units[0])
    taps = {}
    for k, (c, half) in enumerate(units):
        if half == 0:
            taps = {"a": conv_taps(c * FF_CHUNK), "u": conv_taps(D_FF + c * FF_CHUNK)}
        mxu_units = []
        if k + 1 < len(units):
            mxu_units.append(functools.partial(up_proj, *units[k + 1]))
        if k >= 1:
            mxu_units.append(functools.partial(down_proj, *units[k - 1]))
        per_unit = blocks_per_half // len(mxu_units)
        for m, unit in enumerate(mxu_units):
            unit()
            for rb in range(m * per_unit, (m + 1) * per_unit):
                gate_rows(c, half * blocks_per_half + rb, taps["a"], taps["u"])
    finish(0)
    down_proj(*units[-1])
    finish(1)


def _ffn(h, w_up, conv_w, conv_b, w_down, g, b):
    bsz, seq, _ = h.shape
    n_tiles = seq // TILE
    halo_per_tile = TILE // CONV_HALO
    n_halo_blocks = seq // CONV_HALO
    in_specs = [
        pl.BlockSpec((1, CONV_HALO, D_MODEL), lambda b, i: (b, jnp.maximum(i * halo_per_tile - 1, 0), 0)),
        pl.BlockSpec((1, TILE, D_MODEL), lambda b, i: (b, i, 0)),
        pl.BlockSpec((1, CONV_HALO, D_MODEL),
                     lambda b, i: (b, jnp.minimum((i + 1) * halo_per_tile, n_halo_blocks - 1), 0)),
        _const_spec((D_MODEL, 2 * D_FF)),
        _const_spec((1, 3, 2 * D_FF)), _const_spec((1, 2 * D_FF)),
        _const_spec((D_FF, D_MODEL)),
        _const_spec((1, D_MODEL)), _const_spec((1, D_MODEL)),
    ]
    ext = TILE + 2 * CONV_HALO
    return pl.pallas_call(
        _ffn_kernel,
        out_shape=jax.ShapeDtypeStruct((bsz, seq, D_MODEL), F32),
        grid=(bsz, n_tiles),
        in_specs=in_specs,
        out_specs=pl.BlockSpec((1, TILE, D_MODEL), lambda b, i: (b, i, 0)),
        scratch_shapes=[
            pltpu.VMEM((ext, D_MODEL), BF16),
            pltpu.VMEM((2, FF_CHUNK // LANES, ROW_PITCH * ext, LANES), F32),
            pltpu.VMEM((2, FF_CHUNK // LANES, ROW_PITCH * ext, LANES), F32),
            pltpu.VMEM((2, TILE, FF_CHUNK), BF16),
            pltpu.VMEM((TILE, D_MODEL), F32),
        ],
        compiler_params=pltpu.CompilerParams(
            dimension_semantics=("arbitrary", "arbitrary"), vmem_limit_bytes=VMEM_LIMIT_BYTES),
        name="ffn",
    )(h, h, h, w_up, conv_w, conv_b, w_down, g, b)


def kernel(x, ln_in_g, ln_in_b, w_in, b_gates, attn_sink, rel_bias, conv_w, conv_b, w_att_branch,
           w_conv_branch, w_o, ln_mix_g, ln_mix_b, w_ffn_up, ffn_conv_w, ffn_conv_b, w_ffn_down,
           ln_ffn_g, ln_ffn_b):
    assert w_in.shape[0] == DEPTH == 1
    row = lambda v: v.reshape(1, -1)
    h, w_up, w_down = _mixer(x, row(ln_in_g), row(ln_in_b), w_in[0], row(b_gates[0]), attn_sink[0],
                             _bias_by_rel(rel_bias), conv_w, row(conv_b[0]), w_att_branch[0],
                             w_conv_branch[0], w_o[0], row(ln_mix_g[0]), row(ln_mix_b[0]),
                             w_ffn_up[0], w_ffn_down[0])
    return _ffn(h, w_up, ffn_conv_w, row(ffn_conv_b[0]), w_down, row(ln_ffn_g[0]), row(ln_ffn_b[0]))
```

```python
import functools
import math

import jax
import jax.numpy as jnp
import numpy as np
from jax import lax
from jax.experimental import pallas as pl
from jax.experimental.pallas import tpu as pltpu

D_MODEL = 1024
HEAD_DIM = 64
N_Q_HEADS = 8
N_KV_HEADS = 2
GROUP = N_Q_HEADS // N_KV_HEADS
ATT_WIDTH = N_Q_HEADS * HEAD_DIM
KV_WIDTH = N_KV_HEADS * HEAD_DIM
WINDOW = 128
BLOCK = 128
CONV_WIDTH = D_MODEL // 2
D_FF = 2816
N_BUCKETS = 32
MAX_DISTANCE = 128
LN_EPS = 1e-5
DEPTH = 1
DEEPNORM_ALPHA = (2 * DEPTH) ** 0.25
MASK_VALUE = -1e30

Q_OFF = 0
K_OFF = ATT_WIDTH
V_OFF = K_OFF + KV_WIDTH
CB_OFF = V_OFF + KV_WIDTH
CC_OFF = CB_OFF + CONV_WIDTH
CX_OFF = CC_OFF + CONV_WIDTH
GATE_OFF = CX_OFF + CONV_WIDTH
IN_PROJ_WIDTH = GATE_OFF + 2 * D_MODEL

TILE = 512
QB_PER_TILE = TILE // BLOCK
CONV_HALO = 16
FF_CHUNK = 256
N_FF_CHUNKS = D_FF // FF_CHUNK
FF_ROWS = 32
PROJ_COLS = 256
MIX_ROWS = 32
STAGE_ROWS = 256
STAGE_ROWS_WIDE = 128
LN_ROWS = 32
SM_ROWS = 32
LANES = 128
BF16_SUBLANES = 16
ROW_PITCH = 2
VMEM_LIMIT_BYTES = 60 * 1024 * 1024

F32 = jnp.float32
BF16 = jnp.bfloat16


def _layer_norm(x, g, b):
    mu = jnp.mean(x, axis=-1, keepdims=True)
    xc = x - mu
    var = jnp.mean(xc * xc, axis=-1, keepdims=True)
    return xc * lax.rsqrt(var + LN_EPS) * g + b


def _sigmoid(x):
    return 0.5 * jnp.tanh(0.5 * x) + 0.5


def _dot(a, b):
    return jnp.dot(a, b, preferred_element_type=F32)


def _dot_nt(a, b):
    return lax.dot_general(a, b, (((1,), (1,)), ((), ())), preferred_element_type=F32)


def _bias_by_rel(rel_bias):
    rel = jnp.arange(4 * BLOCK) - (2 * BLOCK - 1)
    half = N_BUCKETS // 2
    max_exact = half // 2
    offset = jnp.where(rel > 0, half, 0)
    n = jnp.abs(rel)
    nf = jnp.maximum(n, 1).astype(jnp.float32)
    large = max_exact + (jnp.log(nf / max_exact) / math.log(MAX_DISTANCE / max_exact)
                         * (half - max_exact)).astype(jnp.int32)
    large = jnp.minimum(large, half - 1)
    bucket = offset + jnp.where(n < max_exact, n, large)
    return jnp.transpose(rel_bias.astype(F32)[bucket], (1, 0))


def _fill_bias_table(bias_ref, out_ref):
    shape = (BLOCK, 3 * BLOCK)
    col = lax.broadcasted_iota(jnp.int32, shape, 1)
    rel = col - BLOCK - lax.broadcasted_iota(jnp.int32, shape, 0)
    in_window = jnp.abs(rel) <= WINDOW
    for h in range(N_Q_HEADS):
        by_rel = jnp.broadcast_to(bias_ref[h:h + 1, :], (BLOCK, 4 * BLOCK))
        band = pltpu.roll(by_rel, 3 * BLOCK + 1, axis=1, stride=1, stride_axis=0)[:, :3 * BLOCK]
        t = jnp.where(in_window, band, MASK_VALUE)
        out_ref[0, h] = t
        out_ref[1, h] = jnp.where(col < BLOCK, MASK_VALUE, t)
        out_ref[2, h] = jnp.where(col >= 2 * BLOCK, MASK_VALUE, t)


def _stage_weight(src_hbm, dst, stage, sem):
    chunk = stage.shape[1]
    n_chunks = src_hbm.shape[0] // chunk
    assert n_chunks * chunk == src_hbm.shape[0] and stage.shape[2] == src_hbm.shape[1]

    def copy(k):
        return pltpu.make_async_copy(src_hbm.at[pl.ds(k * chunk, chunk)], stage.at[k % 2], sem.at[k % 2])

    copy(0).start()
    for k in range(n_chunks):
        if k + 1 < n_chunks:
            copy(k + 1).start()
        copy(k).wait()
        dst[k * chunk:(k + 1) * chunk] = stage[k % 2].astype(dst.dtype)


def _mixer_kernel(xc_ref, xn_ref, lng_ref, lnb_ref, win_hbm, bg_ref, sink_ref, brel_ref,
                  cw_ref, cbias_ref, wa_hbm, wc_hbm, wo_hbm, g2_ref, b2_ref, wup_in, wdn_in,
                  out_ref, wup_out, wdn_out,
                  hext, hres, q_scr, kvar, vvar, att_scr, u_scr, conv_scr, g_scr, merged_scr,
                  win_ref, wa_ref, wc_ref, wo_ref, stage_in, stage_sq, dma_sem, bias_ref):
    i = pl.program_id(1)
    last_i = pl.num_programs(1) - 1

    @pl.when((pl.program_id(0) == 0) & (i == 0))
    def _():
        _stage_weight(win_hbm, win_ref, stage_in, dma_sem)
        _stage_weight(wa_hbm, wa_ref, stage_sq, dma_sem)
        _stage_weight(wc_hbm, wc_ref, stage_sq, dma_sem)
        _stage_weight(wo_hbm, wo_ref, stage_sq, dma_sem)
        _fill_bias_table(brel_ref, bias_ref)
        kvar[:, TILE:TILE + BLOCK] = jnp.zeros((4, BLOCK, KV_WIDTH), BF16)
        vvar[:, TILE:TILE + BLOCK] = jnp.zeros((4, BLOCK, KV_WIDTH), BF16)
        hext[TILE + BLOCK - CONV_HALO:TILE + BLOCK] = jnp.zeros((CONV_HALO, D_MODEL), BF16)

    wup_out[...] = wup_in[...].astype(BF16)
    wdn_out[...] = wdn_in[...].astype(BF16)

    same_seq = i > 0
    for scr in (kvar, vvar):
        carried = scr[:, TILE:TILE + BLOCK]
        scr[:, 0:BLOCK] = jnp.where(same_seq, carried, jnp.zeros_like(carried))
    carried = hext[TILE + BLOCK - CONV_HALO:TILE + BLOCK]
    hext[BLOCK - CONV_HALO:BLOCK] = jnp.where(same_seq, carried, jnp.zeros_like(carried))

    lng = lng_ref[...]
    lnb = lnb_ref[...]
    ext = TILE + 2 * BLOCK

    def norm_rows(lo, hi):
        assert lo >= BLOCK
        for r in range(lo, hi, LN_ROWS):
            if r < BLOCK + TILE:
                x = xc_ref[0, r - BLOCK:r - BLOCK + LN_ROWS]
            else:
                x = xn_ref[0, r - BLOCK - TILE:r - BLOCK - TILE + LN_ROWS]
            y = _layer_norm(x, lng, lnb)
            if BLOCK <= r < BLOCK + TILE:
                hres[r - BLOCK:r - BLOCK + LN_ROWS] = y
            hext[r:r + LN_ROWS] = y.astype(BF16)

    def project_qkv(lo, hi):
        rows = slice(lo, hi)
        kv = _dot(hext[rows], win_ref[:, K_OFF:K_OFF + 2 * KV_WIDTH])
        low = lax.broadcasted_iota(jnp.int32, (hi - lo, KV_WIDTH), 1) < HEAD_DIM
        for src, dst in ((kv[:, :KV_WIDTH], kvar), (kv[:, KV_WIDTH:], vvar)):
            rolled = pltpu.roll(src, HEAD_DIM, axis=1)
            dst[0, rows] = jnp.where(low, src, 0.0).astype(BF16)
            dst[1, rows] = jnp.where(low, 0.0, rolled).astype(BF16)
            dst[2, rows] = jnp.where(low, rolled, 0.0).astype(BF16)
            dst[3, rows] = jnp.where(low, 0.0, src).astype(BF16)
        q_lo, q_hi = max(lo, BLOCK), min(hi, BLOCK + TILE)
        if q_hi > q_lo:
            q_scr[q_lo - BLOCK:q_hi - BLOCK] = (
                _dot(hext[q_lo:q_hi], win_ref[:, Q_OFF:Q_OFF + ATT_WIDTH]) * (HEAD_DIM ** -0.5)
            ).astype(BF16)

    for lo_rows, hi_rows in ((BLOCK, ext // 2), (ext // 2, BLOCK + TILE), (BLOCK + TILE, ext)):
        norm_rows(lo_rows, hi_rows)
        project_qkv(lo_rows, hi_rows)
    hcb = hext[BLOCK:BLOCK + TILE]

    lane_low = lax.broadcasted_iota(jnp.int32, (BLOCK, 2 * HEAD_DIM), 1) < HEAD_DIM

    def att_scores(qb, pair):
        rows = slice(qb * BLOCK, qb * BLOCK + 3 * BLOCK)
        kvh = pair // (GROUP // 2)
        q2 = q_scr[qb * BLOCK:(qb + 1) * BLOCK, pair * 2 * HEAD_DIM:(pair + 1) * 2 * HEAD_DIM]
        return _dot_nt(q2, jnp.concatenate([kvar[2 * kvh, rows], kvar[2 * kvh + 1, rows]], axis=0))

    def att_softmax(qb, pair, scores):
        if qb == 0:
            edge = jnp.where(i == 0, 1, 0)
        elif qb == QB_PER_TILE - 1:
            edge = jnp.where(i == last_i, 2, 0)
        else:
            edge = 0
        probs, inv = [], []
        for sub in range(2):
            h = 2 * pair + sub
            sink = sink_ref[h]
            p_blocks, inv_blocks = [], []
            for r in range(0, BLOCK, SM_ROWS):
                logits = (scores[r:r + SM_ROWS, sub * 3 * BLOCK:(sub + 1) * 3 * BLOCK]
                          + bias_ref[edge, h, r:r + SM_ROWS, :])
                m = jnp.maximum(jnp.max(logits, axis=-1, keepdims=True), sink)
                p = jnp.exp(logits - m)
                denom = jnp.sum(p, axis=-1, keepdims=True) + jnp.exp(sink - m)
                p_blocks.append(p.astype(BF16))
                inv_blocks.append(1.0 / denom)
            probs.append(jnp.concatenate(p_blocks, axis=0))
            inv.append(jnp.concatenate(inv_blocks, axis=0))
        return jnp.concatenate(probs, axis=1), jnp.where(lane_low, inv[0], inv[1])

    def att_values(qb, pair, probs, scale):
        rows = slice(qb * BLOCK, qb * BLOCK + 3 * BLOCK)
        kvh = pair // (GROUP // 2)
        o = _dot(probs, jnp.concatenate([vvar[2 * kvh, rows], vvar[2 * kvh + 1, rows]], axis=0))
        att_scr[qb * BLOCK:(qb + 1) * BLOCK, pair * 2 * HEAD_DIM:(pair + 1) * 2 * HEAD_DIM] = (
            o * scale).astype(BF16)

    lo = BLOCK - CONV_HALO
    hi = BLOCK + TILE + CONV_HALO
    slabs = PROJ_COLS // LANES

    def conv_input(j):
        cc = _dot(hext[lo:hi], win_ref[:, CC_OFF + j * PROJ_COLS:CC_OFF + (j + 1) * PROJ_COLS])
        cx = _dot(hext[lo:hi], win_ref[:, CX_OFF + j * PROJ_COLS:CX_OFF + (j + 1) * PROJ_COLS])
        row = lax.broadcasted_iota(jnp.int32, (TILE + 2 * CONV_HALO, 1), 0)
        inside = ((row >= CONV_HALO) | (i > 0)) & ((row < CONV_HALO + TILE) | (i < last_i))
        u = jnp.where(inside, cc * cx, 0.0)
        for s in range(slabs):
            u_scr[j * slabs + s, pl.ds(0, TILE + 2 * CONV_HALO, stride=ROW_PITCH), :] = (
                u[:, s * LANES:(s + 1) * LANES])

    def conv_branch(j):
        cb = _dot(hcb, win_ref[:, CB_OFF + j * PROJ_COLS:CB_OFF + (j + 1) * PROJ_COLS])
        taps = []
        for s in range(slabs):
            lanes = slice(j * PROJ_COLS + s * LANES, j * PROJ_COLS + (s + 1) * LANES)
            w = cw_ref[0, :, lanes]
            taps.append([jnp.broadcast_to(w[k:k + 1], (MIX_ROWS, LANES)) for k in range(3)]
                        + [jnp.broadcast_to(cbias_ref[:, lanes], (MIX_ROWS, LANES))])
        for rb in range(TILE // MIX_ROWS):
            r0 = CONV_HALO + rb * MIX_ROWS
            parts = []
            for s in range(slabs):
                ld = lambda r: u_scr[j * slabs + s, pl.ds(ROW_PITCH * r, MIX_ROWS, stride=ROW_PITCH), :]
                t = taps[s]
                dw = ld(r0 - 1) * t[0] + ld(r0) * t[1] + ld(r0 + 1) * t[2] + t[3]
                parts.append(cb[rb * MIX_ROWS:(rb + 1) * MIX_ROWS, s * LANES:(s + 1) * LANES] * dw)
            conv_scr[j, rb * MIX_ROWS:(rb + 1) * MIX_ROWS] = jnp.concatenate(parts, axis=1).astype(BF16)

    def gate(n):
        cols = slice(n * PROJ_COLS, (n + 1) * PROJ_COLS)
        pre = _dot(hcb, win_ref[:, GATE_OFF + n * PROJ_COLS:GATE_OFF + (n + 1) * PROJ_COLS])
        bias = bg_ref[:, cols]
        for rb in range(TILE // BLOCK):
            rows = slice(rb * BLOCK, (rb + 1) * BLOCK)
            g_scr[n, rows] = _sigmoid(pre[rows] + bias)

    n_conv = CONV_WIDTH // PROJ_COLS
    n_gate = 2 * D_MODEL // PROJ_COLS
    proj_units = ([functools.partial(conv_input, j) for j in range(n_conv)]
                  + [functools.partial(conv_branch, j) for j in range(n_conv)]
                  + [functools.partial(gate, n) for n in range(n_gate)])
    n_out = D_MODEL // PROJ_COLS
    half_rows = TILE // 2

    def merge(half, n):
        rows = slice(half * half_rows, (half + 1) * half_rows)
        cols = slice(n * PROJ_COLS, (n + 1) * PROJ_COLS)
        conv = jnp.concatenate([conv_scr[j, rows] for j in range(n_conv)], axis=1)
        ya = _dot(att_scr[rows], wa_ref[:, cols])
        yc = _dot(conv, wc_ref[:, cols])
        merged_scr[n, rows] = (g_scr[n, rows] * ya + g_scr[n_out + n, rows] * yc).astype(BF16)

    def out_proj(lo, hi):
        rows = slice(lo, hi)
        return _dot(jnp.concatenate([merged_scr[n, rows] for n in range(n_out)], axis=1), wo_ref[...])

    def finish(lo, mix, blocks):
        for blk in blocks:
            r = lo + blk * LN_ROWS
            out_ref[0, r:r + LN_ROWS] = _layer_norm(
                DEEPNORM_ALPHA * hres[r:r + LN_ROWS] + mix[blk * LN_ROWS:(blk + 1) * LN_ROWS],
                g2_ref[...], b2_ref[...])

    att_units = [(qb, pair) for qb in range(QB_PER_TILE) for pair in range(N_Q_HEADS // 2)]
    filler = proj_units + [functools.partial(merge, 0, n) for n in range(n_out)]
    assert len(filler) == len(att_units) and len(proj_units) >= len(att_units) // 2
    scores = att_scores(*att_units[0])
    for k, unit in enumerate(att_units):
        ahead = att_scores(*att_units[k + 1]) if k + 1 < len(att_units) else None
        probs, scale = att_softmax(*unit, scores)
        filler[k]()
        att_values(*unit, probs, scale)
        scores = ahead
    ln_blocks = half_rows // LN_ROWS
    mix = out_proj(0, half_rows)
    for n in range(n_out):
        finish(0, mix, range(n * ln_blocks // n_out, (n + 1) * ln_blocks // n_out))
        merge(1, n)
    quarter = half_rows // 2
    mix_a = out_proj(half_rows, half_rows + quarter)
    mix_b = out_proj(half_rows + quarter, TILE)
    finish(half_rows, mix_a, range(quarter // LN_ROWS))
    finish(half_rows + quarter, mix_b, range(quarter // LN_ROWS))


def _const_spec(shape):
    return pl.BlockSpec(shape, lambda b, i: (0,) * len(shape), pipeline_mode=pl.Buffered(1))


def _mixer(x, ln_g, ln_b, w_in, b_gates, sink, bias_by_rel, conv_w, conv_b, w_a, w_c, w_o, g2, b2,
           w_ffn_up, w_ffn_down):
    bsz, seq, _ = x.shape
    n_tiles = seq // TILE
    blocks_per_tile = TILE // BLOCK
    n_blocks = seq // BLOCK
    in_specs = [
        pl.BlockSpec((1, TILE, D_MODEL), lambda b, i: (b, i, 0)),
        pl.BlockSpec((1, BLOCK, D_MODEL),
                     lambda b, i: (b, jnp.minimum((i + 1) * blocks_per_tile, n_blocks - 1), 0)),
        _const_spec((1, D_MODEL)), _const_spec((1, D_MODEL)),
        pl.BlockSpec(memory_space=pl.ANY),
        _const_spec((1, 2 * D_MODEL)),
        pl.BlockSpec(memory_space=pltpu.SMEM),
        _const_spec((N_Q_HEADS, 4 * BLOCK)),
        _const_spec((1, 3, CONV_WIDTH)), _const_spec((1, CONV_WIDTH)),
        pl.BlockSpec(memory_space=pl.ANY), pl.BlockSpec(memory_space=pl.ANY),
        pl.BlockSpec(memory_space=pl.ANY),
        _const_spec((1, D_MODEL)), _const_spec((1, D_MODEL)),
    ]
    n_steps = bsz * n_tiles
    cast_specs, cast_shapes = [], []
    for w in (w_ffn_up, w_ffn_down):
        rows = next(r for r in range(BF16_SUBLANES, w.shape[0] + 1, BF16_SUBLANES)
                    if w.shape[0] % r == 0 and w.shape[0] // r <= n_steps)
        last = w.shape[0] // rows - 1
        cast_specs.append(pl.BlockSpec(
            (rows, w.shape[1]), lambda b, i, last=last: (jnp.minimum(b * n_tiles + i, last), 0)))
        cast_shapes.append(jax.ShapeDtypeStruct(w.shape, BF16))
    ext = TILE + 2 * BLOCK
    return pl.pallas_call(
        _mixer_kernel,
        out_shape=[jax.ShapeDtypeStruct((bsz, seq, D_MODEL), F32)] + cast_shapes,
        grid=(bsz, n_tiles),
        in_specs=in_specs + cast_specs,
        out_specs=[pl.BlockSpec((1, TILE, D_MODEL), lambda b, i: (b, i, 0))] + cast_specs,
        scratch_shapes=[
            pltpu.VMEM((ext, D_MODEL), BF16),
            pltpu.VMEM((TILE, D_MODEL), F32),
            pltpu.VMEM((TILE, ATT_WIDTH), BF16),
            pltpu.VMEM((4, ext, KV_WIDTH), BF16),
            pltpu.VMEM((4, ext, KV_WIDTH), BF16),
            pltpu.VMEM((TILE, ATT_WIDTH), BF16),
            pltpu.VMEM((CONV_WIDTH // LANES, ROW_PITCH * (TILE + 2 * CONV_HALO), LANES), F32),
            pltpu.VMEM((CONV_WIDTH // PROJ_COLS, TILE, PROJ_COLS), BF16),
            pltpu.VMEM((2 * D_MODEL // PROJ_COLS, TILE, PROJ_COLS), F32),
            pltpu.VMEM((D_MODEL // PROJ_COLS, TILE, PROJ_COLS), BF16),
            pltpu.VMEM((D_MODEL, IN_PROJ_WIDTH), BF16),
            pltpu.VMEM((ATT_WIDTH, D_MODEL), BF16),
            pltpu.VMEM((CONV_WIDTH, D_MODEL), BF16),
            pltpu.VMEM((D_MODEL, D_MODEL), BF16),
            pltpu.VMEM((2, STAGE_ROWS_WIDE, IN_PROJ_WIDTH), F32),
            pltpu.VMEM((2, STAGE_ROWS, D_MODEL), F32),
            pltpu.SemaphoreType.DMA((2,)),
            pltpu.VMEM((3, N_Q_HEADS, BLOCK, 3 * BLOCK), F32),
        ],
        compiler_params=pltpu.CompilerParams(
            dimension_semantics=("arbitrary", "arbitrary"), vmem_limit_bytes=VMEM_LIMIT_BYTES),
        name="mixer",
    )(x, x, ln_g, ln_b, w_in, b_gates, sink, bias_by_rel, conv_w, conv_b, w_a, w_c, w_o, g2, b2,
      w_ffn_up, w_ffn_down)


def _ffn_call(tiles_per_seq, h_hbm, wup_ref, cw_ref, cb_ref, wdn_ref, g_ref, b_ref, out_hbm,
              hext, a_scr, u_scr, act_scr, acc):
    bsz, seq, _ = h_hbm.shape
    halo_per_tile = TILE // CONV_HALO
    n_halo_blocks = seq // CONV_HALO

    def tile(hp_ref, hc_ref, hn_ref, out_ref):
        _ffn_kernel(pl.program_id(1), tiles_per_seq - 1, hp_ref, hc_ref, hn_ref, wup_ref, cw_ref, cb_ref,
                    wdn_ref, g_ref, b_ref, out_ref, hext, a_scr, u_scr, act_scr, acc)

    pltpu.emit_pipeline(
        tile,
        grid=(bsz, tiles_per_seq),
        in_specs=[
            pl.BlockSpec((1, CONV_HALO, D_MODEL),
                         lambda b, i: (b, jnp.maximum(i * halo_per_tile - 1, 0), 0)),
            pl.BlockSpec((1, TILE, D_MODEL), lambda b, i: (b, i, 0)),
            pl.BlockSpec((1, CONV_HALO, D_MODEL),
                         lambda b, i: (b, jnp.minimum((i + 1) * halo_per_tile, n_halo_blocks - 1), 0)),
        ],
        out_specs=[pl.BlockSpec((1, TILE, D_MODEL), lambda b, i: (b, i, 0))],
    )(h_hbm, h_hbm, h_hbm, out_hbm)


def _ffn_kernel(i, last_i, hp_ref, hc_ref, hn_ref, wup_ref, cw_ref, cb_ref, wdn_ref, g_ref, b_ref, out_ref,
                hext, a_scr, u_scr, act_scr, acc):

    hext[0:CONV_HALO] = jnp.where(i > 0, hp_ref[0], 0.0).astype(BF16)
    hext[CONV_HALO:CONV_HALO + TILE] = hc_ref[0].astype(BF16)
    hext[CONV_HALO + TILE:] = jnp.where(i < last_i, hn_ref[0], 0.0).astype(BF16)

    tile_half = TILE // 2
    ext_split = (0, tile_half + 2 * CONV_HALO, TILE + 2 * CONV_HALO)

    def up_proj(c, half):
        col = c * FF_CHUNK
        start, stop = ext_split[half], ext_split[half + 1]
        dst = pl.ds(ROW_PITCH * start, stop - start, stride=ROW_PITCH)
        for scr, off in ((a_scr, col), (u_scr, D_FF + col)):
            res = _dot(hext[start:stop], wup_ref[:, off:off + FF_CHUNK])
            for slab in range(FF_CHUNK // LANES):
                scr[c % 2, slab, dst, :] = res[:, slab * LANES:(slab + 1) * LANES]

    def down_proj(c, half):
        rows = slice(half * tile_half, (half + 1) * tile_half)
        part = _dot(act_scr[c % 2, rows], wdn_ref[c * FF_CHUNK:(c + 1) * FF_CHUNK, :])
        if c == 0:
            acc[rows] = part
        else:
            acc[rows] += part

    def finish(half):
        rows = slice(half * tile_half, (half + 1) * tile_half)
        out_ref[0, rows] = _layer_norm(DEEPNORM_ALPHA * hc_ref[0, rows] + acc[rows], g_ref[...], b_ref[...])

    def conv_taps(col):
        out = []
        for slab in range(FF_CHUNK // LANES):
            lanes = slice(col + slab * LANES, col + (slab + 1) * LANES)
            w = cw_ref[0, :, lanes]
            out.append([jnp.broadcast_to(w[k:k + 1], (FF_ROWS, LANES)) for k in range(3)]
                       + [jnp.broadcast_to(cb_ref[:, lanes], (FF_ROWS, LANES))])
        return out

    def conv3(scr, slot, slab, r0, taps):
        rows = lambda r: scr[slot, slab, pl.ds(ROW_PITCH * r, FF_ROWS, stride=ROW_PITCH), :]
        return rows(r0 - 1) * taps[0] + rows(r0) * taps[1] + rows(r0 + 1) * taps[2] + taps[3]

    def gate_rows(c, rb, a_taps, u_taps):
        r0 = CONV_HALO + rb * FF_ROWS
        parts = []
        for slab in range(FF_CHUNK // LANES):
            a = conv3(a_scr, c % 2, slab, r0, a_taps[slab])
            u = conv3(u_scr, c % 2, slab, r0, u_taps[slab])
            parts.append(a * _sigmoid(a) * u)
        act_scr[c % 2, rb * FF_ROWS:(rb + 1) * FF_ROWS] = jnp.concatenate(parts, axis=1).astype(BF16)

    units = [(c, half) for c in range(N_FF_CHUNKS) for half in range(2)]
    blocks_per_half = tile_half // FF_ROWS
    up_proj(*units[0])
    taps = {}
    for k, (c, half) in enumerate(units):
        if half == 0:
            taps = {"a": conv_taps(c * FF_CHUNK), "u": conv_taps(D_FF + c * FF_CHUNK)}
        mxu_units = []
        if k + 1 < len(units):
            mxu_units.append(functools.partial(up_proj, *units[k + 1]))
        if k >= 1:
            mxu_units.append(functools.partial(down_proj, *units[k - 1]))
        per_unit = blocks_per_half // len(mxu_units)
        for m, unit in enumerate(mxu_units):
            unit()
            for rb in range(m * per_unit, (m + 1) * per_unit):
                gate_rows(c, half * blocks_per_half + rb, taps["a"], taps["u"])
    finish(0)
    down_proj(*units[-1])
    finish(1)


def _ffn(h, w_up, conv_w, conv_b, w_down, g, b):
    bsz, seq, _ = h.shape
    whole = pl.BlockSpec(memory_space=pltpu.VMEM)
    in_specs = [
        pl.BlockSpec(memory_space=pl.ANY),
        whole,
        whole, whole,
        whole,
        whole, whole,
    ]
    ext = TILE + 2 * CONV_HALO
    return pl.pallas_call(
        functools.partial(_ffn_call, seq // TILE),
        out_shape=jax.ShapeDtypeStruct((bsz, seq, D_MODEL), F32),
        in_specs=in_specs,
        out_specs=pl.BlockSpec(memory_space=pl.ANY),
        scratch_shapes=[
            pltpu.VMEM((ext, D_MODEL), BF16),
            pltpu.VMEM((2, FF_CHUNK // LANES, ROW_PITCH * ext, LANES), F32),
            pltpu.VMEM((2, FF_CHUNK // LANES, ROW_PITCH * ext, LANES), F32),
            pltpu.VMEM((2, TILE, FF_CHUNK), BF16),
            pltpu.VMEM((TILE, D_MODEL), F32),
        ],
        compiler_params=pltpu.CompilerParams(vmem_limit_bytes=VMEM_LIMIT_BYTES),
        name="ffn",
    )(h, w_up, conv_w, conv_b, w_down, g, b)


def kernel(x, ln_in_g, ln_in_b, w_in, b_gates, attn_sink, rel_bias, conv_w, conv_b, w_att_branch,
           w_conv_branch, w_o, ln_mix_g, ln_mix_b, w_ffn_up, ffn_conv_w, ffn_conv_b, w_ffn_down,
           ln_ffn_g, ln_ffn_b):
    assert w_in.shape[0] == DEPTH == 1
    row = lambda v: v.reshape(1, -1)
    h, w_up, w_down = _mixer(x, row(ln_in_g), row(ln_in_b), w_in[0], row(b_gates[0]), attn_sink[0],
                             _bias_by_rel(rel_bias), conv_w, row(conv_b[0]), w_att_branch[0],
                             w_conv_branch[0], w_o[0], row(ln_mix_g[0]), row(ln_mix_b[0]),
                             w_ffn_up[0], w_ffn_down[0])
    return _ffn(h, w_up, ffn_conv_w, row(ffn_conv_b[0]), w_down, row(ln_ffn_g[0]), row(ln_ffn_b[0]))
```

```python
import functools
import math

import jax
import jax.numpy as jnp
import numpy as np
from jax import lax
from jax.experimental import pallas as pl
from jax.experimental.pallas import tpu as pltpu

D_MODEL = 1024
HEAD_DIM = 64
N_Q_HEADS = 8
N_KV_HEADS = 2
GROUP = N_Q_HEADS // N_KV_HEADS
ATT_WIDTH = N_Q_HEADS * HEAD_DIM
KV_WIDTH = N_KV_HEADS * HEAD_DIM
WINDOW = 128
BLOCK = 128
CONV_WIDTH = D_MODEL // 2
D_FF = 2816
N_BUCKETS = 32
MAX_DISTANCE = 128
LN_EPS = 1e-5
DEPTH = 1
DEEPNORM_ALPHA = (2 * DEPTH) ** 0.25
MASK_VALUE = -1e30

Q_OFF = 0
K_OFF = ATT_WIDTH
V_OFF = K_OFF + KV_WIDTH
CB_OFF = V_OFF + KV_WIDTH
CC_OFF = CB_OFF + CONV_WIDTH
CX_OFF = CC_OFF + CONV_WIDTH
GATE_OFF = CX_OFF + CONV_WIDTH
IN_PROJ_WIDTH = GATE_OFF + 2 * D_MODEL

TILE = 512
QB_PER_TILE = TILE // BLOCK
CONV_HALO = 16
FF_CHUNK = 256
N_FF_CHUNKS = D_FF // FF_CHUNK
FF_ROWS = 32
PROJ_COLS = 256
MIX_ROWS = 32
STAGE_ROWS = 256
STAGE_ROWS_WIDE = 128
LN_ROWS = 32
SM_ROWS = 32
LANES = 128
BF16_SUBLANES = 16
ROW_PITCH = 2
VMEM_LIMIT_BYTES = 60 * 1024 * 1024

F32 = jnp.float32
BF16 = jnp.bfloat16


def _layer_norm(x, g, b):
    mu = jnp.mean(x, axis=-1, keepdims=True)
    xc = x - mu
    var = jnp.mean(xc * xc, axis=-1, keepdims=True)
    return xc * lax.rsqrt(var + LN_EPS) * g + b


def _sigmoid(x):
    return 0.5 * jnp.tanh(0.5 * x) + 0.5


def _dot(a, b):
    return jnp.dot(a, b, preferred_element_type=F32)


def _dot_nt(a, b):
    return lax.dot_general(a, b, (((1,), (1,)), ((), ())), preferred_element_type=F32)


def _bias_by_rel(rel_bias):
    rel = jnp.arange(4 * BLOCK) - (2 * BLOCK - 1)
    half = N_BUCKETS // 2
    max_exact = half // 2
    offset = jnp.where(rel > 0, half, 0)
    n = jnp.abs(rel)
    nf = jnp.maximum(n, 1).astype(jnp.float32)
    large = max_exact + (jnp.log(nf / max_exact) / math.log(MAX_DISTANCE / max_exact)
                         * (half - max_exact)).astype(jnp.int32)
    large = jnp.minimum(large, half - 1)
    bucket = offset + jnp.where(n < max_exact, n, large)
    return jnp.transpose(rel_bias.astype(F32)[bucket], (1, 0))


def _fill_bias_table(bias_ref, out_ref):
    shape = (BLOCK, 3 * BLOCK)
    col = lax.broadcasted_iota(jnp.int32, shape, 1)
    rel = col - BLOCK - lax.broadcasted_iota(jnp.int32, shape, 0)
    in_window = jnp.abs(rel) <= WINDOW
    for h in range(N_Q_HEADS):
        by_rel = jnp.broadcast_to(bias_ref[h:h + 1, :], (BLOCK, 4 * BLOCK))
        band = pltpu.roll(by_rel, 3 * BLOCK + 1, axis=1, stride=1, stride_axis=0)[:, :3 * BLOCK]
        t = jnp.where(in_window, band, MASK_VALUE)
        out_ref[0, h] = t
        out_ref[1, h] = jnp.where(col < BLOCK, MASK_VALUE, t)
        out_ref[2, h] = jnp.where(col >= 2 * BLOCK, MASK_VALUE, t)


def _stage_weight(src_hbm, dst, stage, sem):
    chunk = stage.shape[1]
    n_chunks = src_hbm.shape[0] // chunk
    assert n_chunks * chunk == src_hbm.shape[0] and stage.shape[2] == src_hbm.shape[1]

    def copy(k):
        return pltpu.make_async_copy(src_hbm.at[pl.ds(k * chunk, chunk)], stage.at[k % 2], sem.at[k % 2])

    copy(0).start()
    for k in range(n_chunks):
        if k + 1 < n_chunks:
            copy(k + 1).start()
        copy(k).wait()
        dst[k * chunk:(k + 1) * chunk] = stage[k % 2].astype(dst.dtype)


def _mixer_kernel(xc_ref, xn_ref, lng_ref, lnb_ref, win_hbm, bg_ref, sink_ref, brel_ref,
                  cw_ref, cbias_ref, wa_hbm, wc_hbm, wo_hbm, g2_ref, b2_ref, wup_in, wdn_in,
                  out_ref, wup_out, wdn_out,
                  hext, hres, q_scr, kvar, vvar, att_scr, u_scr, conv_scr, g_scr, merged_scr,
                  win_ref, wa_ref, wc_ref, wo_ref, stage_in, stage_sq, dma_sem, bias_ref):
    i = pl.program_id(1)
    last_i = pl.num_programs(1) - 1

    @pl.when((pl.program_id(0) == 0) & (i == 0))
    def _():
        _stage_weight(win_hbm, win_ref, stage_in, dma_sem)
        _stage_weight(wa_hbm, wa_ref, stage_sq, dma_sem)
        _stage_weight(wc_hbm, wc_ref, stage_sq, dma_sem)
        _stage_weight(wo_hbm, wo_ref, stage_sq, dma_sem)
        _fill_bias_table(brel_ref, bias_ref)
        kvar[:, TILE:TILE + BLOCK] = jnp.zeros((4, BLOCK, KV_WIDTH), BF16)
        vvar[:, TILE:TILE + BLOCK] = jnp.zeros((4, BLOCK, KV_WIDTH), BF16)
        hext[TILE + BLOCK - CONV_HALO:TILE + BLOCK] = jnp.zeros((CONV_HALO, D_MODEL), BF16)

    wup_out[...] = wup_in[...].astype(BF16)
    wdn_out[...] = wdn_in[...].astype(BF16)

    same_seq = i > 0
    for scr in (kvar, vvar):
        carried = scr[:, TILE:TILE + BLOCK]
        scr[:, 0:BLOCK] = jnp.where(same_seq, carried, jnp.zeros_like(carried))
    carried = hext[TILE + BLOCK - CONV_HALO:TILE + BLOCK]
    hext[BLOCK - CONV_HALO:BLOCK] = jnp.where(same_seq, carried, jnp.zeros_like(carried))

    lng = lng_ref[...]
    lnb = lnb_ref[...]
    ext = TILE + 2 * BLOCK

    def norm_rows(lo, hi):
        assert lo >= BLOCK
        for r in range(lo, hi, LN_ROWS):
            if r < BLOCK + TILE:
                x = xc_ref[0, r - BLOCK:r - BLOCK + LN_ROWS]
            else:
                x = xn_ref[0, r - BLOCK - TILE:r - BLOCK - TILE + LN_ROWS]
            y = _layer_norm(x, lng, lnb)
            if BLOCK <= r < BLOCK + TILE:
                hres[r - BLOCK:r - BLOCK + LN_ROWS] = y
            hext[r:r + LN_ROWS] = y.astype(BF16)

    def project_qkv(lo, hi):
        rows = slice(lo, hi)
        kv = _dot(hext[rows], win_ref[:, K_OFF:K_OFF + 2 * KV_WIDTH])
        low = lax.broadcasted_iota(jnp.int32, (hi - lo, KV_WIDTH), 1) < HEAD_DIM
        for src, dst in ((kv[:, :KV_WIDTH], kvar), (kv[:, KV_WIDTH:], vvar)):
            rolled = pltpu.roll(src, HEAD_DIM, axis=1)
            dst[0, rows] = jnp.where(low, src, 0.0).astype(BF16)
            dst[1, rows] = jnp.where(low, 0.0, rolled).astype(BF16)
            dst[2, rows] = jnp.where(low, rolled, 0.0).astype(BF16)
            dst[3, rows] = jnp.where(low, 0.0, src).astype(BF16)
        q_lo, q_hi = max(lo, BLOCK), min(hi, BLOCK + TILE)
        if q_hi > q_lo:
            q_scr[q_lo - BLOCK:q_hi - BLOCK] = (
                _dot(hext[q_lo:q_hi], win_ref[:, Q_OFF:Q_OFF + ATT_WIDTH]) * (HEAD_DIM ** -0.5)
            ).astype(BF16)

    for lo_rows, hi_rows in ((BLOCK, ext // 2), (ext // 2, BLOCK + TILE), (BLOCK + TILE, ext)):
        norm_rows(lo_rows, hi_rows)
        project_qkv(lo_rows, hi_rows)
    hcb = hext[BLOCK:BLOCK + TILE]

    lane_low = lax.broadcasted_iota(jnp.int32, (BLOCK, 2 * HEAD_DIM), 1) < HEAD_DIM

    def att_scores(qb, pair):
        rows = slice(qb * BLOCK, qb * BLOCK + 3 * BLOCK)
        kvh = pair // (GROUP // 2)
        q2 = q_scr[qb * BLOCK:(qb + 1) * BLOCK, pair * 2 * HEAD_DIM:(pair + 1) * 2 * HEAD_DIM]
        return _dot_nt(q2, jnp.concatenate([kvar[2 * kvh, rows], kvar[2 * kvh + 1, rows]], axis=0))

    def att_softmax(qb, pair, scores):
        if qb == 0:
            edge = jnp.where(i == 0, 1, 0)
        elif qb == QB_PER_TILE - 1:
            edge = jnp.where(i == last_i, 2, 0)
        else:
            edge = 0
        probs, inv = [], []
        for sub in range(2):
            h = 2 * pair + sub
            sink = sink_ref[h]
            p_blocks, inv_blocks = [], []
            for r in range(0, BLOCK, SM_ROWS):
                logits = (scores[r:r + SM_ROWS, sub * 3 * BLOCK:(sub + 1) * 3 * BLOCK]
                          + bias_ref[edge, h, r:r + SM_ROWS, :])
                m = jnp.maximum(jnp.max(logits, axis=-1, keepdims=True), sink)
                p = jnp.exp(logits - m)
                denom = jnp.sum(p, axis=-1, keepdims=True) + jnp.exp(sink - m)
                p_blocks.append(p.astype(BF16))
                inv_blocks.append(1.0 / denom)
            probs.append(jnp.concatenate(p_blocks, axis=0))
            inv.append(jnp.concatenate(inv_blocks, axis=0))
        return jnp.concatenate(probs, axis=1), jnp.where(lane_low, inv[0], inv[1])

    def att_values(qb, pair, probs, scale):
        rows = slice(qb * BLOCK, qb * BLOCK + 3 * BLOCK)
        kvh = pair // (GROUP // 2)
        o = _dot(probs, jnp.concatenate([vvar[2 * kvh, rows], vvar[2 * kvh + 1, rows]], axis=0))
        att_scr[qb * BLOCK:(qb + 1) * BLOCK, pair * 2 * HEAD_DIM:(pair + 1) * 2 * HEAD_DIM] = (
            o * scale).astype(BF16)

    lo = BLOCK - CONV_HALO
    hi = BLOCK + TILE + CONV_HALO
    slabs = PROJ_COLS // LANES

    def conv_input(j):
        cc = _dot(hext[lo:hi], win_ref[:, CC_OFF + j * PROJ_COLS:CC_OFF + (j + 1) * PROJ_COLS])
        cx = _dot(hext[lo:hi], win_ref[:, CX_OFF + j * PROJ_COLS:CX_OFF + (j + 1) * PROJ_COLS])
        row = lax.broadcasted_iota(jnp.int32, (TILE + 2 * CONV_HALO, 1), 0)
        inside = ((row >= CONV_HALO) | (i > 0)) & ((row < CONV_HALO + TILE) | (i < last_i))
        u = jnp.where(inside, cc * cx, 0.0)
        for s in range(slabs):
            u_scr[j * slabs + s, pl.ds(0, TILE + 2 * CONV_HALO, stride=ROW_PITCH), :] = (
                u[:, s * LANES:(s + 1) * LANES])

    def conv_branch(j):
        cb = _dot(hcb, win_ref[:, CB_OFF + j * PROJ_COLS:CB_OFF + (j + 1) * PROJ_COLS])
        taps = []
        for s in range(slabs):
            lanes = slice(j * PROJ_COLS + s * LANES, j * PROJ_COLS + (s + 1) * LANES)
            w = cw_ref[0, :, lanes]
            taps.append([jnp.broadcast_to(w[k:k + 1], (MIX_ROWS, LANES)) for k in range(3)]
                        + [jnp.broadcast_to(cbias_ref[:, lanes], (MIX_ROWS, LANES))])
        for rb in range(TILE // MIX_ROWS):
            r0 = CONV_HALO + rb * MIX_ROWS
            parts = []
            for s in range(slabs):
                ld = lambda r: u_scr[j * slabs + s, pl.ds(ROW_PITCH * r, MIX_ROWS, stride=ROW_PITCH), :]
                t = taps[s]
                dw = ld(r0 - 1) * t[0] + ld(r0) * t[1] + ld(r0 + 1) * t[2] + t[3]
                parts.append(cb[rb * MIX_ROWS:(rb + 1) * MIX_ROWS, s * LANES:(s + 1) * LANES] * dw)
            conv_scr[j, rb * MIX_ROWS:(rb + 1) * MIX_ROWS] = jnp.concatenate(parts, axis=1).astype(BF16)

    def gate(n):
        cols = slice(n * PROJ_COLS, (n + 1) * PROJ_COLS)
        pre = _dot(hcb, win_ref[:, GATE_OFF + n * PROJ_COLS:GATE_OFF + (n + 1) * PROJ_COLS])
        bias = bg_ref[:, cols]
        for rb in range(TILE // BLOCK):
            rows = slice(rb * BLOCK, (rb + 1) * BLOCK)
            g_scr[n, rows] = _sigmoid(pre[rows] + bias)

    n_conv = CONV_WIDTH // PROJ_COLS
    n_gate = 2 * D_MODEL // PROJ_COLS
    proj_units = ([functools.partial(conv_input, j) for j in range(n_conv)]
                  + [functools.partial(conv_branch, j) for j in range(n_conv)]
                  + [functools.partial(gate, n) for n in range(n_gate)])
    n_out = D_MODEL // PROJ_COLS
    half_rows = TILE // 2

    def merge(half, n):
        rows = slice(half * half_rows, (half + 1) * half_rows)
        cols = slice(n * PROJ_COLS, (n + 1) * PROJ_COLS)
        conv = jnp.concatenate([conv_scr[j, rows] for j in range(n_conv)], axis=1)
        ya = _dot(att_scr[rows], wa_ref[:, cols])
        yc = _dot(conv, wc_ref[:, cols])
        merged_scr[n, rows] = (g_scr[n, rows] * ya + g_scr[n_out + n, rows] * yc).astype(BF16)

    def out_proj(lo, hi):
        rows = slice(lo, hi)
        return _dot(jnp.concatenate([merged_scr[n, rows] for n in range(n_out)], axis=1), wo_ref[...])

    def finish(lo, mix, blocks):
        for blk in blocks:
            r = lo + blk * LN_ROWS
            out_ref[0, r:r + LN_ROWS] = _layer_norm(
                DEEPNORM_ALPHA * hres[r:r + LN_ROWS] + mix[blk * LN_ROWS:(blk + 1) * LN_ROWS],
                g2_ref[...], b2_ref[...])

    att_units = [(qb, pair) for qb in range(QB_PER_TILE) for pair in range(N_Q_HEADS // 2)]
    filler = proj_units + [functools.partial(merge, 0, n) for n in range(n_out)]
    assert len(filler) == len(att_units) and len(proj_units) >= len(att_units) // 2
    scores = att_scores(*att_units[0])
    for k, unit in enumerate(att_units):
        ahead = att_scores(*att_units[k + 1]) if k + 1 < len(att_units) else None
        probs, scale = att_softmax(*unit, scores)
        filler[k]()
        att_values(*unit, probs, scale)
        scores = ahead
    ln_blocks = half_rows // LN_ROWS
    mix = out_proj(0, half_rows)
    for n in range(n_out):
        finish(0, mix, range(n * ln_blocks // n_out, (n + 1) * ln_blocks // n_out))
        merge(1, n)
    quarter = half_rows // 2
    mix_a = out_proj(half_rows, half_rows + quarter)
    mix_b = out_proj(half_rows + quarter, TILE)
    finish(half_rows, mix_a, range(quarter // LN_ROWS))
    finish(half_rows + quarter, mix_b, range(quarter // LN_ROWS))


def _const_spec(shape):
    return pl.BlockSpec(shape, lambda b, i: (0,) * len(shape), pipeline_mode=pl.Buffered(1))


def _mixer(x, ln_g, ln_b, w_in, b_gates, sink, bias_by_rel, conv_w, conv_b, w_a, w_c, w_o, g2, b2,
           w_ffn_up, w_ffn_down):
    bsz, seq, _ = x.shape
    n_tiles = seq // TILE
    blocks_per_tile = TILE // BLOCK
    n_blocks = seq // BLOCK
    in_specs = [
        pl.BlockSpec((1, TILE, D_MODEL), lambda b, i: (b, i, 0)),
        pl.BlockSpec((1, BLOCK, D_MODEL),
                     lambda b, i: (b, jnp.minimum((i + 1) * blocks_per_tile, n_blocks - 1), 0)),
        _const_spec((1, D_MODEL)), _const_spec((1, D_MODEL)),
        pl.BlockSpec(memory_space=pl.ANY),
        _const_spec((1, 2 * D_MODEL)),
        pl.BlockSpec(memory_space=pltpu.SMEM),
        _const_spec((N_Q_HEADS, 4 * BLOCK)),
        _const_spec((1, 3, CONV_WIDTH)), _const_spec((1, CONV_WIDTH)),
        pl.BlockSpec(memory_space=pl.ANY), pl.BlockSpec(memory_space=pl.ANY),
        pl.BlockSpec(memory_space=pl.ANY),
        _const_spec((1, D_MODEL)), _const_spec((1, D_MODEL)),
    ]
    n_steps = bsz * n_tiles
    cast_specs, cast_shapes = [], []
    for w in (w_ffn_up, w_ffn_down):
        rows = next(r for r in range(BF16_SUBLANES, w.shape[0] + 1, BF16_SUBLANES)
                    if w.shape[0] % r == 0 and w.shape[0] // r <= n_steps)
        last = w.shape[0] // rows - 1
        cast_specs.append(pl.BlockSpec(
            (rows, w.shape[1]), lambda b, i, last=last: (jnp.minimum(b * n_tiles + i, last), 0)))
        cast_shapes.append(jax.ShapeDtypeStruct(w.shape, BF16))
    ext = TILE + 2 * BLOCK
    return pl.pallas_call(
        _mixer_kernel,
        out_shape=[jax.ShapeDtypeStruct((bsz, seq, D_MODEL), F32)] + cast_shapes,
        grid=(bsz, n_tiles),
        in_specs=in_specs + cast_specs,
        out_specs=[pl.BlockSpec((1, TILE, D_MODEL), lambda b, i: (b, i, 0))] + cast_specs,
        scratch_shapes=[
            pltpu.VMEM((ext, D_MODEL), BF16),
            pltpu.VMEM((TILE, D_MODEL), F32),
            pltpu.VMEM((TILE, ATT_WIDTH), BF16),
            pltpu.VMEM((4, ext, KV_WIDTH), BF16),
            pltpu.VMEM((4, ext, KV_WIDTH), BF16),
            pltpu.VMEM((TILE, ATT_WIDTH), BF16),
            pltpu.VMEM((CONV_WIDTH // LANES, ROW_PITCH * (TILE + 2 * CONV_HALO), LANES), F32),
            pltpu.VMEM((CONV_WIDTH // PROJ_COLS, TILE, PROJ_COLS), BF16),
            pltpu.VMEM((2 * D_MODEL // PROJ_COLS, TILE, PROJ_COLS), F32),
            pltpu.VMEM((D_MODEL // PROJ_COLS, TILE, PROJ_COLS), BF16),
            pltpu.VMEM((D_MODEL, IN_PROJ_WIDTH), BF16),
            pltpu.VMEM((ATT_WIDTH, D_MODEL), BF16),
            pltpu.VMEM((CONV_WIDTH, D_MODEL), BF16),
            pltpu.VMEM((D_MODEL, D_MODEL), BF16),
            pltpu.VMEM((2, STAGE_ROWS_WIDE, IN_PROJ_WIDTH), F32),
            pltpu.VMEM((2, STAGE_ROWS, D_MODEL), F32),
            pltpu.SemaphoreType.DMA((2,)),
            pltpu.VMEM((3, N_Q_HEADS, BLOCK, 3 * BLOCK), F32),
        ],
        compiler_params=pltpu.CompilerParams(
            dimension_semantics=("arbitrary", "arbitrary"), vmem_limit_bytes=VMEM_LIMIT_BYTES),
        name="mixer",
    )(x, x, ln_g, ln_b, w_in, b_gates, sink, bias_by_rel, conv_w, conv_b, w_a, w_c, w_o, g2, b2,
      w_ffn_up, w_ffn_down)


def _ffn_call(tiles_per_seq, h_hbm, wup_ref, cw_ref, cb_ref, wdn_ref, g_ref, b_ref, out_hbm,
              hext, a_scr, u_scr, act_scr, acc):
    bsz, seq, _ = h_hbm.shape
    halo_per_tile = TILE // CONV_HALO
    n_halo_blocks = seq // CONV_HALO

    def tile(hp_ref, hc_ref, hn_ref, out_ref):
        _ffn_kernel(pl.program_id(1), tiles_per_seq - 1, hp_ref, hc_ref, hn_ref, wup_ref, cw_ref, cb_ref,
                    wdn_ref, g_ref, b_ref, out_ref, hext, a_scr, u_scr, act_scr, acc)

    pltpu.emit_pipeline(
        tile,
        grid=(bsz, tiles_per_seq),
        in_specs=[
            pl.BlockSpec((1, CONV_HALO, D_MODEL),
                         lambda b, i: (b, jnp.maximum(i * halo_per_tile - 1, 0), 0)),
            pl.BlockSpec((1, TILE, D_MODEL), lambda b, i: (b, i, 0), pipeline_mode=pl.Buffered(3)),
            pl.BlockSpec((1, CONV_HALO, D_MODEL),
                         lambda b, i: (b, jnp.minimum((i + 1) * halo_per_tile, n_halo_blocks - 1), 0)),
        ],
        out_specs=[pl.BlockSpec((1, TILE, D_MODEL), lambda b, i: (b, i, 0))],
    )(h_hbm, h_hbm, h_hbm, out_hbm)


def _ffn_kernel(i, last_i, hp_ref, hc_ref, hn_ref, wup_ref, cw_ref, cb_ref, wdn_ref, g_ref, b_ref, out_ref,
                hext, a_scr, u_scr, act_scr, acc):

    hext[0:CONV_HALO] = jnp.where(i > 0, hp_ref[0], 0.0).astype(BF16)
    hext[CONV_HALO:CONV_HALO + TILE] = hc_ref[0].astype(BF16)
    hext[CONV_HALO + TILE:] = jnp.where(i < last_i, hn_ref[0], 0.0).astype(BF16)

    tile_half = TILE // 2
    ext_split = (0, tile_half + 2 * CONV_HALO, TILE + 2 * CONV_HALO)

    def up_proj(c, half):
        col = c * FF_CHUNK
        start, stop = ext_split[half], ext_split[half + 1]
        dst = pl.ds(ROW_PITCH * start, stop - start, stride=ROW_PITCH)
        for scr, off in ((a_scr, col), (u_scr, D_FF + col)):
            res = _dot(hext[start:stop], wup_ref[:, off:off + FF_CHUNK])
            for slab in range(FF_CHUNK // LANES):
                scr[c % 2, slab, dst, :] = res[:, slab * LANES:(slab + 1) * LANES]

    def down_proj(c, half):
        rows = slice(half * tile_half, (half + 1) * tile_half)
        part = _dot(act_scr[c % 2, rows], wdn_ref[c * FF_CHUNK:(c + 1) * FF_CHUNK, :])
        if c == 0:
            acc[rows] = part
        else:
            acc[rows] += part

    def finish(half):
        rows = slice(half * tile_half, (half + 1) * tile_half)
        out_ref[0, rows] = _layer_norm(DEEPNORM_ALPHA * hc_ref[0, rows] + acc[rows], g_ref[...], b_ref[...])

    def conv_taps(col):
        out = []
        for slab in range(FF_CHUNK // LANES):
            lanes = slice(col + slab * LANES, col + (slab + 1) * LANES)
            w = cw_ref[0, :, lanes]
            out.append([jnp.broadcast_to(w[k:k + 1], (FF_ROWS, LANES)) for k in range(3)]
                       + [jnp.broadcast_to(cb_ref[:, lanes], (FF_ROWS, LANES))])
        return out

    def conv3(scr, slot, slab, r0, taps):
        rows = lambda r: scr[slot, slab, pl.ds(ROW_PITCH * r, FF_ROWS, stride=ROW_PITCH), :]
        return rows(r0 - 1) * taps[0] + rows(r0) * taps[1] + rows(r0 + 1) * taps[2] + taps[3]

    def gate_rows(c, rb, a_taps, u_taps):
        r0 = CONV_HALO + rb * FF_ROWS
        parts = []
        for slab in range(FF_CHUNK // LANES):
            a = conv3(a_scr, c % 2, slab, r0, a_taps[slab])
            u = conv3(u_scr, c % 2, slab, r0, u_taps[slab])
            parts.append(a * _sigmoid(a) * u)
        act_scr[c % 2, rb * FF_ROWS:(rb + 1) * FF_ROWS] = jnp.concatenate(parts, axis=1).astype(BF16)

    units = [(c, half) for c in range(N_FF_CHUNKS) for half in range(2)]
    blocks_per_half = tile_half // FF_ROWS
    up_proj(*units[0])
    taps = {}
    for k, (c, half) in enumerate(units):
        if half == 0:
            taps = {"a": conv_taps(c * FF_CHUNK), "u": conv_taps(D_FF + c * FF_CHUNK)}
        mxu_units = []
        if k + 1 < len(units):
            mxu_units.append(functools.partial(up_proj, *units[k + 1]))
        if k >= 1:
            mxu_units.append(functools.partial(down_proj, *units[k - 1]))
        per_unit = blocks_per_half // len(mxu_units)
        for m, unit in enumerate(mxu_units):
            unit()
            for rb in range(m * per_unit, (m + 1) * per_unit):
                gate_rows(c, half * blocks_per_half + rb, taps["a"], taps["u"])
    finish(0)
    down_proj(*units[-1])
    finish(1)


def _ffn(h, w_up, conv_w, conv_b, w_down, g, b):
    bsz, seq, _ = h.shape
    whole = pl.BlockSpec(memory_space=pltpu.VMEM)
    in_specs = [
        pl.BlockSpec(memory_space=pl.ANY),
        whole,
        whole, whole,
        whole,
        whole, whole,
    ]
    ext = TILE + 2 * CONV_HALO
    return pl.pallas_call(
        functools.partial(_ffn_call, seq // TILE),
        out_shape=jax.ShapeDtypeStruct((bsz, seq, D_MODEL), F32),
        in_specs=in_specs,
        out_specs=pl.BlockSpec(memory_space=pl.ANY),
        scratch_shapes=[
            pltpu.VMEM((ext, D_MODEL), BF16),
            pltpu.VMEM((2, FF_CHUNK // LANES, ROW_PITCH * ext, LANES), F32),
            pltpu.VMEM((2, FF_CHUNK // LANES, ROW_PITCH * ext, LANES), F32),
            pltpu.VMEM((2, TILE, FF_CHUNK), BF16),
            pltpu.VMEM((TILE, D_MODEL), F32),
        ],
        compiler_params=pltpu.CompilerParams(vmem_limit_bytes=VMEM_LIMIT_BYTES),
        name="ffn",
    )(h, w_up, conv_w, conv_b, w_down, g, b)


def kernel(x, ln_in_g, ln_in_b, w_in, b_gates, attn_sink, rel_bias, conv_w, conv_b, w_att_branch,
           w_conv_branch, w_o, ln_mix_g, ln_mix_b, w_ffn_up, ffn_conv_w, ffn_conv_b, w_ffn_down,
           ln_ffn_g, ln_ffn_b):
    assert w_in.shape[0] == DEPTH == 1
    row = lambda v: v.reshape(1, -1)
    h, w_up, w_down = _mixer(x, row(ln_in_g), row(ln_in_b), w_in[0], row(b_gates[0]), attn_sink[0],
                             _bias_by_rel(rel_bias), conv_w, row(conv_b[0]), w_att_branch[0],
                             w_conv_branch[0], w_o[0], row(ln_mix_g[0]), row(ln_mix_b[0]),
                             w_ffn_up[0], w_ffn_down[0])
    return _ffn(h, w_up, ffn_conv_w, row(ffn_conv_b[0]), w_down, row(ln_ffn_g[0]), row(ln_ffn_b[0]))
```
